```python
import jax, jax.numpy as jnp
from jax import lax
import numpy as np

D_MODEL = 2048
BATCH = 4
SEQ = 2048
DEPTH = 1
DEC_BATCH = 32
DEC_SEQ = 1
PAST_LEN = 8192
PAGE_SIZE = 128

D_MIX = D_MODEL
D_ATT = D_MIX // 2
HD_ATT = 64
H_ATT = D_ATT // HD_ATT
DIL_PATTERNS = ((128, 1), (512, 4), (2048, 16))
WIN_MAX = max(w for w, _ in DIL_PATTERNS)
ATT_BLOCK = max(w // d for w, d in DIL_PATTERNS)
D_GLA_V = D_MIX - D_ATT
H_GLA = 4
D_GLA_K = D_GLA_V // 2
DK_GLA = D_GLA_K // H_GLA
DV_GLA = D_GLA_V // H_GLA
GATE_RANK = 16
GATE_TAU = 16.0
GLA_CHUNK = 64
D_FF = ((8 * D_MODEL // 3 + 255) // 256) * 256
NORM_EPS = 1e-6
PROJ_SIZES = (D_ATT, D_ATT, D_ATT, D_GLA_K, D_GLA_K, D_GLA_V, D_GLA_V, GATE_RANK)
D_PROJ = sum(PROJ_SIZES)

kernel_name = "hymba_dilated_gla_macaron_step"


def _rms_f32(x):
    xf = x.astype(jnp.float32)
    return xf * lax.rsqrt(jnp.mean(xf * xf, axis=-1, keepdims=True) + NORM_EPS)


def rmsnorm(x, g):
    return (_rms_f32(x) * g.astype(jnp.float32)).astype(x.dtype)


def swiglu(h, w_in, w_out):
    a, b = jnp.split(h @ w_in, 2, axis=-1)
    return (jax.nn.silu(a) * b) @ w_out


def project(h, w_in, w_gate2, b_gate2):
    B, T, _ = h.shape
    idx = [int(i) for i in np.cumsum(PROJ_SIZES)[:-1]]
    qa, ka, va, qg, kg, vg, rg, lr = jnp.split(h @ w_in, idx, axis=-1)
    qa = qa.reshape(B, T, H_ATT, HD_ATT)
    ka = ka.reshape(B, T, H_ATT, HD_ATT)
    va = va.reshape(B, T, H_ATT, HD_ATT)
    qg = qg.astype(jnp.float32).reshape(B, T, H_GLA, DK_GLA) * (DK_GLA ** -0.5)
    kg = kg.astype(jnp.float32).reshape(B, T, H_GLA, DK_GLA)
    vg = vg.astype(jnp.float32).reshape(B, T, H_GLA, DV_GLA)
    log_a = jax.nn.log_sigmoid((lr @ w_gate2 + b_gate2).astype(jnp.float32)) / GATE_TAU
    log_a = log_a.reshape(B, T, H_GLA, DK_GLA)
    return qa, ka, va, qg, kg, vg, rg, log_a


def dilated_attn_prompt(q, k, v, window, dil):
    B, S, H, Dh = q.shape
    n_keys = window // dil
    L = -(-S // dil)
    L = -(-L // ATT_BLOCK) * ATT_BLOCK
    Sp = L * dil
    nb = L // ATT_BLOCK

    def to_streams(a):
        a = jnp.pad(a, ((0, 0), (0, Sp - S), (0, 0), (0, 0)))
        a = a.reshape(B, L, dil, H, Dh).transpose(0, 2, 1, 3, 4)
        return a.reshape(B, dil, nb, ATT_BLOCK, H, Dh)

    def with_prev(a):
        prev = jnp.pad(a, ((0, 0), (0, 0), (1, 0), (0, 0), (0, 0), (0, 0)))[:, :, :-1]
        return jnp.concatenate([prev, a], axis=3)

    qs = to_streams(q)
    kk = with_prev(to_streams(k))
    vv = with_prev(to_streams(v))
    s = jnp.einsum("brnqhd,brnkhd->brnhqk", qs, kk) * (Dh ** -0.5)
    qi = jnp.arange(ATT_BLOCK)[:, None]
    ki = jnp.arange(2 * ATT_BLOCK)[None, :]
    dist = qi + ATT_BLOCK - ki
    band = (dist >= 0) & (dist <= n_keys)
    blk = jnp.arange(nb)[:, None, None]
    exists = (blk * ATT_BLOCK + ki[None] - ATT_BLOCK) >= 0
    mask = band[None] & exists
    s = jnp.where(mask[:, None], s, -jnp.inf)
    m = jnp.max(s, axis=-1, keepdims=True)
    e = jnp.exp(s - m)
    den = jnp.sum(e, axis=-1, keepdims=True)
    o = jnp.einsum("brnhqk,brnkhd->brnqhd", e / den, vv)
    lse = (m + jnp.log(den))[..., 0].transpose(0, 1, 2, 4, 3)
    o = o.reshape(B, dil, L, H, Dh).transpose(0, 2, 1, 3, 4).reshape(B, Sp, H, Dh)[:, :S]
    lse = lse.reshape(B, dil, L, H).transpose(0, 2, 1, 3).reshape(B, Sp, H)[:, :S]
    return o, lse


def dilated_attn_sample(q, kc, vc, window, dil):
    B, T, H, Dh = q.shape
    WB = kc.shape[1] - T
    j = jnp.arange(window // dil + 1)
    idx = WB + jnp.arange(T)[:, None] - dil * j[None, :]
    valid = idx >= 0
    idx = jnp.maximum(idx, 0)
    kg = kc[:, idx]
    vg = vc[:, idx]
    s = jnp.einsum("bthd,btjhd->bthj", q, kg) * (Dh ** -0.5)
    s = jnp.where(valid[None, :, None, :], s, -jnp.inf)
    m = jnp.max(s, axis=-1, keepdims=True)
    e = jnp.exp(s - m)
    den = jnp.sum(e, axis=-1, keepdims=True)
    o = jnp.einsum("bthj,btjhd->bthd", e / den, vg)
    return o, (m + jnp.log(den))[..., 0]


def combine_by_denominator(outs, lses):
    w = jax.nn.softmax(jnp.stack(lses, axis=0), axis=0)
    return jnp.sum(w[..., None] * jnp.stack(outs, axis=0), axis=0)


def dilated_mix_prompt(q, k, v):
    q, k, v = (a.astype(jnp.float32) for a in (q, k, v))
    outs, lses = [], []
    for window, dil in DIL_PATTERNS:
        o, lse = dilated_attn_prompt(q, k, v, window, dil)
        outs.append(o)
        lses.append(lse)
    return combine_by_denominator(outs, lses)


def dilated_mix_sample(q, kc, vc):
    q, kc, vc = (a.astype(jnp.float32) for a in (q, kc, vc))
    outs, lses = [], []
    for window, dil in DIL_PATTERNS:
        o, lse = dilated_attn_sample(q, kc, vc, window, dil)
        outs.append(o)
        lses.append(lse)
    return combine_by_denominator(outs, lses)


def gla_chunked(q, k, v, log_a, s0):
    B, T, H, K = q.shape
    C = GLA_CHUNK
    nc = T // C
    causal = jnp.tril(jnp.ones((C, C), dtype=bool))

    def to_chunks(a):
        return a.reshape(B, nc, C, H, a.shape[-1]).transpose(1, 0, 3, 2, 4)

    def step(S, inp):
        qc, kc, vc, gc = inp
        b = jnp.cumsum(gc, axis=2)
        o_inter = jnp.einsum("bhck,bhkv->bhcv", qc * jnp.exp(b), S)
        diff = b[:, :, :, None, :] - b[:, :, None, :, :]
        decay = jnp.exp(jnp.where(causal[:, :, None], diff, -jnp.inf))
        A = jnp.einsum("bhtk,bhsk,bhtsk->bhts", qc, kc, decay)
        o = o_inter + jnp.einsum("bhts,bhsv->bhtv", A, vc)
        bl = b[:, :, -1:, :]
        S = jnp.exp(bl[:, :, 0, :])[..., None] * S + jnp.einsum("bhsk,bhsv->bhkv", kc * jnp.exp(bl - b), vc)
        return S, o

    S, o = lax.scan(step, s0, (to_chunks(q), to_chunks(k), to_chunks(v), to_chunks(log_a)))
    o = o.transpose(1, 0, 3, 2, 4).reshape(B, T, H, v.shape[-1])
    return o, S


def gla_recurrent(q, k, v, log_a, s0):
    def step(S, inp):
        qt, kt, vt, gt = inp
        S = jnp.exp(gt)[..., None] * S + kt[..., :, None] * vt[..., None, :]
        return S, jnp.einsum("bhk,bhkv->bhv", qt, S)

    xs = tuple(a.transpose(1, 0, 2, 3) for a in (q, k, v, log_a))
    S, o = lax.scan(step, s0, xs)
    return o.transpose(1, 0, 2, 3), S


def merge(o_att, o_gla, r_g, g_att_out, g_gla_out, w_out, dtype):
    B, T = o_att.shape[:2]
    a = _rms_f32(o_att.reshape(B, T, D_ATT)) * g_att_out.astype(jnp.float32)
    gg = _rms_f32(o_gla) * g_gla_out.astype(jnp.float32)
    gg = gg.reshape(B, T, D_GLA_V) * jax.nn.silu(r_g.astype(jnp.float32))
    return jnp.concatenate([a, gg], axis=-1).astype(dtype) @ w_out


def setup_inputs(seed: int = 0) -> dict:
    key = jax.random.key(seed)
    ks = jax.random.split(key, 24)
    f32 = jnp.float32
    WB = min(WIN_MAX, PAST_LEN)
    nrm = lambda k, shape, s: jax.random.normal(k, shape, f32) * s
    gain = lambda k, shape: 1.0 + 0.02 * jax.random.normal(k, shape, f32)
    return {
        "x_prompt": nrm(ks[0], (BATCH, SEQ, D_MODEL), 1.0),
        "x_sample": nrm(ks[1], (DEC_BATCH, DEC_SEQ, D_MODEL), 1.0),
        "cache_att_k": nrm(ks[2], (DEPTH, DEC_BATCH, WB, H_ATT, HD_ATT), 1.0),
        "cache_att_v": nrm(ks[3], (DEPTH, DEC_BATCH, WB, H_ATT, HD_ATT), 1.0),
        "state_gla": nrm(ks[4], (DEPTH, DEC_BATCH, H_GLA, DK_GLA, DV_GLA), 2.0),
        "g_ffn1": gain(ks[5], (DEPTH, D_MODEL)),
        "w_ffn1_in": nrm(ks[6], (DEPTH, D_MODEL, 2 * D_FF), D_MODEL ** -0.5),
        "w_ffn1_out": nrm(ks[7], (DEPTH, D_FF, D_MODEL), D_FF ** -0.5),
        "g_mix": gain(ks[8], (DEPTH, D_MODEL)),
        "w_in": nrm(ks[9], (DEPTH, D_MODEL, D_PROJ), D_MODEL ** -0.5),
        "w_gate2": nrm(ks[10], (DEPTH, GATE_RANK, D_GLA_K), GATE_RANK ** -0.5),
        "b_gate2": nrm(ks[11], (DEPTH, D_GLA_K), 0.1),
        "g_att_out": gain(ks[12], (DEPTH, D_ATT)),
        "g_gla_out": gain(ks[13], (DEPTH, DV_GLA)),
        "w_out": nrm(ks[14], (DEPTH, D_MIX, D_MODEL), D_MIX ** -0.5),
        "g_ffn2": gain(ks[15], (DEPTH, D_MODEL)),
        "w_ffn2_in": nrm(ks[16], (DEPTH, D_MODEL, 2 * D_FF), D_MODEL ** -0.5),
        "w_ffn2_out": nrm(ks[17], (DEPTH, D_FF, D_MODEL), D_FF ** -0.5),
        "g_final": gain(ks[18], (D_MODEL,)),
    }


def reference(x_prompt, x_sample, cache_att_k, cache_att_v, state_gla, g_ffn1, w_ffn1_in, w_ffn1_out,
              g_mix, w_in, w_gate2, b_gate2, g_att_out, g_gla_out, w_out, g_ffn2, w_ffn2_in, w_ffn2_out,
              g_final):
    xp, xs = x_prompt, x_sample
    S_p = xp.shape[1]
    T_s = xs.shape[1]
    wb_p = min(WIN_MAX, S_p)
    nk_p, nv_p, ns_p, nk_s, nv_s, ns_s = [], [], [], [], [], []
    for l in range(DEPTH):
        xp = xp + 0.5 * swiglu(rmsnorm(xp, g_ffn1[l]), w_ffn1_in[l], w_ffn1_out[l])
        xs = xs + 0.5 * swiglu(rmsnorm(xs, g_ffn1[l]), w_ffn1_in[l], w_ffn1_out[l])

        qa, ka, va, qg, kg, vg, rg, la = project(rmsnorm(xp, g_mix[l]), w_in[l], w_gate2[l], b_gate2[l])
        o_att = dilated_mix_prompt(qa, ka, va)
        s0 = jnp.zeros((xp.shape[0], H_GLA, DK_GLA, DV_GLA), jnp.float32)
        o_gla, s_fin = gla_chunked(qg, kg, vg, la, s0)
        xp = xp + merge(o_att, o_gla, rg, g_att_out[l], g_gla_out[l], w_out[l], xp.dtype)
        nk_p.append(ka[:, S_p - wb_p:])
        nv_p.append(va[:, S_p - wb_p:])
        ns_p.append(s_fin.astype(xp.dtype))

        qa, ka, va, qg, kg, vg, rg, la = project(rmsnorm(xs, g_mix[l]), w_in[l], w_gate2[l], b_gate2[l])
        kc = jnp.concatenate([cache_att_k[l].astype(ka.dtype), ka], axis=1)
        vc = jnp.concatenate([cache_att_v[l].astype(va.dtype), va], axis=1)
        o_att = dilated_mix_sample(qa, kc, vc)
        o_gla, s_new = gla_recurrent(qg, kg, vg, la, state_gla[l].astype(jnp.float32))
        xs = xs + merge(o_att, o_gla, rg, g_att_out[l], g_gla_out[l], w_out[l], xs.dtype)
        nk_s.append(ka)
        nv_s.append(va)
        ns_s.append(s_new.astype(state_gla.dtype))

        xp = xp + 0.5 * swiglu(rmsnorm(xp, g_ffn2[l]), w_ffn2_in[l], w_ffn2_out[l])
        xs = xs + 0.5 * swiglu(rmsnorm(xs, g_ffn2[l]), w_ffn2_in[l], w_ffn2_out[l])

    y_prompt = rmsnorm(xp, g_final)
    y_sample = rmsnorm(xs, g_final)
    return (y_prompt, y_sample, jnp.stack(nk_p), jnp.stack(nv_p), jnp.stack(ns_p),
            jnp.stack(nk_s), jnp.stack(nv_s), jnp.stack(ns_s))
```

```python
import functools

import numpy as np
import jax
import jax.numpy as jnp
from jax import lax
from jax.experimental import pallas as pl
from jax.experimental.pallas import tpu as pltpu

F32 = jnp.float32
BF16 = jnp.bfloat16

D_MODEL = 2048
D_FF = 5632
D_ATT = 1024
HD_ATT = 64
H_ATT = 16
H_GLA = 4
DK_GLA = 128
DV_GLA = 256
D_GLA_K = H_GLA * DK_GLA
D_GLA_V = H_GLA * DV_GLA
GATE_RANK = 16
GATE_TAU = 16.0
NORM_EPS = 1e-6
ATT_BLOCK = 128
GLA_CHUNK = 64
D_PROJ_MAIN = 3 * D_ATT + 2 * D_GLA_K + 2 * D_GLA_V

COL_QA, COL_KA, COL_VA = 0, D_ATT, 2 * D_ATT
COL_QG = 3 * D_ATT
COL_KG = COL_QG + D_GLA_K
COL_VG = COL_KG + D_GLA_K
COL_RG = COL_VG + D_GLA_V

VMEM_LIMIT_BYTES = 56 * 1024 * 1024


def _rms(x):
    return x * lax.rsqrt(jnp.mean(x * x, axis=-1, keepdims=True) + NORM_EPS)


def _dot(a, b):
    return jnp.dot(a, b, preferred_element_type=F32)


def _dot_nt(a, b):
    return lax.dot_general(a, b, (((1,), (1,)), ((), ())), preferred_element_type=F32)


def _dot_tn(a, b):
    return lax.dot_general(a, b, (((0,), (0,)), ((), ())), preferred_element_type=F32)


def _params(*sem):
    return pltpu.CompilerParams(dimension_semantics=sem, vmem_limit_bytes=VMEM_LIMIT_BYTES)


def _ffn_kernel(x_ref, g_ref, wa_ref, wb_ref, wo_ref, gf_ref, o_ref, h_ref, acc_ref, *, final_norm):
    j = pl.program_id(1)

    @pl.when(j == 0)
    def _():
        h_ref[...] = (_rms(x_ref[...]) * g_ref[...]).astype(BF16)
        acc_ref[...] = jnp.zeros_like(acc_ref)

    h = h_ref[...]
    a = _dot(h, wa_ref[...])
    b = _dot(h, wb_ref[...])
    act = (a * jax.nn.sigmoid(a) * b).astype(BF16)
    acc_ref[...] += _dot(act, wo_ref[...])

    @pl.when(j == pl.num_programs(1) - 1)
    def _():
        y = x_ref[...] + 0.5 * acc_ref[...]
        if final_norm:
            y = _rms(y) * gf_ref[...]
        o_ref[...] = y


def _ffn(x, g, w_in, w_out, g_final, *, tm, tf, final_norm):
    m = x.shape[0]
    nf = D_FF // tf
    return pl.pallas_call(
        functools.partial(_ffn_kernel, final_norm=final_norm),
        grid=(m // tm, nf),
        in_specs=[
            pl.BlockSpec((tm, D_MODEL), lambda i, j: (i, 0)),
            pl.BlockSpec((1, D_MODEL), lambda i, j: (0, 0)),
            pl.BlockSpec((D_MODEL, tf), lambda i, j: (0, j)),
            pl.BlockSpec((D_MODEL, tf), lambda i, j: (0, j + nf)),
            pl.BlockSpec((tf, D_MODEL), lambda i, j: (j, 0)),
            pl.BlockSpec((1, D_MODEL), lambda i, j: (0, 0)),
        ],
        out_specs=pl.BlockSpec((tm, D_MODEL), lambda i, j: (i, 0)),
        out_shape=jax.ShapeDtypeStruct((m, D_MODEL), F32),
        scratch_shapes=[pltpu.VMEM((tm, D_MODEL), BF16), pltpu.VMEM((tm, D_MODEL), F32)],
        compiler_params=_params("parallel", "arbitrary"),
        name="ffn",
    )(x, g, w_in, w_in, w_out, g_final)


def _proj_kernel(x_ref, g_ref, w_ref, wlr_ref, wg2_ref, bg2_ref, p_ref, la_ref, h_ref):
    j = pl.program_id(1)

    @pl.when(j == 0)
    def _():
        h = (_rms(x_ref[...]) * g_ref[...]).astype(BF16)
        h_ref[...] = h
        lr = _dot(h, wlr_ref[...])
        z = _dot(lr.astype(BF16), wg2_ref[...]) + bg2_ref[...]
        log_sig = jnp.minimum(z, 0.0) - jnp.log1p(jnp.exp(-jnp.abs(z)))
        la_ref[...] = log_sig * (1.0 / GATE_TAU)

    p_ref[...] = _dot(h_ref[...], w_ref[...])


def _proj(x, g, w_main, w_lr, w_g2, b_g2, *, tm, tn):
    m = x.shape[0]
    return pl.pallas_call(
        _proj_kernel,
        grid=(m // tm, D_PROJ_MAIN // tn),
        in_specs=[
            pl.BlockSpec((tm, D_MODEL), lambda i, j: (i, 0)),
            pl.BlockSpec((1, D_MODEL), lambda i, j: (0, 0)),
            pl.BlockSpec((D_MODEL, tn), lambda i, j: (0, j)),
            pl.BlockSpec((D_MODEL, GATE_RANK), lambda i, j: (0, 0)),
            pl.BlockSpec((GATE_RANK, D_GLA_K), lambda i, j: (0, 0)),
            pl.BlockSpec((1, D_GLA_K), lambda i, j: (0, 0)),
        ],
        out_specs=[
            pl.BlockSpec((tm, tn), lambda i, j: (i, j)),
            pl.BlockSpec((tm, D_GLA_K), lambda i, j: (i, 0)),
        ],
        out_shape=[
            jax.ShapeDtypeStruct((m, D_PROJ_MAIN), F32),
            jax.ShapeDtypeStruct((m, D_GLA_K), F32),
        ],
        scratch_shapes=[pltpu.VMEM((tm, D_MODEL), BF16)],
        compiler_params=_params("parallel", "arbitrary"),
        name="proj",
    )(x, g, w_main, w_lr, w_g2, b_g2)


def _attn_block(q, kk, vv, bias, lane_lo):
    res = []
    for sel in (lane_lo, jnp.logical_not(lane_lo)):
        qh = jnp.where(sel, q, 0.0).astype(BF16)
        s = _dot_nt(qh, kk) + bias
        m = jnp.max(s, axis=-1, keepdims=True)
        e = jnp.exp(s - m)
        l = jnp.sum(e, axis=-1, keepdims=True)
        u = _dot(e.astype(BF16), vv)
        res.append((m, l, u))
    shape = q.shape
    m = jnp.where(lane_lo, jnp.broadcast_to(res[0][0], shape), jnp.broadcast_to(res[1][0], shape))
    l = jnp.where(lane_lo, jnp.broadcast_to(res[0][1], shape), jnp.broadcast_to(res[1][1], shape))
    u = jnp.where(lane_lo, res[0][2], res[1][2])
    return m, l, u


def _attn_kernel(q_ref, k_ref, v_ref, o_ref, m_ref, l_ref, acc_ref):
    blk = ATT_BLOCK
    scale = HD_ATT ** -0.5
    lane_lo = lax.broadcasted_iota(jnp.int32, (1, 2 * HD_ATT), 1) < HD_ATT
    qi = lax.broadcasted_iota(jnp.int32, (blk, 2 * blk), 0)
    ki = lax.broadcasted_iota(jnp.int32, (blk, 2 * blk), 1)
    neg = jnp.float32(-jnp.inf)
    cur_ok = (ki >= blk) & (ki - blk <= qi)
    prev_ok = (ki < blk) & (ki >= qi)
    bias_band = jnp.where(cur_ok | prev_ok, 0.0, neg)
    bias_first = jnp.where(cur_ok, 0.0, neg)
    qi1 = lax.broadcasted_iota(jnp.int32, (blk, blk), 0)
    ki1 = lax.broadcasted_iota(jnp.int32, (blk, blk), 1)
    bias_causal = jnp.where(ki1 <= qi1, 0.0, neg)

    def rows(start, size, stride):
        if stride == 1:
            return pl.ds(start, size)
        return pl.ds(start, size, stride=stride)

    def banded(q_start, prev_start, has_prev, stride, first_pattern):
        q_rows = rows(q_start, blk, stride)
        q = q_ref[q_rows, :] * scale
        kk = jnp.concatenate([k_ref[rows(prev_start, blk, stride), :], k_ref[q_rows, :]], axis=0)
        vv = jnp.concatenate([v_ref[rows(prev_start, blk, stride), :], v_ref[q_rows, :]], axis=0)
        bias = jnp.where(has_prev, bias_band, bias_first)
        m, l, u = _attn_block(q, kk.astype(BF16), vv.astype(BF16), bias, lane_lo)
        merge(q_rows, m, l, u, first_pattern)

    def merge(q_rows, m, l, u, first_pattern):
        if first_pattern:
            m_ref[q_rows, :] = m
            l_ref[q_rows, :] = l
            acc_ref[q_rows, :] = u
        else:
            m_old = m_ref[q_rows, :]
            m_new = jnp.maximum(m_old, m)
            a_old = jnp.exp(m_old - m_new)
            a_blk = jnp.exp(m - m_new)
            m_ref[q_rows, :] = m_new
            l_ref[q_rows, :] = a_old * l_ref[q_rows, :] + a_blk * l
            acc_ref[q_rows, :] = a_old * acc_ref[q_rows, :] + a_blk * u

    def p1(n, c):
        q_start = pl.multiple_of(n * blk, blk)
        prev_start = pl.multiple_of(jnp.maximum(n - 1, 0) * blk, blk)
        banded(q_start, prev_start, n > 0, 1, True)
        return c

    lax.fori_loop(0, 16, p1, 0)

    def p2(i, c):
        r = i % 4
        n = i // 4
        q_start = r + n * (4 * blk)
        prev_start = r + jnp.maximum(n - 1, 0) * (4 * blk)
        banded(q_start, prev_start, n > 0, 4, False)
        return c

    lax.fori_loop(0, 16, p2, 0)

    def p3(r, c):
        q_rows = rows(r, blk, 16)
        q = q_ref[q_rows, :] * scale
        kk = k_ref[q_rows, :].astype(BF16)
        vv = v_ref[q_rows, :].astype(BF16)
        m, l, u = _attn_block(q, kk, vv, bias_causal, lane_lo)
        merge(q_rows, m, l, u, False)
        return c

    lax.fori_loop(0, 16, p3, 0)

    o_ref[...] = acc_ref[...] / l_ref[...]


def _attn_prompt(p, batch, seq):
    lanes = 2 * HD_ATT
    n_pairs = H_ATT // 2
    return pl.pallas_call(
        _attn_kernel,
        grid=(batch, n_pairs),
        in_specs=[
            pl.BlockSpec((seq, lanes), lambda b, h: (b, COL_QA // lanes + h)),
            pl.BlockSpec((seq, lanes), lambda b, h: (b, COL_KA // lanes + h)),
            pl.BlockSpec((seq, lanes), lambda b, h: (b, COL_VA // lanes + h)),
        ],
        out_specs=pl.BlockSpec((seq, lanes), lambda b, h: (b, h)),
        out_shape=jax.ShapeDtypeStruct((batch * seq, D_ATT), F32),
        scratch_shapes=[pltpu.VMEM((seq, lanes), F32)] * 3,
        compiler_params=_params("parallel", "parallel"),
        name="attn_prompt",
    )(p, p, p)


_GLA_LEVELS = (32, 16, 8, 4, 2, 1)


def _gla_exponent_matrix():
    c = GLA_CHUNK
    t = np.arange(c)[:, None]
    u = np.arange(c)[None, :]
    mats = [(u <= t), (u > t)]
    for h in _GLA_LEVELS:
        mid = (t // (2 * h)) * (2 * h) + h - 1
        upper = (t % (2 * h)) >= h
        mats.append(np.where(upper, (u > mid) & (u <= t), (u > t) & (u <= mid)))
    tmat = np.concatenate(mats, axis=0).astype(np.float32)
    return np.concatenate([tmat, tmat], axis=1)


def _gla_kernel(q_ref, k_ref, v_ref, g_ref, t_ref, o_ref, s_ref, st_ref):
    c = GLA_CHUNK
    scale = DK_GLA ** -0.5
    ti = lax.broadcasted_iota(jnp.int32, (c, c), 0)
    si = lax.broadcasted_iota(jnp.int32, (c, c), 1)
    txs = ti ^ si
    below = ti > si
    level_masks = [below & (txs >= h) & (txs < 2 * h) for h in _GLA_LEVELS]
    diag = ti == si

    st_ref[...] = jnp.zeros_like(st_ref)

    def chunk(ci, carry):
        r = pl.ds(pl.multiple_of(ci * c, c), c)
        g = g_ref[r, :]
        g_hi = g.astype(BF16)
        g_lo = (g - g_hi.astype(F32)).astype(BF16)
        f = jnp.exp(_dot(t_ref[...], jnp.concatenate([g_hi, g_lo], axis=0)))
        q = q_ref[r, :] * scale
        k = k_ref[r, :]
        v = v_ref[r, :].astype(BF16)
        a = jnp.where(diag, _dot_nt(q.astype(BF16), k.astype(BF16)), 0.0)
        for lvl in range(len(_GLA_LEVELS)):
            fl = f[2 * c + lvl * c: 3 * c + lvl * c]
            a = a + jnp.where(level_masks[lvl],
                              _dot_nt((q * fl).astype(BF16), (k * fl).astype(BF16)), 0.0)
        st = st_ref[...]
        qe = (q * f[0:c]).astype(BF16)
        o_ref[r, :] = _dot_nt(qe, st.astype(BF16)) + _dot(a.astype(BF16), v)
        kd = (k * f[c:2 * c]).astype(BF16)
        st_ref[...] = st * f[c - 1:c] + _dot_tn(v, kd)
        return carry

    lax.fori_loop(0, q_ref.shape[0] // c, chunk, 0)
    s_ref[0, 0] = st_ref[...].T


def _gla_prompt(p, la, tmat, batch, seq):
    return pl.pallas_call(
        _gla_kernel,
        grid=(batch, H_GLA),
        in_specs=[
            pl.BlockSpec((seq, DK_GLA), lambda b, h: (b, COL_QG // DK_GLA + h)),
            pl.BlockSpec((seq, DK_GLA), lambda b, h: (b, COL_KG // DK_GLA + h)),
            pl.BlockSpec((seq, DV_GLA), lambda b, h: (b, COL_VG // DV_GLA + h)),
            pl.BlockSpec((seq, DK_GLA), lambda b, h: (b, h)),
            pl.BlockSpec(tmat.shape, lambda b, h: (0, 0)),
        ],
        out_specs=[
            pl.BlockSpec((seq, DV_GLA), lambda b, h: (b, h)),
            pl.BlockSpec((1, 1, DK_GLA, DV_GLA), lambda b, h: (b, h, 0, 0)),
        ],
        out_shape=[
            jax.ShapeDtypeStruct((batch * seq, D_GLA_V), F32),
            jax.ShapeDtypeStruct((batch, H_GLA, DK_GLA, DV_GLA), F32),
        ],
        scratch_shapes=[pltpu.VMEM((DV_GLA, DK_GLA), F32)],
        compiler_params=_params("parallel", "parallel"),
        name="gla_prompt",
    )(p, p, p, la, tmat)


def _merge_kernel(x_ref, oa_ref, og_ref, rg_ref, ga_ref, gg_ref, w_ref, o_ref):
    a = (_rms(oa_ref[...]) * ga_ref[...]).astype(BF16)
    parts = []
    for h in range(H_GLA):
        cols = slice(h * DV_GLA, (h + 1) * DV_GLA)
        r = rg_ref[:, cols]
        parts.append((_rms(og_ref[:, cols]) * gg_ref[...] * (r * jax.nn.sigmoid(r))).astype(BF16))
    gg = jnp.concatenate(parts, axis=-1)
    o_ref[...] = x_ref[...] + _dot(a, w_ref[0:D_ATT, :]) + _dot(gg, w_ref[D_ATT:, :])


def _merge(x, o_att, o_gla, p, g_att, g_gla, w_out, *, tm):
    m = x.shape[0]
    return pl.pallas_call(
        _merge_kernel,
        grid=(m // tm,),
        in_specs=[
            pl.BlockSpec((tm, D_MODEL), lambda i: (i, 0)),
            pl.BlockSpec((tm, D_ATT), lambda i: (i, 0)),
            pl.BlockSpec((tm, D_GLA_V), lambda i: (i, 0)),
            pl.BlockSpec((tm, D_GLA_V), lambda i: (i, COL_RG // D_GLA_V)),
            pl.BlockSpec((1, D_ATT), lambda i: (0, 0)),
            pl.BlockSpec((1, DV_GLA), lambda i: (0, 0)),
            pl.BlockSpec((D_MODEL, D_MODEL), lambda i: (0, 0)),
        ],
        out_specs=pl.BlockSpec((tm, D_MODEL), lambda i: (i, 0)),
        out_shape=jax.ShapeDtypeStruct((m, D_MODEL), F32),
        compiler_params=_params("parallel"),
        name="merge",
    )(x, o_att, o_gla, p, g_att, g_gla, w_out)


def _sattn_kernel(q_ref, kn_ref, vn_ref, k1_ref, k4_ref, k16_ref, v1_ref, v4_ref, v16_ref, o_ref):
    scale = HD_ATT ** -0.5
    head = lax.broadcasted_iota(jnp.int32, (H_ATT, D_ATT), 0)
    lane = lax.broadcasted_iota(jnp.int32, (H_ATT, D_ATT), 1)
    sel = (lane // HD_ATT) == head
    q = q_ref[0] * scale
    q_rows = jnp.where(sel, jnp.broadcast_to(q, sel.shape), 0.0)
    q_rows16 = q_rows.astype(BF16)
    s_new = jnp.sum(q_rows * kn_ref[0], axis=-1, keepdims=True)

    scores = []
    for k_ref in (k1_ref, k4_ref, k16_ref):
        scores.append(_dot_nt(q_rows16, k_ref[0].astype(BF16)))
    m = s_new
    for s in scores:
        m = jnp.maximum(m, jnp.max(s, axis=-1, keepdims=True))
    n_pat = len(scores)
    e_new = n_pat * jnp.exp(s_new - m)
    den = e_new
    num = e_new * vn_ref[0]
    for s, v_ref in zip(scores, (v1_ref, v4_ref, v16_ref)):
        e = jnp.exp(s - m)
        den = den + jnp.sum(e, axis=-1, keepdims=True)
        num = num + _dot(e.astype(BF16), v_ref[0].astype(BF16))
    o_full = num / den
    o_ref[0] = jnp.sum(jnp.where(sel, o_full, 0.0), axis=0, keepdims=True)


def _attn_sample(p3, cache_k, cache_v):
    nb, wb = cache_k.shape[0], cache_k.shape[1]
    row = H_ATT * HD_ATT
    blk = ATT_BLOCK

    def views(c):
        return [c.reshape(nb, wb // d, d * row) for d in (1, 4, 16)]

    def cache_spec(d):
        last = wb // d // blk - 1
        return pl.BlockSpec((1, blk, row), lambda b: (b, last, 0))

    cache_specs = [cache_spec(d) for d in (1, 4, 16)]
    return pl.pallas_call(
        _sattn_kernel,
        grid=(nb,),
        in_specs=[
            pl.BlockSpec((1, 1, row), lambda b: (b, 0, COL_QA // row)),
            pl.BlockSpec((1, 1, row), lambda b: (b, 0, COL_KA // row)),
            pl.BlockSpec((1, 1, row), lambda b: (b, 0, COL_VA // row)),
        ] + cache_specs + cache_specs,
        out_specs=pl.BlockSpec((1, 1, row), lambda b: (b, 0, 0)),
        out_shape=jax.ShapeDtypeStruct((nb, 1, row), F32),
        compiler_params=_params("parallel"),
        name="attn_sample",
    )(p3, p3, p3, *views(cache_k), *views(cache_v))


def _sgla_kernel(q_ref, k_ref, g_ref, v_ref, s_ref, so_ref, o_ref):
    scale = DK_GLA ** -0.5
    for h in range(H_GLA):
        s_new = jnp.exp(g_ref[0, h]) * s_ref[0, h] + k_ref[0, h] * v_ref[0, h]
        so_ref[0, h] = s_new
        o_ref[0, h] = jnp.sum((q_ref[0, h] * scale) * s_new, axis=0, keepdims=True)


def _gla_sample(q, k, g, v, state):
    nb = state.shape[0]
    col = pl.BlockSpec((1, H_GLA, DK_GLA, 1), lambda b: (b, 0, 0, 0))
    return pl.pallas_call(
        _sgla_kernel,
        grid=(nb,),
        in_specs=[col, col, col,
                  pl.BlockSpec((1, H_GLA, 1, DV_GLA), lambda b: (b, 0, 0, 0)),
                  pl.BlockSpec((1, H_GLA, DK_GLA, DV_GLA), lambda b: (b, 0, 0, 0))],
        out_specs=[pl.BlockSpec((1, H_GLA, DK_GLA, DV_GLA), lambda b: (b, 0, 0, 0)),
                   pl.BlockSpec((1, H_GLA, 1, DV_GLA), lambda b: (b, 0, 0, 0))],
        out_shape=[jax.ShapeDtypeStruct((nb, H_GLA, DK_GLA, DV_GLA), F32),
                   jax.ShapeDtypeStruct((nb, H_GLA, 1, DV_GLA), F32)],
        compiler_params=_params("parallel"),
        name="gla_sample",
    )(q, k, g, v, state)


def kernel(x_prompt, x_sample, cache_att_k, cache_att_v, state_gla, g_ffn1, w_ffn1_in, w_ffn1_out, g_mix, w_in, w_gate2, b_gate2, g_att_out, g_gla_out, w_out, g_ffn2, w_ffn2_in, w_ffn2_out, g_final):
    depth = w_in.shape[0]
    assert depth == 1
    batch, seq, _ = x_prompt.shape
    nb, dec_seq, _ = x_sample.shape
    assert dec_seq == 1
    xp = x_prompt.reshape(batch * seq, D_MODEL)
    xs = x_sample.reshape(nb, D_MODEL)
    row = lambda a: a.reshape(1, -1)
    l = 0

    w1i, w1o = w_ffn1_in[l].astype(BF16), w_ffn1_out[l].astype(BF16)
    w2i, w2o = w_ffn2_in[l].astype(BF16), w_ffn2_out[l].astype(BF16)
    w_main = w_in[l, :, :D_PROJ_MAIN].astype(BF16)
    w_lr = w_in[l, :, D_PROJ_MAIN:].astype(BF16)
    w_g2 = w_gate2[l].astype(BF16)
    b_g2 = row(b_gate2[l])
    w_o = w_out[l].astype(BF16)
    gf = row(g_final)
    tmat = jnp.asarray(_gla_exponent_matrix(), dtype=BF16)

    xp = _ffn(xp, row(g_ffn1[l]), w1i, w1o, gf, tm=512, tf=512, final_norm=False)
    pp, lap = _proj(xp, row(g_mix[l]), w_main, w_lr, w_g2, b_g2, tm=512, tn=512)
    o_att = _attn_prompt(pp, batch, seq)
    o_gla, s_fin = _gla_prompt(pp, lap, tmat, batch, seq)
    xp = _merge(xp, o_att, o_gla, pp, row(g_att_out[l]), row(g_gla_out[l]), w_o, tm=256)
    yp = _ffn(xp, row(g_ffn2[l]), w2i, w2o, gf, tm=512, tf=512, final_norm=True)
    nk_p = pp[:, COL_KA:COL_KA + D_ATT].reshape(1, batch, seq, H_ATT, HD_ATT)
    nv_p = pp[:, COL_VA:COL_VA + D_ATT].reshape(1, batch, seq, H_ATT, HD_ATT)

    xs = _ffn(xs, row(g_ffn1[l]), w1i, w1o, gf, tm=nb, tf=512, final_norm=False)
    ps, las = _proj(xs, row(g_mix[l]), w_main, w_lr, w_g2, b_g2, tm=nb, tn=512)
    o_att_s = _attn_sample(ps.reshape(nb, 1, D_PROJ_MAIN), cache_att_k[l], cache_att_v[l])
    col = lambda a: a.reshape(nb, H_GLA, DK_GLA, 1)
    s_new, o_gla_s = _gla_sample(
        col(ps[:, COL_QG:COL_QG + D_GLA_K]), col(ps[:, COL_KG:COL_KG + D_GLA_K]), col(las),
        ps[:, COL_VG:COL_VG + D_GLA_V].reshape(nb, H_GLA, 1, DV_GLA), state_gla[l])
    xs = _merge(xs, o_att_s.reshape(nb, D_ATT), o_gla_s.reshape(nb, D_GLA_V), ps,
                row(g_att_out[l]), row(g_gla_out[l]), w_o, tm=nb)
    ys = _ffn(xs, row(g_ffn2[l]), w2i, w2o, gf, tm=nb, tf=512, final_norm=True)
    nk_s = ps[:, COL_KA:COL_KA + D_ATT].reshape(1, nb, 1, H_ATT, HD_ATT)
    nv_s = ps[:, COL_VA:COL_VA + D_ATT].reshape(1, nb, 1, H_ATT, HD_ATT)

    return (yp.reshape(batch, seq, D_MODEL), ys.reshape(nb, 1, D_MODEL), nk_p, nv_p,
            s_fin[None], nk_s, nv_s, s_new[None])
```

```python
import functools

import numpy as np
import jax
import jax.numpy as jnp
from jax import lax
from jax.experimental import pallas as pl
from jax.experimental.pallas import tpu as pltpu

F32 = jnp.float32
BF16 = jnp.bfloat16

D_MODEL = 2048
D_FF = 5632
D_ATT = 1024
HD_ATT = 64
H_ATT = 16
H_GLA = 4
DK_GLA = 128
DV_GLA = 256
D_GLA_K = H_GLA * DK_GLA
D_GLA_V = H_GLA * DV_GLA
GATE_RANK = 16
GATE_TAU = 16.0
NORM_EPS = 1e-6
DIL_PATTERNS = ((128, 1), (512, 4), (2048, 16))
ATT_BLOCK = 128
ATT_GROUP = 8
GLA_CHUNK = 64
D_PROJ_MAIN = 3 * D_ATT + 2 * D_GLA_K + 2 * D_GLA_V

COL_QA, COL_KA, COL_VA = 0, D_ATT, 2 * D_ATT
COL_QG = 3 * D_ATT
COL_KG = COL_QG + D_GLA_K
COL_VG = COL_KG + D_GLA_K
COL_RG = COL_VG + D_GLA_V

VMEM_LIMIT_BYTES = 56 * 1024 * 1024


def _rms(x):
    return x * lax.rsqrt(jnp.mean(x * x, axis=-1, keepdims=True) + NORM_EPS)


def _dot(a, b):
    return jnp.dot(a, b, preferred_element_type=F32)


def _dot_nt(a, b):
    return lax.dot_general(a, b, (((1,), (1,)), ((), ())), preferred_element_type=F32)


def _dot_tn(a, b):
    return lax.dot_general(a, b, (((0,), (0,)), ((), ())), preferred_element_type=F32)


def _params(*sem):
    return pltpu.CompilerParams(dimension_semantics=sem, vmem_limit_bytes=VMEM_LIMIT_BYTES)


def _ffn_kernel(x_ref, g_ref, wa_ref, wb_ref, wo_ref, gf_ref, o_ref, h_ref, acc_ref, *, final_norm):
    j = pl.program_id(1)

    @pl.when(j == 0)
    def _():
        h_ref[...] = (_rms(x_ref[...]) * g_ref[...]).astype(BF16)
        acc_ref[...] = jnp.zeros_like(acc_ref)

    h = h_ref[...]
    a = _dot(h, wa_ref[...])
    b = _dot(h, wb_ref[...])
    act = (a * jax.nn.sigmoid(a) * b).astype(BF16)
    acc_ref[...] += _dot(act, wo_ref[...])

    @pl.when(j == pl.num_programs(1) - 1)
    def _():
        y = x_ref[...] + 0.5 * acc_ref[...]
        if final_norm:
            y = _rms(y) * gf_ref[...]
        o_ref[...] = y


def _ffn(x, g, w_in, w_out, g_final, *, tm, tf, final_norm):
    m = x.shape[0]
    nf = D_FF // tf
    return pl.pallas_call(
        functools.partial(_ffn_kernel, final_norm=final_norm),
        grid=(m // tm, nf),
        in_specs=[
            pl.BlockSpec((tm, D_MODEL), lambda i, j: (i, 0)),
            pl.BlockSpec((1, D_MODEL), lambda i, j: (0, 0)),
            pl.BlockSpec((D_MODEL, tf), lambda i, j: (0, j)),
            pl.BlockSpec((D_MODEL, tf), lambda i, j: (0, j + nf)),
            pl.BlockSpec((tf, D_MODEL), lambda i, j: (j, 0)),
            pl.BlockSpec((1, D_MODEL), lambda i, j: (0, 0)),
        ],
        out_specs=pl.BlockSpec((tm, D_MODEL), lambda i, j: (i, 0)),
        out_shape=jax.ShapeDtypeStruct((m, D_MODEL), F32),
        scratch_shapes=[pltpu.VMEM((tm, D_MODEL), BF16), pltpu.VMEM((tm, D_MODEL), F32)],
        compiler_params=_params("parallel", "arbitrary"),
        name="ffn",
    )(x, g, w_in, w_in, w_out, g_final)


def _proj_kernel(x_ref, g_ref, w_ref, wlr_ref, wg2_ref, bg2_ref, p_ref, la_ref, h_ref):
    j = pl.program_id(1)

    @pl.when(j == 0)
    def _():
        h = (_rms(x_ref[...]) * g_ref[...]).astype(BF16)
        h_ref[...] = h
        lr = _dot(h, wlr_ref[...])
        z = _dot(lr.astype(BF16), wg2_ref[...]) + bg2_ref[...]
        log_sig = jnp.minimum(z, 0.0) - jnp.log1p(jnp.exp(-jnp.abs(z)))
        la_ref[...] = log_sig * (1.0 / GATE_TAU)

    p_ref[...] = _dot(h_ref[...], w_ref[...])


def _proj(x, g, w_main, w_lr, w_g2, b_g2, *, tm, tn):
    m = x.shape[0]
    return pl.pallas_call(
        _proj_kernel,
        grid=(m // tm, D_PROJ_MAIN // tn),
        in_specs=[
            pl.BlockSpec((tm, D_MODEL), lambda i, j: (i, 0)),
            pl.BlockSpec((1, D_MODEL), lambda i, j: (0, 0)),
            pl.BlockSpec((D_MODEL, tn), lambda i, j: (0, j)),
            pl.BlockSpec((D_MODEL, GATE_RANK), lambda i, j: (0, 0)),
            pl.BlockSpec((GATE_RANK, D_GLA_K), lambda i, j: (0, 0)),
            pl.BlockSpec((1, D_GLA_K), lambda i, j: (0, 0)),
        ],
        out_specs=[
            pl.BlockSpec((tm, tn), lambda i, j: (i, j)),
            pl.BlockSpec((tm, D_GLA_K), lambda i, j: (i, 0)),
        ],
        out_shape=[
            jax.ShapeDtypeStruct((m, D_PROJ_MAIN), F32),
            jax.ShapeDtypeStruct((m, D_GLA_K), F32),
        ],
        scratch_shapes=[pltpu.VMEM((tm, D_MODEL), BF16)],
        compiler_params=_params("parallel", "arbitrary"),
        name="proj",
    )(x, g, w_main, w_lr, w_g2, b_g2)


def _attn_kernel(q_ref, k_ref, v_ref, o_ref, kt_ref, vt_ref,
                 qp_ref, kp_ref, vp_ref, m_ref, l_ref, acc_ref, s_ref, p_ref, ms_ref, ls_ref):
    blk = ATT_BLOCK
    for src_ref, dst_ref in ((k_ref, kt_ref), (v_ref, vt_ref)):
        t = src_ref[...].T
        dst_ref[0, 0] = t[:HD_ATT]
        dst_ref[0, 1] = t[HD_ATT:]

    scale = HD_ATT ** -0.5
    seq = q_ref.shape[0]
    ns = DIL_PATTERNS[-1][1]
    lane_lo = lax.broadcasted_iota(jnp.int32, (1, 2 * HD_ATT), 1) < HD_ATT
    neg = jnp.float32(-jnp.inf)

    assert ns == 16
    for src_ref, tmp_ref, dst_ref in ((q_ref, m_ref, qp_ref), (k_ref, l_ref, kp_ref), (v_ref, acc_ref, vp_ref)):
        for r4 in range(4):
            x = src_ref[pl.ds(r4, seq // 4, stride=4), :]
            tmp_ref[pl.ds(r4 * (seq // 4), seq // 4), :] = x * scale if src_ref is q_ref else x
        for r4 in range(4):
            for a in range(4):
                dst_ref[pl.ds((4 * a + r4) * blk, blk), :] = tmp_ref[pl.ds(r4 * (seq // 4) + a, blk, stride=4), :]

    def run_pattern(d, first_pattern):
        na = ns // d
        plen = blk // na
        per_stream = seq // (d * blk)

        def offset(idx):
            return na * (idx & (plen - 1)) + idx // plen

        qpos = offset(lax.broadcasted_iota(jnp.int32, (blk, blk), 0))
        kpos = offset(lax.broadcasted_iota(jnp.int32, (blk, blk), 1))
        bias_cur = jnp.where(kpos <= qpos, 0.0, neg)
        if per_stream > 1:
            bias_prev = jnp.where(kpos >= qpos, 0.0, neg)
            bias_band = jnp.concatenate([bias_prev, bias_cur], axis=1)
            bias_first = jnp.concatenate([jnp.full((blk, blk), neg, F32), bias_cur], axis=1)

        def pieces(rd, n):
            return [pl.ds(pl.multiple_of((a * d + rd) * blk + plen * n, 8), plen) for a in range(na)]

        def gather(ref, ps):
            return jnp.concatenate([ref[p, :] for p in ps], axis=0)

        nk = 2 * blk if per_stream > 1 else blk
        heads = (lane_lo, jnp.logical_not(lane_lo))

        def keys(ref, rd, n):
            cur = gather(ref, pieces(rd, n))
            if per_stream == 1:
                return cur.astype(BF16)
            prev = gather(ref, pieces(rd, jnp.maximum(n - 1, 0)))
            return jnp.concatenate([prev, cur], axis=0).astype(BF16)

        def group(g, c):
            blocks = [((g * ATT_GROUP + b) % d, (g * ATT_GROUP + b) // d) for b in range(ATT_GROUP)]
            for b, (rd, n) in enumerate(blocks):
                q = gather(qp_ref, pieces(rd, n))
                kk = keys(kp_ref, rd, n)
                bias = jnp.where(n > 0, bias_band, bias_first) if per_stream > 1 else bias_cur
                for h, sel in enumerate(heads):
                    qh = jnp.where(sel, q, 0.0).astype(BF16)
                    s_ref[b, h, :, :nk] = _dot_nt(qh, kk) + bias
            for b in range(ATT_GROUP):
                for h in range(2):
                    m = jnp.max(s_ref[b, h, :, :nk], axis=-1, keepdims=True)
                    ms_ref[b, h] = jnp.broadcast_to(m, (blk, blk))
            for b in range(ATT_GROUP):
                for h in range(2):
                    m = ms_ref[b, h]
                    m = jnp.concatenate([m, m], axis=1) if nk == 2 * blk else m
                    e = jnp.exp(s_ref[b, h, :, :nk] - m)
                    p_ref[b, h, :, :nk] = e.astype(BF16)
                    ls_ref[b, h] = jnp.broadcast_to(jnp.sum(e, axis=-1, keepdims=True), (blk, blk))
            for b, (rd, n) in enumerate(blocks):
                vv = keys(vp_ref, rd, n)
                u = jnp.where(lane_lo, _dot(p_ref[b, 0, :, :nk], vv), _dot(p_ref[b, 1, :, :nk], vv))
                m = jnp.where(lane_lo, ms_ref[b, 0], ms_ref[b, 1])
                l = jnp.where(lane_lo, ls_ref[b, 0], ls_ref[b, 1])
                for a, p in enumerate(pieces(rd, n)):
                    sl = slice(a * plen, (a + 1) * plen)
                    if first_pattern:
                        m_ref[p, :] = m[sl]
                        l_ref[p, :] = l[sl]
                        acc_ref[p, :] = u[sl]
                    else:
                        m_old = m_ref[p, :]
                        m_new = jnp.maximum(m_old, m[sl])
                        a_old = jnp.exp(m_old - m_new)
                        a_blk = jnp.exp(m[sl] - m_new)
                        m_ref[p, :] = m_new
                        l_ref[p, :] = a_old * l_ref[p, :] + a_blk * l[sl]
                        acc_ref[p, :] = a_old * acc_ref[p, :] + a_blk * u[sl]
            return c

        lax.fori_loop(0, d * per_stream // ATT_GROUP, group, 0)

    for idx, (_, d) in enumerate(DIL_PATTERNS):
        run_pattern(d, idx == 0)

    for r4 in range(4):
        for a in range(4):
            rows = pl.ds((4 * a + r4) * blk, blk)
            qp_ref[pl.ds(r4 * (seq // 4) + a, blk, stride=4), :] = acc_ref[rows, :] / l_ref[rows, :]
    for r4 in range(4):
        o_ref[pl.ds(r4, seq // 4, stride=4), :] = qp_ref[pl.ds(r4 * (seq // 4), seq // 4), :]


def _attn_prompt(p, batch, seq):
    ns = DIL_PATTERNS[-1][1]
    assert seq == ns * ATT_BLOCK
    assert all(w // d == ATT_BLOCK and ns % d == 0 and ATT_BLOCK * d // ns >= 8 for w, d in DIL_PATTERNS)
    lanes = 2 * HD_ATT
    n_pairs = H_ATT // 2
    return pl.pallas_call(
        _attn_kernel,
        grid=(batch, n_pairs),
        in_specs=[
            pl.BlockSpec((seq, lanes), lambda b, h: (b, COL_QA // lanes + h)),
            pl.BlockSpec((seq, lanes), lambda b, h: (b, COL_KA // lanes + h)),
            pl.BlockSpec((seq, lanes), lambda b, h: (b, COL_VA // lanes + h)),
        ],
        out_specs=[
            pl.BlockSpec((seq, lanes), lambda b, h: (b, h)),
            pl.BlockSpec((1, 2, HD_ATT, seq), lambda b, h: (b, h, 0, 0)),
            pl.BlockSpec((1, 2, HD_ATT, seq), lambda b, h: (b, h, 0, 0)),
        ],
        out_shape=[
            jax.ShapeDtypeStruct((batch * seq, D_ATT), F32),
            jax.ShapeDtypeStruct((batch, H_ATT, HD_ATT, seq), F32),
            jax.ShapeDtypeStruct((batch, H_ATT, HD_ATT, seq), F32),
        ],
        scratch_shapes=[pltpu.VMEM((seq, lanes), F32)] * 6 + [
            pltpu.VMEM((ATT_GROUP, 2, ATT_BLOCK, 2 * ATT_BLOCK), F32),
            pltpu.VMEM((ATT_GROUP, 2, ATT_BLOCK, 2 * ATT_BLOCK), BF16),
            pltpu.VMEM((ATT_GROUP, 2, ATT_BLOCK, lanes), F32),
            pltpu.VMEM((ATT_GROUP, 2, ATT_BLOCK, lanes), F32),
        ],
        compiler_params=_params("parallel", "parallel"),
        name="attn_prompt",
    )(p, p, p)


_GLA_LEVELS = (32, 16, 8, 4, 2, 1)


def _gla_exponent_matrix():
    c = GLA_CHUNK
    t = np.arange(c)[:, None]
    u = np.arange(c)[None, :]
    mats = [(u <= t), (u > t)]
    for h in _GLA_LEVELS:
        mid = (t // (2 * h)) * (2 * h) + h - 1
        upper = (t % (2 * h)) >= h
        mats.append(np.where(upper, (u > mid) & (u <= t), (u > t) & (u <= mid)))
    tmat = np.concatenate(mats, axis=0).astype(np.float32)
    return np.concatenate([tmat, tmat], axis=1)


def _gla_kernel(q_ref, k_ref, v_ref, g_ref, t_ref, o_ref, s_ref, st_ref):
    c = GLA_CHUNK
    scale = DK_GLA ** -0.5
    ti = lax.broadcasted_iota(jnp.int32, (c, c), 0)
    si = lax.broadcasted_iota(jnp.int32, (c, c), 1)
    txs = ti ^ si
    below = ti > si
    level_masks = [below & (txs >= h) & (txs < 2 * h) for h in _GLA_LEVELS]
    diag = ti == si

    st_ref[...] = jnp.zeros_like(st_ref)

    def chunk(ci, carry):
        r = pl.ds(pl.multiple_of(ci * c, c), c)
        g = g_ref[r, :]
        g_hi = g.astype(BF16)
        g_lo = (g - g_hi.astype(F32)).astype(BF16)
        f = jnp.exp(_dot(t_ref[...], jnp.concatenate([g_hi, g_lo], axis=0)))
        q = q_ref[r, :] * scale
        k = k_ref[r, :]
        v = v_ref[r, :].astype(BF16)
        a = jnp.where(diag, _dot_nt(q.astype(BF16), k.astype(BF16)), 0.0)
        for lvl in range(len(_GLA_LEVELS)):
            fl = f[2 * c + lvl * c: 3 * c + lvl * c]
            a = a + jnp.where(level_masks[lvl],
                              _dot_nt((q * fl).astype(BF16), (k * fl).astype(BF16)), 0.0)
        st = st_ref[...]
        qe = (q * f[0:c]).astype(BF16)
        o_ref[r, :] = _dot_nt(qe, st.astype(BF16)) + _dot(a.astype(BF16), v)
        kd = (k * f[c:2 * c]).astype(BF16)
        st_ref[...] = st * f[c - 1:c] + _dot_tn(v, kd)
        return carry

    lax.fori_loop(0, q_ref.shape[0] // c, chunk, 0)
    s_ref[0, 0] = st_ref[...].T


def _gla_prompt(p, la, tmat, batch, seq):
    return pl.pallas_call(
        _gla_kernel,
        grid=(batch, H_GLA),
        in_specs=[
            pl.BlockSpec((seq, DK_GLA), lambda b, h: (b, COL_QG // DK_GLA + h)),
            pl.BlockSpec((seq, DK_GLA), lambda b, h: (b, COL_KG // DK_GLA + h)),
            pl.BlockSpec((seq, DV_GLA), lambda b, h: (b, COL_VG // DV_GLA + h)),
            pl.BlockSpec((seq, DK_GLA), lambda b, h: (b, h)),
            pl.BlockSpec(tmat.shape, lambda b, h: (0, 0)),
        ],
        out_specs=[
            pl.BlockSpec((seq, DV_GLA), lambda b, h: (b, h)),
            pl.BlockSpec((1, 1, DK_GLA, DV_GLA), lambda b, h: (b, h, 0, 0)),
        ],
        out_shape=[
            jax.ShapeDtypeStruct((batch * seq, D_GLA_V), F32),
            jax.ShapeDtypeStruct((batch, H_GLA, DK_GLA, DV_GLA), F32),
        ],
        scratch_shapes=[pltpu.VMEM((DV_GLA, DK_GLA), F32)],
        compiler_params=_params("parallel", "parallel"),
        name="gla_prompt",
    )(p, p, p, la, tmat)


def _merge_kernel(x_ref, oa_ref, og_ref, rg_ref, ga_ref, gg_ref, w_ref, o_ref):
    a = (_rms(oa_ref[...]) * ga_ref[...]).astype(BF16)
    parts = []
    for h in range(H_GLA):
        cols = slice(h * DV_GLA, (h + 1) * DV_GLA)
        r = rg_ref[:, cols]
        parts.append((_rms(og_ref[:, cols]) * gg_ref[...] * (r * jax.nn.sigmoid(r))).astype(BF16))
    gg = jnp.concatenate(parts, axis=-1)
    o_ref[...] = x_ref[...] + _dot(a, w_ref[0:D_ATT, :]) + _dot(gg, w_ref[D_ATT:, :])


def _merge(x, o_att, o_gla, p, g_att, g_gla, w_out, *, tm):
    m = x.shape[0]
    return pl.pallas_call(
        _merge_kernel,
        grid=(m // tm,),
        in_specs=[
            pl.BlockSpec((tm, D_MODEL), lambda i: (i, 0)),
            pl.BlockSpec((tm, D_ATT), lambda i: (i, 0)),
            pl.BlockSpec((tm, D_GLA_V), lambda i: (i, 0)),
            pl.BlockSpec((tm, D_GLA_V), lambda i: (i, COL_RG // D_GLA_V)),
            pl.BlockSpec((1, D_ATT), lambda i: (0, 0)),
            pl.BlockSpec((1, DV_GLA), lambda i: (0, 0)),
            pl.BlockSpec((D_MODEL, D_MODEL), lambda i: (0, 0)),
        ],
        out_specs=pl.BlockSpec((tm, D_MODEL), lambda i: (i, 0)),
        out_shape=jax.ShapeDtypeStruct((m, D_MODEL), F32),
        compiler_params=_params("parallel"),
        name="merge",
    )(x, o_att, o_gla, p, g_att, g_gla, w_out)


SATT_HEADS_PER_STEP = 4


def _sattn_kernel(q_ref, kn_ref, vn_ref, k_ref, v_ref, o_ref):
    scale = HD_ATT ** -0.5
    wb = k_ref.shape[-1]
    t = lax.broadcasted_iota(jnp.int32, (1, wb), 1)
    cnt = jnp.zeros((1, wb), F32)
    for w, d in DIL_PATTERNS:
        cnt = cnt + jnp.where((t >= wb - w) & ((t & (d - 1)) == 0), 1.0, 0.0)
    bias = jnp.where(cnt > 0.0, 0.0, jnp.float32(-jnp.inf))
    n_pat = float(len(DIL_PATTERNS))
    for h in range(k_ref.shape[1]):
        qc = q_ref[0, h] * scale
        s = jnp.sum(k_ref[0, h] * qc, axis=0, keepdims=True) + bias
        s_new = jnp.sum(kn_ref[0, h] * qc, axis=0, keepdims=True)
        m = jnp.maximum(jnp.max(s, axis=-1, keepdims=True), s_new)
        e = cnt * jnp.exp(s - m)
        e_new = n_pat * jnp.exp(s_new - m)
        den = jnp.sum(e, axis=-1, keepdims=True) + e_new
        num = jnp.sum(v_ref[0, h] * e, axis=-1, keepdims=True) + e_new * vn_ref[0, h]
        o_ref[0, h] = num / den


def _attn_sample(q, kn, vn, cache_kt, cache_vt):
    nb, _, _, wb = cache_kt.shape
    assert all(wb % d == 0 and w <= wb for w, d in DIL_PATTERNS)
    hs = SATT_HEADS_PER_STEP
    col = pl.BlockSpec((1, hs, HD_ATT, 1), lambda b, h: (b, h, 0, 0))
    cache = pl.BlockSpec((1, hs, HD_ATT, wb), lambda b, h: (b, h, 0, 0))
    return pl.pallas_call(
        _sattn_kernel,
        grid=(nb, H_ATT // hs),
        in_specs=[col, col, col, cache, cache],
        out_specs=col,
        out_shape=jax.ShapeDtypeStruct((nb, H_ATT, HD_ATT, 1), F32),
        compiler_params=_params("parallel", "parallel"),
        name="attn_sample",
    )(q, kn, vn, cache_kt, cache_vt)


def _sgla_kernel(q_ref, k_ref, g_ref, v_ref, s_ref, so_ref, o_ref):
    scale = DK_GLA ** -0.5
    for h in range(H_GLA):
        s_new = jnp.exp(g_ref[0, h]) * s_ref[0, h] + k_ref[0, h] * v_ref[0, h]
        so_ref[0, h] = s_new
        o_ref[0, h] = jnp.sum((q_ref[0, h] * scale) * s_new, axis=0, keepdims=True)


def _gla_sample(q, k, g, v, state):
    nb = state.shape[0]
    col = pl.BlockSpec((1, H_GLA, DK_GLA, 1), lambda b: (b, 0, 0, 0))
    return pl.pallas_call(
        _sgla_kernel,
        grid=(nb,),
        in_specs=[col, col, col,
                  pl.BlockSpec((1, H_GLA, 1, DV_GLA), lambda b: (b, 0, 0, 0)),
                  pl.BlockSpec((1, H_GLA, DK_GLA, DV_GLA), lambda b: (b, 0, 0, 0))],
        out_specs=[pl.BlockSpec((1, H_GLA, DK_GLA, DV_GLA), lambda b: (b, 0, 0, 0)),
                   pl.BlockSpec((1, H_GLA, 1, DV_GLA), lambda b: (b, 0, 0, 0))],
        out_shape=[jax.ShapeDtypeStruct((nb, H_GLA, DK_GLA, DV_GLA), F32),
                   jax.ShapeDtypeStruct((nb, H_GLA, 1, DV_GLA), F32)],
        compiler_params=_params("parallel"),
        name="gla_sample",
    )(q, k, g, v, state)


def kernel(x_prompt, x_sample, cache_att_k, cache_att_v, state_gla, g_ffn1, w_ffn1_in, w_ffn1_out, g_mix, w_in, w_gate2, b_gate2, g_att_out, g_gla_out, w_out, g_ffn2, w_ffn2_in, w_ffn2_out, g_final):
    depth = w_in.shape[0]
    assert depth == 1
    batch, seq, _ = x_prompt.shape
    nb, dec_seq, _ = x_sample.shape
    assert dec_seq == 1
    xp = x_prompt.reshape(batch * seq, D_MODEL)
    xs = x_sample.reshape(nb, D_MODEL)
    row = lambda a: a.reshape(1, -1)
    l = 0

    w1i, w1o = w_ffn1_in[l].astype(BF16), w_ffn1_out[l].astype(BF16)
    w2i, w2o = w_ffn2_in[l].astype(BF16), w_ffn2_out[l].astype(BF16)
    w_main = w_in[l, :, :D_PROJ_MAIN].astype(BF16)
    w_lr = w_in[l, :, D_PROJ_MAIN:].astype(BF16)
    w_g2 = w_gate2[l].astype(BF16)
    b_g2 = row(b_gate2[l])
    w_o = w_out[l].astype(BF16)
    gf = row(g_final)
    tmat = jnp.asarray(_gla_exponent_matrix(), dtype=BF16)

    xp = _ffn(xp, row(g_ffn1[l]), w1i, w1o, gf, tm=512, tf=512, final_norm=False)
    pp, lap = _proj(xp, row(g_mix[l]), w_main, w_lr, w_g2, b_g2, tm=1024, tn=1024)
    o_att, kt_p, vt_p = _attn_prompt(pp, batch, seq)
    o_gla, s_fin = _gla_prompt(pp, lap, tmat, batch, seq)
    xp = _merge(xp, o_att, o_gla, pp, row(g_att_out[l]), row(g_gla_out[l]), w_o, tm=256)
    yp = _ffn(xp, row(g_ffn2[l]), w2i, w2o, gf, tm=512, tf=512, final_norm=True)
    nk_p = jnp.transpose(kt_p, (0, 3, 1, 2))[None]
    nv_p = jnp.transpose(vt_p, (0, 3, 1, 2))[None]

    xs = _ffn(xs, row(g_ffn1[l]), w1i, w1o, gf, tm=nb, tf=512, final_norm=False)
    ps, las = _proj(xs, row(g_mix[l]), w_main, w_lr, w_g2, b_g2, tm=nb, tn=512)
    hcol = lambda c0: ps[:, c0:c0 + D_ATT].reshape(nb, H_ATT, HD_ATT, 1)
    o_att_s = _attn_sample(hcol(COL_QA), hcol(COL_KA), hcol(COL_VA),
                           jnp.transpose(cache_att_k[l], (0, 2, 3, 1)),
                           jnp.transpose(cache_att_v[l], (0, 2, 3, 1)))
    col = lambda a: a.reshape(nb, H_GLA, DK_GLA, 1)
    s_new, o_gla_s = _gla_sample(
        col(ps[:, COL_QG:COL_QG + D_GLA_K]), col(ps[:, COL_KG:COL_KG + D_GLA_K]), col(las),
        ps[:, COL_VG:COL_VG + D_GLA_V].reshape(nb, H_GLA, 1, DV_GLA), state_gla[l])
    xs = _merge(xs, o_att_s.reshape(nb, D_ATT), o_gla_s.reshape(nb, D_GLA_V), ps,
                row(g_att_out[l]), row(g_gla_out[l]), w_o, tm=nb)
    ys = _ffn(xs, row(g_ffn2[l]), w2i, w2o, gf, tm=nb, tf=512, final_norm=True)
    nk_s = ps[:, COL_KA:COL_KA + D_ATT].reshape(1, nb, 1, H_ATT, HD_ATT)
    nv_s = ps[:, COL_VA:COL_VA + D_ATT].reshape(1, nb, 1, H_ATT, HD_ATT)

    return (yp.reshape(batch, seq, D_MODEL), ys.reshape(nb, 1, D_MODEL), nk_p, nv_p,
            s_fin[None], nk_s, nv_s, s_new[None])
```

```python
import functools

import numpy as np
import jax
import jax.numpy as jnp
from jax import lax
from jax.experimental import pallas as pl
from jax.experimental.pallas import tpu as pltpu

F32 = jnp.float32
BF16 = jnp.bfloat16

D_MODEL = 2048
D_FF = 5632
D_ATT = 1024
HD_ATT = 64
H_ATT = 16
H_GLA = 4
DK_GLA = 128
DV_GLA = 256
D_GLA_K = H_GLA * DK_GLA
D_GLA_V = H_GLA * DV_GLA
GATE_RANK = 16
GATE_TAU = 16.0
NORM_EPS = 1e-6
DIL_PATTERNS = ((128, 1), (512, 4), (2048, 16))
ATT_BLOCK = 128
ATT_GROUP = 8
GLA_CHUNK = 64
D_PROJ_MAIN = 3 * D_ATT + 2 * D_GLA_K + 2 * D_GLA_V

COL_QA, COL_KA, COL_VA = 0, D_ATT, 2 * D_ATT
COL_QG = 3 * D_ATT
COL_KG = COL_QG + D_GLA_K
COL_VG = COL_KG + D_GLA_K
COL_RG = COL_VG + D_GLA_V

VMEM_LIMIT_BYTES = 56 * 1024 * 1024


def _rms(x):
    return x * lax.rsqrt(jnp.mean(x * x, axis=-1, keepdims=True) + NORM_EPS)


def _dot(a, b):
    return jnp.dot(a, b, preferred_element_type=F32)


def _dot_nt(a, b):
    return lax.dot_general(a, b, (((1,), (1,)), ((), ())), preferred_element_type=F32)


def _dot_tn(a, b):
    return lax.dot_general(a, b, (((0,), (0,)), ((), ())), preferred_element_type=F32)


def _params(*sem):
    return pltpu.CompilerParams(dimension_semantics=sem, vmem_limit_bytes=VMEM_LIMIT_BYTES)


def _ffn_kernel(x_ref, g_ref, wa_ref, wb_ref, wo_ref, gf_ref, o_ref, *rest, final_norm, emit_bf16):
    h_ref = rest[-1]
    j = pl.program_id(1)

    @pl.when(j == 0)
    def _():
        h_ref[...] = (_rms(x_ref[...]) * g_ref[...]).astype(BF16)
        o_ref[...] = jnp.zeros_like(o_ref)

    wa, wb, wo = wa_ref[...], wb_ref[...], wo_ref[...]
    if emit_bf16:
        wa, wb, wo = wa.astype(BF16), wb.astype(BF16), wo.astype(BF16)
        for dst_ref, w in zip(rest[:3], (wa, wb, wo)):
            dst_ref[...] = w
    h = h_ref[...]
    a = _dot(h, wa)
    b = _dot(h, wb)
    act = (a * jax.nn.sigmoid(a) * b).astype(BF16)
    o_ref[...] += _dot(act, wo)

    @pl.when(j == pl.num_programs(1) - 1)
    def _():
        y = x_ref[...] + 0.5 * o_ref[...]
        if final_norm:
            y = _rms(y) * gf_ref[...]
        o_ref[...] = y


def _ffn(x, g, wa, wb, wo, g_final, *, tm, tf, final_norm, emit_bf16=False):
    m = x.shape[0]
    nf = D_FF // tf
    b_off = nf if wb.shape[1] == 2 * D_FF else 0
    out_specs = [pl.BlockSpec((tm, D_MODEL), lambda i, j: (i, 0))]
    out_shape = [jax.ShapeDtypeStruct((m, D_MODEL), F32)]
    if emit_bf16:
        assert m == tm
        out_specs += [pl.BlockSpec((D_MODEL, tf), lambda i, j: (0, j)),
                      pl.BlockSpec((D_MODEL, tf), lambda i, j: (0, j)),
                      pl.BlockSpec((tf, D_MODEL), lambda i, j: (j, 0))]
        out_shape += [jax.ShapeDtypeStruct((D_MODEL, D_FF), BF16),
                      jax.ShapeDtypeStruct((D_MODEL, D_FF), BF16),
                      jax.ShapeDtypeStruct((D_FF, D_MODEL), BF16)]
    outs = pl.pallas_call(
        functools.partial(_ffn_kernel, final_norm=final_norm, emit_bf16=emit_bf16),
        grid=(m // tm, nf),
        in_specs=[
            pl.BlockSpec((tm, D_MODEL), lambda i, j: (i, 0)),
            pl.BlockSpec((1, D_MODEL), lambda i, j: (0, 0)),
            pl.BlockSpec((D_MODEL, tf), lambda i, j: (0, j)),
            pl.BlockSpec((D_MODEL, tf), lambda i, j: (0, j + b_off)),
            pl.BlockSpec((tf, D_MODEL), lambda i, j: (j, 0)),
            pl.BlockSpec((1, D_MODEL), lambda i, j: (0, 0)),
        ],
        out_specs=out_specs,
        out_shape=out_shape,
        scratch_shapes=[pltpu.VMEM((tm, D_MODEL), BF16)],
        compiler_params=_params("parallel", "arbitrary"),
        name="ffn",
    )(x, g, wa, wb, wo, g_final)
    return outs if emit_bf16 else outs[0]


def _proj_kernel(x_ref, g_ref, w_ref, wlr_ref, wg2_ref, bg2_ref, p_ref, la_ref, *rest, emit_bf16):
    h_ref = rest[-1]
    j = pl.program_id(1)

    @pl.when(j == 0)
    def _():
        h = (_rms(x_ref[...]) * g_ref[...]).astype(BF16)
        h_ref[...] = h
        lr = _dot_nt(h, wlr_ref[...])
        z = _dot(lr.astype(BF16), wg2_ref[...]) + bg2_ref[...]
        log_sig = jnp.minimum(z, 0.0) - jnp.log1p(jnp.exp(-jnp.abs(z)))
        la_ref[...] = log_sig * (1.0 / GATE_TAU)

    w = w_ref[...]
    if emit_bf16:
        w = w.astype(BF16)
        rest[0][...] = w
    p_ref[...] = _dot_nt(h_ref[...], w)


def _proj(x, g, w_t, w_lr_t, w_g2, b_g2, *, tm, tn, emit_bf16=False):
    m = x.shape[0]
    out_specs = [
        pl.BlockSpec((tm, tn), lambda i, j: (i, j)),
        pl.BlockSpec((tm, D_GLA_K), lambda i, j: (i, 0)),
    ]
    out_shape = [
        jax.ShapeDtypeStruct((m, D_PROJ_MAIN), F32),
        jax.ShapeDtypeStruct((m, D_GLA_K), F32),
    ]
    if emit_bf16:
        assert m == tm
        out_specs.append(pl.BlockSpec((tn, D_MODEL), lambda i, j: (j, 0)))
        out_shape.append(jax.ShapeDtypeStruct((D_PROJ_MAIN, D_MODEL), BF16))
    return pl.pallas_call(
        functools.partial(_proj_kernel, emit_bf16=emit_bf16),
        grid=(m // tm, D_PROJ_MAIN // tn),
        in_specs=[
            pl.BlockSpec((tm, D_MODEL), lambda i, j: (i, 0)),
            pl.BlockSpec((1, D_MODEL), lambda i, j: (0, 0)),
            pl.BlockSpec((tn, D_MODEL), lambda i, j: (j, 0)),
            pl.BlockSpec((GATE_RANK, D_MODEL), lambda i, j: (0, 0)),
            pl.BlockSpec((GATE_RANK, D_GLA_K), lambda i, j: (0, 0)),
            pl.BlockSpec((1, D_GLA_K), lambda i, j: (0, 0)),
        ],
        out_specs=out_specs,
        out_shape=out_shape,
        scratch_shapes=[pltpu.VMEM((tm, D_MODEL), BF16)],
        compiler_params=_params("parallel", "arbitrary"),
        name="proj",
    )(x, g, w_t, w_lr_t, w_g2, b_g2)


def _attn_kernel(q_ref, k_ref, v_ref, o_ref, kt_ref, vt_ref,
                 qp_ref, kp_ref, vp_ref, m_ref, l_ref, acc_ref, s_ref, p_ref, ms_ref, ls_ref):
    blk = ATT_BLOCK
    for src_ref, dst_ref in ((k_ref, kt_ref), (v_ref, vt_ref)):
        t = src_ref[...].T
        dst_ref[0, 0] = t[:HD_ATT]
        dst_ref[0, 1] = t[HD_ATT:]

    scale = HD_ATT ** -0.5
    seq = q_ref.shape[0]
    ns = DIL_PATTERNS[-1][1]
    lane_lo = lax.broadcasted_iota(jnp.int32, (1, 2 * HD_ATT), 1) < HD_ATT
    neg = jnp.float32(-jnp.inf)

    assert ns == 16
    for src_ref, tmp_ref, dst_ref in ((q_ref, m_ref, qp_ref), (k_ref, l_ref, kp_ref), (v_ref, acc_ref, vp_ref)):
        for r4 in range(4):
            x = src_ref[pl.ds(r4, seq // 4, stride=4), :]
            tmp_ref[pl.ds(r4 * (seq // 4), seq // 4), :] = x * scale if src_ref is q_ref else x
        for r4 in range(4):
            for a in range(4):
                dst_ref[pl.ds((4 * a + r4) * blk, blk), :] = tmp_ref[pl.ds(r4 * (seq // 4) + a, blk, stride=4), :]

    def run_pattern(d, first_pattern):
        na = ns // d
        plen = blk // na
        per_stream = seq // (d * blk)

        def offset(idx):
            return na * (idx & (plen - 1)) + idx // plen

        qpos = offset(lax.broadcasted_iota(jnp.int32, (blk, blk), 0))
        kpos = offset(lax.broadcasted_iota(jnp.int32, (blk, blk), 1))
        bias_cur = jnp.where(kpos <= qpos, 0.0, neg)
        if per_stream > 1:
            bias_prev = jnp.where(kpos >= qpos, 0.0, neg)
            bias_band = jnp.concatenate([bias_prev, bias_cur], axis=1)
            bias_first = jnp.concatenate([jnp.full((blk, blk), neg, F32), bias_cur], axis=1)

        def pieces(rd, n):
            return [pl.ds(pl.multiple_of((a * d + rd) * blk + plen * n, 8), plen) for a in range(na)]

        def gather(ref, ps):
            return jnp.concatenate([ref[p, :] for p in ps], axis=0)

        nk = 2 * blk if per_stream > 1 else blk
        heads = (lane_lo, jnp.logical_not(lane_lo))

        def keys(ref, rd, n):
            cur = gather(ref, pieces(rd, n))
            if per_stream == 1:
                return cur.astype(BF16)
            prev = gather(ref, pieces(rd, jnp.maximum(n - 1, 0)))
            return jnp.concatenate([prev, cur], axis=0).astype(BF16)

        def group(g, c):
            blocks = [((g * ATT_GROUP + b) % d, (g * ATT_GROUP + b) // d) for b in range(ATT_GROUP)]
            for b, (rd, n) in enumerate(blocks):
                q = gather(qp_ref, pieces(rd, n))
                kk = keys(kp_ref, rd, n)
                bias = jnp.where(n > 0, bias_band, bias_first) if per_stream > 1 else bias_cur
                for h, sel in enumerate(heads):
                    qh = jnp.where(sel, q, 0.0).astype(BF16)
                    s_ref[b, h, :, :nk] = _dot_nt(qh, kk) + bias
            for b in range(ATT_GROUP):
                for h in range(2):
                    m = jnp.max(s_ref[b, h, :, :nk], axis=-1, keepdims=True)
                    ms_ref[b, h] = jnp.broadcast_to(m, (blk, blk))
            for b in range(ATT_GROUP):
                for h in range(2):
                    m = ms_ref[b, h]
                    m = jnp.concatenate([m, m], axis=1) if nk == 2 * blk else m
                    e = jnp.exp(s_ref[b, h, :, :nk] - m)
                    p_ref[b, h, :, :nk] = e.astype(BF16)
                    ls_ref[b, h] = jnp.broadcast_to(jnp.sum(e, axis=-1, keepdims=True), (blk, blk))
            for b, (rd, n) in enumerate(blocks):
                vv = keys(vp_ref, rd, n)
                u = jnp.where(lane_lo, _dot(p_ref[b, 0, :, :nk], vv), _dot(p_ref[b, 1, :, :nk], vv))
                m = jnp.where(lane_lo, ms_ref[b, 0], ms_ref[b, 1])
                l = jnp.where(lane_lo, ls_ref[b, 0], ls_ref[b, 1])
                for a, p in enumerate(pieces(rd, n)):
                    sl = slice(a * plen, (a + 1) * plen)
                    if first_pattern:
                        m_ref[p, :] = m[sl]
                        l_ref[p, :] = l[sl]
                        acc_ref[p, :] = u[sl]
                    else:
                        m_old = m_ref[p, :]
                        m_new = jnp.maximum(m_old, m[sl])
                        a_old = jnp.exp(m_old - m_new)
                        a_blk = jnp.exp(m[sl] - m_new)
                        m_ref[p, :] = m_new
                        l_ref[p, :] = a_old * l_ref[p, :] + a_blk * l[sl]
                        acc_ref[p, :] = a_old * acc_ref[p, :] + a_blk * u[sl]
            return c

        lax.fori_loop(0, d * per_stream // ATT_GROUP, group, 0)

    for idx, (_, d) in enumerate(DIL_PATTERNS):
        run_pattern(d, idx == 0)

    for r4 in range(4):
        for a in range(4):
            rows = pl.ds((4 * a + r4) * blk, blk)
            qp_ref[pl.ds(r4 * (seq // 4) + a, blk, stride=4), :] = acc_ref[rows, :] / l_ref[rows, :]
    for r4 in range(4):
        o_ref[pl.ds(r4, seq // 4, stride=4), :] = qp_ref[pl.ds(r4 * (seq // 4), seq // 4), :]


def _attn_prompt(p, batch, seq):
    ns = DIL_PATTERNS[-1][1]
    assert seq == ns * ATT_BLOCK
    assert all(w // d == ATT_BLOCK and ns % d == 0 and ATT_BLOCK * d // ns >= 8 for w, d in DIL_PATTERNS)
    lanes = 2 * HD_ATT
    n_pairs = H_ATT // 2
    return pl.pallas_call(
        _attn_kernel,
        grid=(batch, n_pairs),
        in_specs=[
            pl.BlockSpec((seq, lanes), lambda b, h: (b, COL_QA // lanes + h)),
            pl.BlockSpec((seq, lanes), lambda b, h: (b, COL_KA // lanes + h)),
            pl.BlockSpec((seq, lanes), lambda b, h: (b, COL_VA // lanes + h)),
        ],
        out_specs=[
            pl.BlockSpec((seq, lanes), lambda b, h: (b, h)),
            pl.BlockSpec((1, 2, HD_ATT, seq), lambda b, h: (b, h, 0, 0)),
            pl.BlockSpec((1, 2, HD_ATT, seq), lambda b, h: (b, h, 0, 0)),
        ],
        out_shape=[
            jax.ShapeDtypeStruct((batch * seq, D_ATT), F32),
            jax.ShapeDtypeStruct((batch, H_ATT, HD_ATT, seq), F32),
            jax.ShapeDtypeStruct((batch, H_ATT, HD_ATT, seq), F32),
        ],
        scratch_shapes=[pltpu.VMEM((seq, lanes), F32)] * 6 + [
            pltpu.VMEM((ATT_GROUP, 2, ATT_BLOCK, 2 * ATT_BLOCK), F32),
            pltpu.VMEM((ATT_GROUP, 2, ATT_BLOCK, 2 * ATT_BLOCK), BF16),
            pltpu.VMEM((ATT_GROUP, 2, ATT_BLOCK, lanes), F32),
            pltpu.VMEM((ATT_GROUP, 2, ATT_BLOCK, lanes), F32),
        ],
        compiler_params=_params("parallel", "parallel"),
        name="attn_prompt",
    )(p, p, p)


_GLA_LEVELS = (32, 16, 8, 4, 2, 1)


def _gla_exponent_matrix():
    c = GLA_CHUNK
    t = np.arange(c)[:, None]
    u = np.arange(c)[None, :]
    mats = [(u <= t), (u > t)]
    for h in _GLA_LEVELS:
        mid = (t // (2 * h)) * (2 * h) + h - 1
        upper = (t % (2 * h)) >= h
        mats.append(np.where(upper, (u > mid) & (u <= t), (u > t) & (u <= mid)))
    tmat = np.concatenate(mats, axis=0).astype(np.float32)
    return np.concatenate([tmat, tmat], axis=1)


def _gla_kernel(q_ref, k_ref, v_ref, g_ref, t_ref, o_ref, s_ref, st_ref):
    c = GLA_CHUNK
    scale = DK_GLA ** -0.5
    ti = lax.broadcasted_iota(jnp.int32, (c, c), 0)
    si = lax.broadcasted_iota(jnp.int32, (c, c), 1)
    txs = ti ^ si
    below = ti > si
    level_masks = [below & (txs >= h) & (txs < 2 * h) for h in _GLA_LEVELS]
    diag = ti == si

    st_ref[...] = jnp.zeros_like(st_ref)

    def chunk(ci, carry):
        r = pl.ds(pl.multiple_of(ci * c, c), c)
        g = g_ref[r, :]
        g_hi = g.astype(BF16)
        g_lo = (g - g_hi.astype(F32)).astype(BF16)
        f = jnp.exp(_dot(t_ref[...], jnp.concatenate([g_hi, g_lo], axis=0)))
        q = q_ref[r, :] * scale
        k = k_ref[r, :]
        v = v_ref[r, :].astype(BF16)
        a = jnp.where(diag, _dot_nt(q.astype(BF16), k.astype(BF16)), 0.0)
        for lvl in range(len(_GLA_LEVELS)):
            fl = f[2 * c + lvl * c: 3 * c + lvl * c]
            a = a + jnp.where(level_masks[lvl],
                              _dot_nt((q * fl).astype(BF16), (k * fl).astype(BF16)), 0.0)
        st = st_ref[...]
        qe = (q * f[0:c]).astype(BF16)
        o_ref[r, :] = _dot_nt(qe, st.astype(BF16)) + _dot(a.astype(BF16), v)
        kd = (k * f[c:2 * c]).astype(BF16)
        st_ref[...] = st * f[c - 1:c] + _dot_tn(v, kd)
        return carry

    lax.fori_loop(0, q_ref.shape[0] // c, chunk, 0)
    s_ref[0, 0] = st_ref[...].T


def _gla_prompt(p, la, tmat, batch, seq):
    return pl.pallas_call(
        _gla_kernel,
        grid=(batch, H_GLA),
        in_specs=[
            pl.BlockSpec((seq, DK_GLA), lambda b, h: (b, COL_QG // DK_GLA + h)),
            pl.BlockSpec((seq, DK_GLA), lambda b, h: (b, COL_KG // DK_GLA + h)),
            pl.BlockSpec((seq, DV_GLA), lambda b, h: (b, COL_VG // DV_GLA + h)),
            pl.BlockSpec((seq, DK_GLA), lambda b, h: (b, h)),
            pl.BlockSpec(tmat.shape, lambda b, h: (0, 0)),
        ],
        out_specs=[
            pl.BlockSpec((seq, DV_GLA), lambda b, h: (b, h)),
            pl.BlockSpec((1, 1, DK_GLA, DV_GLA), lambda b, h: (b, h, 0, 0)),
        ],
        out_shape=[
            jax.ShapeDtypeStruct((batch * seq, D_GLA_V), F32),
            jax.ShapeDtypeStruct((batch, H_GLA, DK_GLA, DV_GLA), F32),
        ],
        scratch_shapes=[pltpu.VMEM((DV_GLA, DK_GLA), F32)],
        compiler_params=_params("parallel", "parallel"),
        name="gla_prompt",
    )(p, p, p, la, tmat)


def _merge_kernel(x_ref, oa_ref, og_ref, rg_ref, ga_ref, gg_ref, w_ref, o_ref):
    a = (_rms(oa_ref[...]) * ga_ref[...]).astype(BF16)
    parts = []
    for h in range(H_GLA):
        cols = slice(h * DV_GLA, (h + 1) * DV_GLA)
        r = rg_ref[:, cols]
        parts.append((_rms(og_ref[:, cols]) * gg_ref[...] * (r * jax.nn.sigmoid(r))).astype(BF16))
    gg = jnp.concatenate(parts, axis=-1)
    o_ref[...] = x_ref[...] + _dot(a, w_ref[0:D_ATT, :]) + _dot(gg, w_ref[D_ATT:, :])


def _merge(x, o_att, o_gla, p, g_att, g_gla, w_out, *, tm):
    m = x.shape[0]
    return pl.pallas_call(
        _merge_kernel,
        grid=(m // tm,),
        in_specs=[
            pl.BlockSpec((tm, D_MODEL), lambda i: (i, 0)),
            pl.BlockSpec((tm, D_ATT), lambda i: (i, 0)),
            pl.BlockSpec((tm, D_GLA_V), lambda i: (i, 0)),
            pl.BlockSpec((tm, D_GLA_V), lambda i: (i, COL_RG // D_GLA_V)),
            pl.BlockSpec((1, D_ATT), lambda i: (0, 0)),
            pl.BlockSpec((1, DV_GLA), lambda i: (0, 0)),
            pl.BlockSpec((D_MODEL, D_MODEL), lambda i: (0, 0)),
        ],
        out_specs=pl.BlockSpec((tm, D_MODEL), lambda i: (i, 0)),
        out_shape=jax.ShapeDtypeStruct((m, D_MODEL), F32),
        compiler_params=_params("parallel"),
        name="merge",
    )(x, o_att, o_gla, p, g_att, g_gla, w_out)


SATT_HEADS_PER_STEP = H_ATT


def _sattn_kernel(q_ref, kn_ref, vn_ref, k_ref, v_ref, o_ref):
    scale = HD_ATT ** -0.5
    wb = k_ref.shape[-1]
    t = lax.broadcasted_iota(jnp.int32, (1, wb), 1)
    cnt = jnp.zeros((1, wb), F32)
    for w, d in DIL_PATTERNS:
        cnt = cnt + jnp.where((t >= wb - w) & ((t & (d - 1)) == 0), 1.0, 0.0)
    bias = jnp.where(cnt > 0.0, 0.0, jnp.float32(-jnp.inf))
    n_pat = float(len(DIL_PATTERNS))
    n_heads = k_ref.shape[1]
    q = q_ref[0] * scale
    q16 = q.astype(BF16)
    s_new_all = jnp.sum(q * kn_ref[0], axis=-1, keepdims=True)
    for h in range(n_heads):
        hrow = slice(h, h + 1)
        s = _dot(q16, k_ref[0, h].astype(BF16))[hrow, :] + bias
        s_new = s_new_all[hrow, :]
        m = jnp.maximum(jnp.max(s, axis=-1, keepdims=True), s_new)
        e = cnt * jnp.exp(s - m)
        e_new = n_pat * jnp.exp(s_new - m)
        den = jnp.sum(e, axis=-1, keepdims=True) + e_new
        e_rows = jnp.broadcast_to(e, (n_heads, wb)).astype(BF16)
        num = _dot_nt(e_rows, v_ref[0, h].astype(BF16))[0:1, :] + e_new * vn_ref[0, hrow, :]
        o_ref[0, hrow, :] = num / den


def _attn_sample(q, kn, vn, cache_kt, cache_vt):
    nb, _, _, wb = cache_kt.shape
    assert all(wb % d == 0 and w <= wb for w, d in DIL_PATTERNS)
    hs = SATT_HEADS_PER_STEP
    row = pl.BlockSpec((1, hs, HD_ATT), lambda b, h: (b, h, 0))
    cache = pl.BlockSpec((1, hs, HD_ATT, wb), lambda b, h: (b, h, 0, 0))
    return pl.pallas_call(
        _sattn_kernel,
        grid=(nb, H_ATT // hs),
        in_specs=[row, row, row, cache, cache],
        out_specs=row,
        out_shape=jax.ShapeDtypeStruct((nb, H_ATT, HD_ATT), F32),
        compiler_params=_params("parallel", "parallel"),
        name="attn_sample",
    )(q, kn, vn, cache_kt, cache_vt)


def _sgla_kernel(q_ref, k_ref, g_ref, v_ref, s_ref, so_ref, o_ref):
    scale = DK_GLA ** -0.5
    for h in range(H_GLA):
        s_new = jnp.exp(g_ref[0, h]) * s_ref[0, h] + k_ref[0, h] * v_ref[0, h]
        so_ref[0, h] = s_new
        o_ref[0, h] = jnp.sum((q_ref[0, h] * scale) * s_new, axis=0, keepdims=True)


def _gla_sample(q, k, g, v, state):
    nb = state.shape[0]
    col = pl.BlockSpec((1, H_GLA, DK_GLA, 1), lambda b: (b, 0, 0, 0))
    return pl.pallas_call(
        _sgla_kernel,
        grid=(nb,),
        in_specs=[col, col, col,
                  pl.BlockSpec((1, H_GLA, 1, DV_GLA), lambda b: (b, 0, 0, 0)),
                  pl.BlockSpec((1, H_GLA, DK_GLA, DV_GLA), lambda b: (b, 0, 0, 0))],
        out_specs=[pl.BlockSpec((1, H_GLA, DK_GLA, DV_GLA), lambda b: (b, 0, 0, 0)),
                   pl.BlockSpec((1, H_GLA, 1, DV_GLA), lambda b: (b, 0, 0, 0))],
        out_shape=[jax.ShapeDtypeStruct((nb, H_GLA, DK_GLA, DV_GLA), F32),
                   jax.ShapeDtypeStruct((nb, H_GLA, 1, DV_GLA), F32)],
        compiler_params=_params("parallel"),
        name="gla_sample",
    )(q, k, g, v, state)


def kernel(x_prompt, x_sample, cache_att_k, cache_att_v, state_gla, g_ffn1, w_ffn1_in, w_ffn1_out, g_mix, w_in, w_gate2, b_gate2, g_att_out, g_gla_out, w_out, g_ffn2, w_ffn2_in, w_ffn2_out, g_final):
    depth = w_in.shape[0]
    assert depth == 1
    batch, seq, _ = x_prompt.shape
    nb, dec_seq, _ = x_sample.shape
    assert dec_seq == 1
    xp = x_prompt.reshape(batch * seq, D_MODEL)
    xs = x_sample.reshape(nb, D_MODEL)
    row = lambda a: a.reshape(1, -1)
    l = 0

    w_in_t = w_in[l].T
    w_lr = w_in_t[D_PROJ_MAIN:].astype(BF16)
    w_g2 = w_gate2[l].astype(BF16)
    b_g2 = row(b_gate2[l])
    w_o = w_out[l].astype(BF16)
    gf = row(g_final)
    tmat = jnp.asarray(_gla_exponent_matrix(), dtype=BF16)

    xs, w1a, w1b, w1o = _ffn(xs, row(g_ffn1[l]), w_ffn1_in[l], w_ffn1_in[l], w_ffn1_out[l], gf,
                             tm=nb, tf=512, final_norm=False, emit_bf16=True)
    ps, las, w_main = _proj(xs, row(g_mix[l]), w_in_t, w_lr, w_g2, b_g2, tm=nb, tn=512, emit_bf16=True)
    heads = lambda c0: ps[:, c0:c0 + D_ATT].reshape(nb, H_ATT, HD_ATT)
    k_new, v_new = heads(COL_KA), heads(COL_VA)
    o_att_s = _attn_sample(heads(COL_QA), k_new, v_new,
                           jnp.transpose(cache_att_k[l], (0, 2, 3, 1)),
                           jnp.transpose(cache_att_v[l], (0, 2, 3, 1)))
    col = lambda a: a.reshape(nb, H_GLA, DK_GLA, 1)
    s_new, o_gla_s = _gla_sample(
        col(ps[:, COL_QG:COL_QG + D_GLA_K]), col(ps[:, COL_KG:COL_KG + D_GLA_K]), col(las),
        ps[:, COL_VG:COL_VG + D_GLA_V].reshape(nb, H_GLA, 1, DV_GLA), state_gla[l])
    xs = _merge(xs, o_att_s.reshape(nb, D_ATT), o_gla_s.reshape(nb, D_GLA_V), ps,
                row(g_att_out[l]), row(g_gla_out[l]), w_o, tm=nb)
    ys, w2a, w2b, w2o = _ffn(xs, row(g_ffn2[l]), w_ffn2_in[l], w_ffn2_in[l], w_ffn2_out[l], gf,
                             tm=nb, tf=512, final_norm=True, emit_bf16=True)
    nk_s = k_new.reshape(1, nb, 1, H_ATT, HD_ATT)
    nv_s = v_new.reshape(1, nb, 1, H_ATT, HD_ATT)

    xp = _ffn(xp, row(g_ffn1[l]), w1a, w1b, w1o, gf, tm=512, tf=512, final_norm=False)
    pp, lap = _proj(xp, row(g_mix[l]), w_main, w_lr, w_g2, b_g2, tm=1024, tn=1024)
    o_att, kt_p, vt_p = _attn_prompt(pp, batch, seq)
    o_gla, s_fin = _gla_prompt(pp, lap, tmat, batch, seq)
    xp = _merge(xp, o_att, o_gla, pp, row(g_att_out[l]), row(g_gla_out[l]), w_o, tm=256)
    yp = _ffn(xp, row(g_ffn2[l]), w2a, w2b, w2o, gf, tm=512, tf=512, final_norm=True)
    nk_p = jnp.transpose(kt_p, (0, 3, 1, 2))[None]
    nv_p = jnp.transpose(vt_p, (0, 3, 1, 2))[None]

    return (yp.reshape(batch, seq, D_MODEL), ys.reshape(nb, 1, D_MODEL), nk_p, nv_p,
            s_fin[None], nk_s, nv_s, s_new[None])
```

```python
import functools

import numpy as np
import jax
import jax.numpy as jnp
from jax import lax
from jax.experimental import pallas as pl
from jax.experimental.pallas import tpu as pltpu

F32 = jnp.float32
BF16 = jnp.bfloat16

D_MODEL = 2048
D_FF = 5632
D_ATT = 1024
HD_ATT = 64
H_ATT = 16
H_GLA = 4
DK_GLA = 128
DV_GLA = 256
D_GLA_K = H_GLA * DK_GLA
D_GLA_V = H_GLA * DV_GLA
GATE_RANK = 16
GATE_TAU = 16.0
NORM_EPS = 1e-6
DIL_PATTERNS = ((128, 1), (512, 4), (2048, 16))
ATT_BLOCK = 128
ATT_GROUP = 16
GLA_CHUNK = 64
D_PROJ_MAIN = 3 * D_ATT + 2 * D_GLA_K + 2 * D_GLA_V

COL_QA, COL_KA, COL_VA = 0, D_ATT, 2 * D_ATT
COL_QG = 3 * D_ATT
COL_KG = COL_QG + D_GLA_K
COL_VG = COL_KG + D_GLA_K
COL_RG = COL_VG + D_GLA_V

VMEM_LIMIT_BYTES = 56 * 1024 * 1024


def _rms(x):
    return x * lax.rsqrt(jnp.mean(x * x, axis=-1, keepdims=True) + NORM_EPS)


def _dot(a, b):
    return jnp.dot(a, b, preferred_element_type=F32)


def _dot_nt(a, b):
    return lax.dot_general(a, b, (((1,), (1,)), ((), ())), preferred_element_type=F32)


def _dot_tn(a, b):
    return lax.dot_general(a, b, (((0,), (0,)), ((), ())), preferred_element_type=F32)


def _params(*sem):
    return pltpu.CompilerParams(dimension_semantics=sem, vmem_limit_bytes=VMEM_LIMIT_BYTES)


def _ffn_kernel(x_ref, g_ref, wa_ref, wb_ref, wo_ref, gf_ref, o_ref, *rest, final_norm, emit_bf16):
    h_ref = rest[-1]
    j = pl.program_id(1)

    @pl.when(j == 0)
    def _():
        h_ref[...] = (_rms(x_ref[...]) * g_ref[...]).astype(BF16)
        o_ref[...] = jnp.zeros_like(o_ref)

    wa, wb, wo = wa_ref[...], wb_ref[...], wo_ref[...]
    if emit_bf16:
        wa, wb, wo = wa.astype(BF16), wb.astype(BF16), wo.astype(BF16)
        for dst_ref, w in zip(rest[:3], (wa, wb, wo)):
            dst_ref[...] = w
    h = h_ref[...]
    a = _dot(h, wa)
    b = _dot(h, wb)
    act = (a * jax.nn.sigmoid(a) * b).astype(BF16)
    o_ref[...] += _dot(act, wo)

    @pl.when(j == pl.num_programs(1) - 1)
    def _():
        y = x_ref[...] + 0.5 * o_ref[...]
        if final_norm:
            y = _rms(y) * gf_ref[...]
        o_ref[...] = y


def _ffn(x, g, wa, wb, wo, g_final, *, tm, tf, final_norm, emit_bf16=False):
    m = x.shape[0]
    nf = D_FF // tf
    b_off = nf if wb.shape[1] == 2 * D_FF else 0
    out_specs = [pl.BlockSpec((tm, D_MODEL), lambda i, j: (i, 0))]
    out_shape = [jax.ShapeDtypeStruct((m, D_MODEL), F32)]
    if emit_bf16:
        assert m == tm
        out_specs += [pl.BlockSpec((D_MODEL, tf), lambda i, j: (0, j)),
                      pl.BlockSpec((D_MODEL, tf), lambda i, j: (0, j)),
                      pl.BlockSpec((tf, D_MODEL), lambda i, j: (j, 0))]
        out_shape += [jax.ShapeDtypeStruct((D_MODEL, D_FF), BF16),
                      jax.ShapeDtypeStruct((D_MODEL, D_FF), BF16),
                      jax.ShapeDtypeStruct((D_FF, D_MODEL), BF16)]
    outs = pl.pallas_call(
        functools.partial(_ffn_kernel, final_norm=final_norm, emit_bf16=emit_bf16),
        grid=(m // tm, nf),
        in_specs=[
            pl.BlockSpec((tm, D_MODEL), lambda i, j: (i, 0)),
            pl.BlockSpec((1, D_MODEL), lambda i, j: (0, 0)),
            pl.BlockSpec((D_MODEL, tf), lambda i, j: (0, j)),
            pl.BlockSpec((D_MODEL, tf), lambda i, j: (0, j + b_off)),
            pl.BlockSpec((tf, D_MODEL), lambda i, j: (j, 0)),
            pl.BlockSpec((1, D_MODEL), lambda i, j: (0, 0)),
        ],
        out_specs=out_specs,
        out_shape=out_shape,
        scratch_shapes=[pltpu.VMEM((tm, D_MODEL), BF16)],
        compiler_params=_params("parallel", "arbitrary"),
        name="ffn",
    )(x, g, wa, wb, wo, g_final)
    return outs if emit_bf16 else outs[0]


def _proj_kernel(x_ref, g_ref, w_ref, wlr_ref, wg2_ref, bg2_ref, p_ref, la_ref, *rest, emit_bf16):
    h_ref = rest[-1]
    j = pl.program_id(1)

    @pl.when(j == 0)
    def _():
        h = (_rms(x_ref[...]) * g_ref[...]).astype(BF16)
        h_ref[...] = h
        lr = _dot_nt(h, wlr_ref[...])
        z = _dot(lr.astype(BF16), wg2_ref[...]) + bg2_ref[...]
        log_sig = jnp.minimum(z, 0.0) - jnp.log1p(jnp.exp(-jnp.abs(z)))
        la_ref[...] = log_sig * (1.0 / GATE_TAU)

    w = w_ref[...]
    if emit_bf16:
        w = w.astype(BF16)
        rest[0][...] = w
    p_ref[...] = _dot_nt(h_ref[...], w)


def _proj(x, g, w_t, w_lr_t, w_g2, b_g2, *, tm, tn, emit_bf16=False):
    m = x.shape[0]
    out_specs = [
        pl.BlockSpec((tm, tn), lambda i, j: (i, j)),
        pl.BlockSpec((tm, D_GLA_K), lambda i, j: (i, 0)),
    ]
    out_shape = [
        jax.ShapeDtypeStruct((m, D_PROJ_MAIN), F32),
        jax.ShapeDtypeStruct((m, D_GLA_K), F32),
    ]
    if emit_bf16:
        assert m == tm
        out_specs.append(pl.BlockSpec((tn, D_MODEL), lambda i, j: (j, 0)))
        out_shape.append(jax.ShapeDtypeStruct((D_PROJ_MAIN, D_MODEL), BF16))
    return pl.pallas_call(
        functools.partial(_proj_kernel, emit_bf16=emit_bf16),
        grid=(m // tm, D_PROJ_MAIN // tn),
        in_specs=[
            pl.BlockSpec((tm, D_MODEL), lambda i, j: (i, 0)),
            pl.BlockSpec((1, D_MODEL), lambda i, j: (0, 0)),
            pl.BlockSpec((tn, D_MODEL), lambda i, j: (j, 0)),
            pl.BlockSpec((GATE_RANK, D_MODEL), lambda i, j: (0, 0)),
            pl.BlockSpec((GATE_RANK, D_GLA_K), lambda i, j: (0, 0)),
            pl.BlockSpec((1, D_GLA_K), lambda i, j: (0, 0)),
        ],
        out_specs=out_specs,
        out_shape=out_shape,
        scratch_shapes=[pltpu.VMEM((tm, D_MODEL), BF16)],
        compiler_params=_params("parallel", "arbitrary"),
        name="proj",
    )(x, g, w_t, w_lr_t, w_g2, b_g2)


def _attn_kernel(q_ref, k_ref, v_ref, o_ref, kt_ref, vt_ref,
                 qp_ref, kp_ref, vp_ref, m_ref, l_ref, acc_ref, s_ref, ms_ref):
    blk = ATT_BLOCK
    for src_ref, dst_ref in ((k_ref, kt_ref), (v_ref, vt_ref)):
        t = src_ref[...].T
        dst_ref[0, 0] = t[:HD_ATT]
        dst_ref[0, 1] = t[HD_ATT:]

    scale = HD_ATT ** -0.5
    seq = q_ref.shape[0]
    ns = DIL_PATTERNS[-1][1]
    lane_lo = lax.broadcasted_iota(jnp.int32, (1, 2 * HD_ATT), 1) < HD_ATT
    neg = jnp.float32(-jnp.inf)

    assert ns == 16
    for src_ref, tmp_ref, dst_ref in ((q_ref, m_ref, qp_ref), (k_ref, l_ref, kp_ref), (v_ref, acc_ref, vp_ref)):
        for r4 in range(4):
            x = src_ref[pl.ds(r4, seq // 4, stride=4), :]
            tmp_ref[pl.ds(r4 * (seq // 4), seq // 4), :] = x * scale if src_ref is q_ref else x
        for r4 in range(4):
            for a in range(4):
                dst_ref[pl.ds((4 * a + r4) * blk, blk), :] = tmp_ref[pl.ds(r4 * (seq // 4) + a, blk, stride=4), :]

    def run_pattern(d, first_pattern):
        na = ns // d
        plen = blk // na
        per_stream = seq // (d * blk)

        def offset(idx):
            return na * (idx & (plen - 1)) + idx // plen

        qpos = offset(lax.broadcasted_iota(jnp.int32, (blk, blk), 0))
        kpos = offset(lax.broadcasted_iota(jnp.int32, (blk, blk), 1))
        bias_cur = jnp.where(kpos <= qpos, 0.0, neg)
        if per_stream > 1:
            bias_prev = jnp.where(kpos >= qpos, 0.0, neg)
            bias_band = jnp.concatenate([bias_prev, bias_cur], axis=1)
            bias_first = jnp.concatenate([jnp.full((blk, blk), neg, F32), bias_cur], axis=1)

        def pieces(rd, n):
            return [pl.ds((a * d + rd) * blk + plen * n, plen) for a in range(na)]

        def gather(ref, ps):
            return jnp.concatenate([ref[p, :] for p in ps], axis=0)

        nk = 2 * blk if per_stream > 1 else blk
        heads = (lane_lo, jnp.logical_not(lane_lo))

        def keys(ref, rd, n):
            cur = gather(ref, pieces(rd, n))
            if per_stream == 1:
                return cur.astype(BF16)
            prev = gather(ref, pieces(rd, max(n - 1, 0)))
            return jnp.concatenate([prev, cur], axis=0).astype(BF16)

        def group(g):
            blocks = [((g * ATT_GROUP + b) % d, (g * ATT_GROUP + b) // d) for b in range(ATT_GROUP)]
            for b, (rd, n) in enumerate(blocks):
                q = gather(qp_ref, pieces(rd, n))
                kk = keys(kp_ref, rd, n)
                bias = bias_cur if per_stream == 1 else (bias_band if n > 0 else bias_first)
                for h, sel in enumerate(heads):
                    qh = jnp.where(sel, q, 0.0).astype(BF16)
                    s_ref[b, h, :, :nk] = _dot_nt(qh, kk) + bias
            for b in range(ATT_GROUP):
                for h in range(2):
                    m = jnp.max(s_ref[b, h, :, :nk], axis=-1, keepdims=True)
                    ms_ref[b, h] = jnp.broadcast_to(m, (blk, blk))
            for b, (rd, n) in enumerate(blocks):
                vv = jnp.concatenate([keys(vp_ref, rd, n), jnp.ones((nk, blk), BF16)], axis=1)
                res = []
                for h in range(2):
                    mh = ms_ref[b, h]
                    mh = jnp.concatenate([mh, mh], axis=1) if nk == 2 * blk else mh
                    res.append(_dot(jnp.exp(s_ref[b, h, :, :nk] - mh).astype(BF16), vv))
                u = jnp.where(lane_lo, res[0][:, :blk], res[1][:, :blk])
                l = jnp.where(lane_lo, res[0][:, blk:], res[1][:, blk:])
                m = jnp.where(lane_lo, ms_ref[b, 0], ms_ref[b, 1])
                for a, p in enumerate(pieces(rd, n)):
                    sl = slice(a * plen, (a + 1) * plen)
                    if first_pattern:
                        m_ref[p, :] = m[sl]
                        l_ref[p, :] = l[sl]
                        acc_ref[p, :] = u[sl]
                    else:
                        m_old = m_ref[p, :]
                        m_new = jnp.maximum(m_old, m[sl])
                        a_old = jnp.exp(m_old - m_new)
                        a_blk = jnp.exp(m[sl] - m_new)
                        m_ref[p, :] = m_new
                        l_ref[p, :] = a_old * l_ref[p, :] + a_blk * l[sl]
                        acc_ref[p, :] = a_old * acc_ref[p, :] + a_blk * u[sl]

        for g in range(d * per_stream // ATT_GROUP):
            group(g)

    for idx, (_, d) in enumerate(DIL_PATTERNS):
        run_pattern(d, idx == 0)

    for r4 in range(4):
        for a in range(4):
            rows = pl.ds((4 * a + r4) * blk, blk)
            qp_ref[pl.ds(r4 * (seq // 4) + a, blk, stride=4), :] = acc_ref[rows, :] / l_ref[rows, :]
    for r4 in range(4):
        o_ref[pl.ds(r4, seq // 4, stride=4), :] = qp_ref[pl.ds(r4 * (seq // 4), seq // 4), :]


def _attn_prompt(p, batch, seq):
    ns = DIL_PATTERNS[-1][1]
    assert seq == ns * ATT_BLOCK
    assert all(w // d == ATT_BLOCK and ns % d == 0 and ATT_BLOCK * d // ns >= 8 for w, d in DIL_PATTERNS)
    lanes = 2 * HD_ATT
    n_pairs = H_ATT // 2
    return pl.pallas_call(
        _attn_kernel,
        grid=(batch, n_pairs),
        in_specs=[
            pl.BlockSpec((seq, lanes), lambda b, h: (b, COL_QA // lanes + h)),
            pl.BlockSpec((seq, lanes), lambda b, h: (b, COL_KA // lanes + h)),
            pl.BlockSpec((seq, lanes), lambda b, h: (b, COL_VA // lanes + h)),
        ],
        out_specs=[
            pl.BlockSpec((seq, lanes), lambda b, h: (b, h)),
            pl.BlockSpec((1, 2, HD_ATT, seq), lambda b, h: (b, h, 0, 0)),
            pl.BlockSpec((1, 2, HD_ATT, seq), lambda b, h: (b, h, 0, 0)),
        ],
        out_shape=[
            jax.ShapeDtypeStruct((batch * seq, D_ATT), F32),
            jax.ShapeDtypeStruct((batch, H_ATT, HD_ATT, seq), F32),
            jax.ShapeDtypeStruct((batch, H_ATT, HD_ATT, seq), F32),
        ],
        scratch_shapes=[pltpu.VMEM((seq, lanes), F32)] * 6 + [
            pltpu.VMEM((ATT_GROUP, 2, ATT_BLOCK, 2 * ATT_BLOCK), F32),
            pltpu.VMEM((ATT_GROUP, 2, ATT_BLOCK, lanes), F32),
        ],
        compiler_params=_params("parallel", "parallel"),
        name="attn_prompt",
    )(p, p, p)


_GLA_LEVELS = (32, 16, 8, 4, 2, 1)


def _gla_exponent_matrix():
    c = GLA_CHUNK
    t = np.arange(c)[:, None]
    u = np.arange(c)[None, :]
    mats = [(u <= t), (u > t)]
    for h in _GLA_LEVELS:
        mid = (t // (2 * h)) * (2 * h) + h - 1
        upper = (t % (2 * h)) >= h
        mats.append(np.where(upper, (u > mid) & (u <= t), (u > t) & (u <= mid)))
    tmat = np.concatenate(mats, axis=0).astype(np.float32)
    return np.concatenate([tmat, tmat], axis=1)


def _gla_kernel(q_ref, k_ref, v_ref, g_ref, t_ref, o_ref, s_ref, st_ref):
    c = GLA_CHUNK
    scale = DK_GLA ** -0.5
    ti = lax.broadcasted_iota(jnp.int32, (c, c), 0)
    si = lax.broadcasted_iota(jnp.int32, (c, c), 1)
    txs = ti ^ si
    below = ti > si
    level_masks = [below & (txs >= h) & (txs < 2 * h) for h in _GLA_LEVELS]
    diag = ti == si

    st_ref[...] = jnp.zeros_like(st_ref)

    def chunk(ci, carry):
        r = pl.ds(pl.multiple_of(ci * c, c), c)
        g = g_ref[r, :]
        g_hi = g.astype(BF16)
        g_lo = (g - g_hi.astype(F32)).astype(BF16)
        f = jnp.exp(_dot(t_ref[...], jnp.concatenate([g_hi, g_lo], axis=0)))
        q = q_ref[r, :] * scale
        k = k_ref[r, :]
        v = v_ref[r, :].astype(BF16)
        a = jnp.where(diag, _dot_nt(q.astype(BF16), k.astype(BF16)), 0.0)
        for lvl in range(len(_GLA_LEVELS)):
            fl = f[2 * c + lvl * c: 3 * c + lvl * c]
            a = a + jnp.where(level_masks[lvl],
                              _dot_nt((q * fl).astype(BF16), (k * fl).astype(BF16)), 0.0)
        st = st_ref[...]
        qe = (q * f[0:c]).astype(BF16)
        o_ref[r, :] = _dot_nt(qe, st.astype(BF16)) + _dot(a.astype(BF16), v)
        kd = (k * f[c:2 * c]).astype(BF16)
        st_ref[...] = st * f[c - 1:c] + _dot_tn(v, kd)
        return carry

    lax.fori_loop(0, q_ref.shape[0] // c, chunk, 0, unroll=8)
    s_ref[0, 0] = st_ref[...].T


def _gla_prompt(p, la, tmat, batch, seq):
    return pl.pallas_call(
        _gla_kernel,
        grid=(batch, H_GLA),
        in_specs=[
            pl.BlockSpec((seq, DK_GLA), lambda b, h: (b, COL_QG // DK_GLA + h)),
            pl.BlockSpec((seq, DK_GLA), lambda b, h: (b, COL_KG // DK_GLA + h)),
            pl.BlockSpec((seq, DV_GLA), lambda b, h: (b, COL_VG // DV_GLA + h)),
            pl.BlockSpec((seq, DK_GLA), lambda b, h: (b, h)),
            pl.BlockSpec(tmat.shape, lambda b, h: (0, 0)),
        ],
        out_specs=[
            pl.BlockSpec((seq, DV_GLA), lambda b, h: (b, h)),
            pl.BlockSpec((1, 1, DK_GLA, DV_GLA), lambda b, h: (b, h, 0, 0)),
        ],
        out_shape=[
            jax.ShapeDtypeStruct((batch * seq, D_GLA_V), F32),
            jax.ShapeDtypeStruct((batch, H_GLA, DK_GLA, DV_GLA), F32),
        ],
        scratch_shapes=[pltpu.VMEM((DV_GLA, DK_GLA), F32)],
        compiler_params=_params("parallel", "parallel"),
        name="gla_prompt",
    )(p, p, p, la, tmat)


def _merge_kernel(x_ref, oa_ref, og_ref, rg_ref, ga_ref, gg_ref, w_ref, o_ref):
    a = (_rms(oa_ref[...]) * ga_ref[...]).astype(BF16)
    parts = []
    for h in range(H_GLA):
        cols = slice(h * DV_GLA, (h + 1) * DV_GLA)
        r = rg_ref[:, cols]
        parts.append((_rms(og_ref[:, cols]) * gg_ref[...] * (r * jax.nn.sigmoid(r))).astype(BF16))
    gg = jnp.concatenate(parts, axis=-1)
    o_ref[...] = x_ref[...] + _dot(a, w_ref[0:D_ATT, :]) + _dot(gg, w_ref[D_ATT:, :])


def _merge(x, o_att, o_gla, p, g_att, g_gla, w_out, *, tm):
    m = x.shape[0]
    return pl.pallas_call(
        _merge_kernel,
        grid=(m // tm,),
        in_specs=[
            pl.BlockSpec((tm, D_MODEL), lambda i: (i, 0)),
            pl.BlockSpec((tm, D_ATT), lambda i: (i, 0)),
            pl.BlockSpec((tm, D_GLA_V), lambda i: (i, 0)),
            pl.BlockSpec((tm, D_GLA_V), lambda i: (i, COL_RG // D_GLA_V)),
            pl.BlockSpec((1, D_ATT), lambda i: (0, 0)),
            pl.BlockSpec((1, DV_GLA), lambda i: (0, 0)),
            pl.BlockSpec((D_MODEL, D_MODEL), lambda i: (0, 0)),
        ],
        out_specs=pl.BlockSpec((tm, D_MODEL), lambda i: (i, 0)),
        out_shape=jax.ShapeDtypeStruct((m, D_MODEL), F32),
        compiler_params=_params("parallel"),
        name="merge",
    )(x, o_att, o_gla, p, g_att, g_gla, w_out)


SATT_HEADS_PER_STEP = H_ATT


def _sattn_kernel(q_ref, kn_ref, vn_ref, k_ref, v_ref, o_ref, s_ref):
    scale = HD_ATT ** -0.5
    wb = k_ref.shape[-1]
    t = lax.broadcasted_iota(jnp.int32, (1, wb), 1)
    cnt = jnp.zeros((1, wb), F32)
    for w, d in DIL_PATTERNS:
        cnt = cnt + jnp.where((t >= wb - w) & ((t & (d - 1)) == 0), 1.0, 0.0)
    bias = jnp.where(cnt > 0.0, 0.0, jnp.float32(-jnp.inf))
    n_pat = float(len(DIL_PATTERNS))
    n_heads = k_ref.shape[1]
    q = q_ref[0] * scale
    q16 = q.astype(BF16)
    s_new = jnp.sum(q * kn_ref[0], axis=-1, keepdims=True)
    for h in range(n_heads):
        s_ref[h:h + 1, :] = _dot(q16, k_ref[0, h].astype(BF16))[h:h + 1, :]
    s = s_ref[...] + bias
    m = jnp.maximum(jnp.max(s, axis=-1, keepdims=True), s_new)
    e = cnt * jnp.exp(s - m)
    e_new = n_pat * jnp.exp(s_new - m)
    den = jnp.sum(e, axis=-1, keepdims=True) + e_new
    e16 = e.astype(BF16)
    head = lax.broadcasted_iota(jnp.int32, (n_heads, HD_ATT), 0)
    num = e_new * vn_ref[0]
    for h in range(n_heads):
        num = num + jnp.where(head == h, _dot_nt(e16, v_ref[0, h].astype(BF16)), 0.0)
    o_ref[0] = num / den


def _attn_sample(q, kn, vn, cache_kt, cache_vt):
    nb, _, _, wb = cache_kt.shape
    assert all(wb % d == 0 and w <= wb for w, d in DIL_PATTERNS)
    hs = SATT_HEADS_PER_STEP
    row = pl.BlockSpec((1, hs, HD_ATT), lambda b, h: (b, h, 0))
    cache = pl.BlockSpec((1, hs, HD_ATT, wb), lambda b, h: (b, h, 0, 0))
    return pl.pallas_call(
        _sattn_kernel,
        grid=(nb, H_ATT // hs),
        in_specs=[row, row, row, cache, cache],
        out_specs=row,
        out_shape=jax.ShapeDtypeStruct((nb, H_ATT, HD_ATT), F32),
        scratch_shapes=[pltpu.VMEM((hs, wb), F32)],
        compiler_params=_params("parallel", "parallel"),
        name="attn_sample",
    )(q, kn, vn, cache_kt, cache_vt)


SGLA_SEQS_PER_STEP = 4
_SPLIT = 3


def _bf16_pieces(x):
    pieces = []
    for _ in range(_SPLIT):
        p = x.astype(BF16)
        pieces.append(p)
        x = x - p.astype(F32)
    return pieces


def _sgla_kernel(q_ref, k_ref, g_ref, v_ref, s_ref, so_ref, o_ref):
    scale = DK_GLA ** -0.5
    n_vec = 3
    sel_rows = 16
    r = lax.broadcasted_iota(jnp.int32, (sel_rows, n_vec * DV_GLA), 0)
    c = lax.broadcasted_iota(jnp.int32, (sel_rows, n_vec * DV_GLA), 1)
    selector = jnp.where((r // _SPLIT == c // DV_GLA) & (r < n_vec * _SPLIT), 1.0, 0.0).astype(BF16)
    pad = jnp.zeros((sel_rows - n_vec * _SPLIT, DK_GLA), BF16)
    for s in range(q_ref.shape[0]):
        for h in range(H_GLA):
            hrow = slice(h, h + 1)
            vecs = (jnp.exp(g_ref[s, hrow, :]), k_ref[s, hrow, :], q_ref[s, hrow, :] * scale)
            lhs = jnp.concatenate([p for x in vecs for p in _bf16_pieces(x)] + [pad], axis=0)
            cols = _dot_tn(lhs, selector)
            decay, kcol, qcol = (cols[:, i * DV_GLA:(i + 1) * DV_GLA] for i in range(n_vec))
            s_new = decay * s_ref[s, h] + kcol * v_ref[s, hrow, :]
            so_ref[s, h] = s_new
            o_ref[s, hrow, :] = jnp.sum(qcol * s_new, axis=0, keepdims=True)


def _gla_sample(q, k, g, v, state):
    nb = state.shape[0]
    bs = SGLA_SEQS_PER_STEP
    krow = pl.BlockSpec((bs, H_GLA, DK_GLA), lambda b: (b, 0, 0))
    vrow = pl.BlockSpec((bs, H_GLA, DV_GLA), lambda b: (b, 0, 0))
    st = pl.BlockSpec((bs, H_GLA, DK_GLA, DV_GLA), lambda b: (b, 0, 0, 0))
    return pl.pallas_call(
        _sgla_kernel,
        grid=(nb // bs,),
        in_specs=[krow, krow, krow, vrow, st],
        out_specs=[st, vrow],
        out_shape=[jax.ShapeDtypeStruct((nb, H_GLA, DK_GLA, DV_GLA), F32),
                   jax.ShapeDtypeStruct((nb, H_GLA, DV_GLA), F32)],
        compiler_params=_params("parallel"),
        name="gla_sample",
    )(q, k, g, v, state)


def kernel(x_prompt, x_sample, cache_att_k, cache_att_v, state_gla, g_ffn1, w_ffn1_in, w_ffn1_out, g_mix, w_in, w_gate2, b_gate2, g_att_out, g_gla_out, w_out, g_ffn2, w_ffn2_in, w_ffn2_out, g_final):
    depth = w_in.shape[0]
    assert depth == 1
    batch, seq, _ = x_prompt.shape
    nb, dec_seq, _ = x_sample.shape
    assert dec_seq == 1
    xp = x_prompt.reshape(batch * seq, D_MODEL)
    xs = x_sample.reshape(nb, D_MODEL)
    row = lambda a: a.reshape(1, -1)
    l = 0

    w_in_t = w_in[l].T
    w_lr = w_in_t[D_PROJ_MAIN:].astype(BF16)
    w_g2 = w_gate2[l].astype(BF16)
    b_g2 = row(b_gate2[l])
    w_o = w_out[l].astype(BF16)
    gf = row(g_final)
    tmat = jnp.asarray(_gla_exponent_matrix(), dtype=BF16)

    xs, w1a, w1b, w1o = _ffn(xs, row(g_ffn1[l]), w_ffn1_in[l], w_ffn1_in[l], w_ffn1_out[l], gf,
                             tm=nb, tf=512, final_norm=False, emit_bf16=True)
    ps, las, w_main = _proj(xs, row(g_mix[l]), w_in_t, w_lr, w_g2, b_g2, tm=nb, tn=512, emit_bf16=True)
    heads = lambda c0: ps[:, c0:c0 + D_ATT].reshape(nb, H_ATT, HD_ATT)
    k_new, v_new = heads(COL_KA), heads(COL_VA)
    o_att_s = _attn_sample(heads(COL_QA), k_new, v_new,
                           jnp.transpose(cache_att_k[l], (0, 2, 3, 1)),
                           jnp.transpose(cache_att_v[l], (0, 2, 3, 1)))
    krows = lambda a: a.reshape(nb, H_GLA, DK_GLA)
    s_new, o_gla_s = _gla_sample(
        krows(ps[:, COL_QG:COL_QG + D_GLA_K]), krows(ps[:, COL_KG:COL_KG + D_GLA_K]), krows(las),
        ps[:, COL_VG:COL_VG + D_GLA_V].reshape(nb, H_GLA, DV_GLA), state_gla[l])
    xs = _merge(xs, o_att_s.reshape(nb, D_ATT), o_gla_s.reshape(nb, D_GLA_V), ps,
                row(g_att_out[l]), row(g_gla_out[l]), w_o, tm=nb)
    ys, w2a, w2b, w2o = _ffn(xs, row(g_ffn2[l]), w_ffn2_in[l], w_ffn2_in[l], w_ffn2_out[l], gf,
                             tm=nb, tf=512, final_norm=True, emit_bf16=True)
    nk_s = k_new.reshape(1, nb, 1, H_ATT, HD_ATT)
    nv_s = v_new.reshape(1, nb, 1, H_ATT, HD_ATT)

    xp = _ffn(xp, row(g_ffn1[l]), w1a, w1b, w1o, gf, tm=512, tf=512, final_norm=False)
    pp, lap = _proj(xp, row(g_mix[l]), w_main, w_lr, w_g2, b_g2, tm=1024, tn=1024)
    o_att, kt_p, vt_p = _attn_prompt(pp, batch, seq)
    o_gla, s_fin = _gla_prompt(pp, lap, tmat, batch, seq)
    xp = _merge(xp, o_att, o_gla, pp, row(g_att_out[l]), row(g_gla_out[l]), w_o, tm=512)
    yp = _ffn(xp, row(g_ffn2[l]), w2a, w2b, w2o, gf, tm=512, tf=512, final_norm=True)
    nk_p = jnp.transpose(kt_p, (0, 3, 1, 2))[None]
    nv_p = jnp.transpose(vt_p, (0, 3, 1, 2))[None]

    return (yp.reshape(batch, seq, D_MODEL), ys.reshape(nb, 1, D_MODEL), nk_p, nv_p,
            s_fin[None], nk_s, nv_s, s_new[None])
```

```python
import functools

import numpy as np
import jax
import jax.numpy as jnp
from jax import lax
from jax.experimental import pallas as pl
from jax.experimental.pallas import tpu as pltpu

F32 = jnp.float32
BF16 = jnp.bfloat16

D_MODEL = 2048
D_FF = 5632
D_ATT = 1024
HD_ATT = 64
H_ATT = 16
H_GLA = 4
DK_GLA = 128
DV_GLA = 256
D_GLA_K = H_GLA * DK_GLA
D_GLA_V = H_GLA * DV_GLA
GATE_RANK = 16
GATE_TAU = 16.0
NORM_EPS = 1e-6
DIL_PATTERNS = ((128, 1), (512, 4), (2048, 16))
ATT_BLOCK = 128
ATT_GROUP = 16
GLA_CHUNK = 64
D_PROJ_MAIN = 3 * D_ATT + 2 * D_GLA_K + 2 * D_GLA_V

COL_QA, COL_KA, COL_VA = 0, D_ATT, 2 * D_ATT
COL_QG = 3 * D_ATT
COL_KG = COL_QG + D_GLA_K
COL_VG = COL_KG + D_GLA_K
COL_RG = COL_VG + D_GLA_V

VMEM_LIMIT_BYTES = 56 * 1024 * 1024


def _rms(x):
    return x * lax.rsqrt(jnp.mean(x * x, axis=-1, keepdims=True) + NORM_EPS)


def _dot(a, b):
    return jnp.dot(a, b, preferred_element_type=F32)


def _dot_nt(a, b):
    return lax.dot_general(a, b, (((1,), (1,)), ((), ())), preferred_element_type=F32)


def _dot_tn(a, b):
    return lax.dot_general(a, b, (((0,), (0,)), ((), ())), preferred_element_type=F32)


def _params(*sem):
    return pltpu.CompilerParams(dimension_semantics=sem, vmem_limit_bytes=VMEM_LIMIT_BYTES)


def _ffn_step(x_ref, g_ref, wa, wb, wo, gf_ref, o_ref, h_ref, final_norm, side_jobs=None):
    j = pl.program_id(1)

    @pl.when(j == 0)
    def _():
        h_ref[...] = (_rms(x_ref[...]) * g_ref[...]).astype(BF16)
        o_ref[...] = jnp.zeros_like(o_ref)

    h = h_ref[...]
    a = _dot(h, wa)
    b = _dot(h, wb)
    act = (a * jax.nn.sigmoid(a) * b).astype(BF16)
    o_ref[...] += _dot(act, wo)
    if side_jobs is not None:
        side_jobs()

    @pl.when(j == pl.num_programs(1) - 1)
    def _():
        y = x_ref[...] + 0.5 * o_ref[...]
        if final_norm:
            y = _rms(y) * gf_ref[...]
        o_ref[...] = y


def _ffn_kernel(x_ref, g_ref, wa_ref, wb_ref, wo_ref, gf_ref, o_ref, *rest, final_norm, emit_bf16):
    wa, wb, wo = wa_ref[...], wb_ref[...], wo_ref[...]
    if emit_bf16:
        wa, wb, wo = wa.astype(BF16), wb.astype(BF16), wo.astype(BF16)
        for dst_ref, w in zip(rest[:3], (wa, wb, wo)):
            dst_ref[...] = w
    _ffn_step(x_ref, g_ref, wa, wb, wo, gf_ref, o_ref, rest[-1], final_norm)


def _ffn_side_kernel(x_ref, g_ref, wa_ref, wb_ref, wo_ref, gf_ref,
                     q_ref, kn_ref, vn_ref, k_ref, v_ref, cwi_ref, cwo_ref,
                     o_ref, so_ref, cwi16_ref, cwo16_ref, h_ref, s_ref, *, final_norm, n_side_blocks):
    def side_jobs():
        hs = k_ref.shape[1]
        groups = H_ATT // hs
        step = pl.program_id(0) * pl.num_programs(1) + pl.program_id(1)
        blk = jnp.minimum(step, n_side_blocks - 1)
        heads = pl.ds(pl.multiple_of((blk % groups) * hs, hs), hs)
        so_ref[0, heads, :] = _sattn_heads(q_ref[0, heads, :], kn_ref[0, heads, :], vn_ref[0, heads, :],
                                           k_ref, v_ref, s_ref)
        cwi16_ref[...] = cwi_ref[...].astype(BF16)
        cwo16_ref[...] = cwo_ref[...].astype(BF16)

    _ffn_step(x_ref, g_ref, wa_ref[...], wb_ref[...], wo_ref[...], gf_ref, o_ref, h_ref, final_norm, side_jobs)


def _ffn(x, g, wa, wb, wo, g_final, *, tm, tf, final_norm, emit_bf16=False, side=None):
    m = x.shape[0]
    nf = D_FF // tf
    ni = m // tm
    b_off = nf if wb.shape[1] == 2 * D_FF else 0
    in_specs = [
        pl.BlockSpec((tm, D_MODEL), lambda i, j: (i, 0)),
        pl.BlockSpec((1, D_MODEL), lambda i, j: (0, 0)),
        pl.BlockSpec((D_MODEL, tf), lambda i, j: (0, j)),
        pl.BlockSpec((D_MODEL, tf), lambda i, j: (0, j + b_off)),
        pl.BlockSpec((tf, D_MODEL), lambda i, j: (j, 0)),
        pl.BlockSpec((1, D_MODEL), lambda i, j: (0, 0)),
    ]
    out_specs = [pl.BlockSpec((tm, D_MODEL), lambda i, j: (i, 0))]
    out_shape = [jax.ShapeDtypeStruct((m, D_MODEL), F32)]
    scratch = [pltpu.VMEM((tm, D_MODEL), BF16)]
    operands = (x, g, wa, wb, wo, g_final)
    if side is None:
        body = functools.partial(_ffn_kernel, final_norm=final_norm, emit_bf16=emit_bf16)
        sem = ("parallel", "arbitrary")
    if emit_bf16:
        assert m == tm and side is None
        out_specs += [pl.BlockSpec((D_MODEL, tf), lambda i, j: (0, j)),
                      pl.BlockSpec((D_MODEL, tf), lambda i, j: (0, j)),
                      pl.BlockSpec((tf, D_MODEL), lambda i, j: (j, 0))]
        out_shape += [jax.ShapeDtypeStruct((D_MODEL, D_FF), BF16),
                      jax.ShapeDtypeStruct((D_MODEL, D_FF), BF16),
                      jax.ShapeDtypeStruct((D_FF, D_MODEL), BF16)]
    if side is not None:
        q, kn, vn, cache_kt, cache_vt, cw_in, cw_out = side
        nb, _, _, wb_len = cache_kt.shape
        assert all(wb_len % d == 0 and w <= wb_len for w, d in DIL_PATTERNS)
        hs = SATT_HEADS_PER_STEP
        groups = H_ATT // hs
        n_blocks = nb * groups
        assert ni * nf >= n_blocks and D_MODEL % ni == 0 and (2 * D_FF) % nf == 0
        sblk = lambda i, j: jnp.minimum(i * nf + j, n_blocks - 1)
        seq_row = pl.BlockSpec((1, H_ATT, HD_ATT), lambda i, j: (sblk(i, j) // groups, 0, 0))
        cache = pl.BlockSpec((1, hs, HD_ATT, wb_len),
                             lambda i, j: (sblk(i, j) // groups, sblk(i, j) % groups, 0, 0))
        cwi = pl.BlockSpec((D_MODEL // ni, 2 * D_FF // nf), lambda i, j: (i, j))
        cwo = pl.BlockSpec((D_FF // nf, D_MODEL // ni), lambda i, j: (j, i))
        in_specs += [seq_row, seq_row, seq_row, cache, cache, cwi, cwo]
        out_specs += [seq_row, cwi, cwo]
        out_shape += [jax.ShapeDtypeStruct((nb, H_ATT, HD_ATT), F32),
                      jax.ShapeDtypeStruct(cw_in.shape, BF16),
                      jax.ShapeDtypeStruct(cw_out.shape, BF16)]
        scratch.append(pltpu.VMEM((hs, wb_len), F32))
        operands += (q, kn, vn, cache_kt, cache_vt, cw_in, cw_out)
        body = functools.partial(_ffn_side_kernel, final_norm=final_norm, n_side_blocks=n_blocks)
        sem = ("arbitrary", "arbitrary")
    outs = pl.pallas_call(
        body,
        grid=(ni, nf),
        in_specs=in_specs,
        out_specs=out_specs,
        out_shape=out_shape,
        scratch_shapes=scratch,
        compiler_params=_params(*sem),
        name="ffn",
    )(*operands)
    return outs if (emit_bf16 or side is not None) else outs[0]


def _proj_kernel(x_ref, g_ref, w_ref, wlr_ref, wg2_ref, bg2_ref, p_ref, la_ref, *rest, emit_bf16):
    h_ref = rest[-1]
    j = pl.program_id(1)

    @pl.when(j == 0)
    def _():
        h = (_rms(x_ref[...]) * g_ref[...]).astype(BF16)
        h_ref[...] = h
        lr = _dot_nt(h, wlr_ref[...])
        z = _dot(lr.astype(BF16), wg2_ref[...]) + bg2_ref[...]
        log_sig = jnp.minimum(z, 0.0) - jnp.log1p(jnp.exp(-jnp.abs(z)))
        la_ref[...] = log_sig * (1.0 / GATE_TAU)

    w = w_ref[...]
    if emit_bf16:
        w = w.astype(BF16)
        rest[0][...] = w
    p_ref[...] = _dot_nt(h_ref[...], w)


def _proj(x, g, w_t, w_lr_t, w_g2, b_g2, *, tm, tn, emit_bf16=False):
    m = x.shape[0]
    out_specs = [
        pl.BlockSpec((tm, tn), lambda i, j: (i, j)),
        pl.BlockSpec((tm, D_GLA_K), lambda i, j: (i, 0)),
    ]
    out_shape = [
        jax.ShapeDtypeStruct((m, D_PROJ_MAIN), F32),
        jax.ShapeDtypeStruct((m, D_GLA_K), F32),
    ]
    if emit_bf16:
        assert m == tm
        out_specs.append(pl.BlockSpec((tn, D_MODEL), lambda i, j: (j, 0)))
        out_shape.append(jax.ShapeDtypeStruct((D_PROJ_MAIN, D_MODEL), BF16))
    return pl.pallas_call(
        functools.partial(_proj_kernel, emit_bf16=emit_bf16),
        grid=(m // tm, D_PROJ_MAIN // tn),
        in_specs=[
            pl.BlockSpec((tm, D_MODEL), lambda i, j: (i, 0)),
            pl.BlockSpec((1, D_MODEL), lambda i, j: (0, 0)),
            pl.BlockSpec((tn, D_MODEL), lambda i, j: (j, 0)),
            pl.BlockSpec((GATE_RANK, D_MODEL), lambda i, j: (0, 0)),
            pl.BlockSpec((GATE_RANK, D_GLA_K), lambda i, j: (0, 0)),
            pl.BlockSpec((1, D_GLA_K), lambda i, j: (0, 0)),
        ],
        out_specs=out_specs,
        out_shape=out_shape,
        scratch_shapes=[pltpu.VMEM((tm, D_MODEL), BF16)],
        compiler_params=_params("parallel", "arbitrary"),
        name="proj",
    )(x, g, w_t, w_lr_t, w_g2, b_g2)


def _attn_kernel(q_ref, k_ref, v_ref, o_ref, kt_ref, vt_ref,
                 qp_ref, kp_ref, vp_ref, m_ref, l_ref, acc_ref, s_ref, ms_ref):
    blk = ATT_BLOCK
    for src_ref, dst_ref in ((k_ref, kt_ref), (v_ref, vt_ref)):
        t = src_ref[...].T
        dst_ref[0, 0] = t[:HD_ATT]
        dst_ref[0, 1] = t[HD_ATT:]

    scale = HD_ATT ** -0.5
    seq = q_ref.shape[0]
    ns = DIL_PATTERNS[-1][1]
    lane_lo = lax.broadcasted_iota(jnp.int32, (1, 2 * HD_ATT), 1) < HD_ATT
    neg = jnp.float32(-jnp.inf)

    assert ns == 16
    for src_ref, tmp_ref, dst_ref in ((q_ref, m_ref, qp_ref), (k_ref, l_ref, kp_ref), (v_ref, acc_ref, vp_ref)):
        for r4 in range(4):
            x = src_ref[pl.ds(r4, seq // 4, stride=4), :]
            tmp_ref[pl.ds(r4 * (seq // 4), seq // 4), :] = x * scale if src_ref is q_ref else x
        for r4 in range(4):
            for a in range(4):
                dst_ref[pl.ds((4 * a + r4) * blk, blk), :] = tmp_ref[pl.ds(r4 * (seq // 4) + a, blk, stride=4), :]

    def run_pattern(d, first_pattern):
        na = ns // d
        plen = blk // na
        per_stream = seq // (d * blk)

        def offset(idx):
            return na * (idx & (plen - 1)) + idx // plen

        qpos = offset(lax.broadcasted_iota(jnp.int32, (blk, blk), 0))
        kpos = offset(lax.broadcasted_iota(jnp.int32, (blk, blk), 1))
        bias_cur = jnp.where(kpos <= qpos, 0.0, neg)
        if per_stream > 1:
            bias_prev = jnp.where(kpos >= qpos, 0.0, neg)
            bias_band = jnp.concatenate([bias_prev, bias_cur], axis=1)
            bias_first = jnp.concatenate([jnp.full((blk, blk), neg, F32), bias_cur], axis=1)

        def pieces(rd, n):
            return [pl.ds((a * d + rd) * blk + plen * n, plen) for a in range(na)]

        def gather(ref, ps):
            return jnp.concatenate([ref[p, :] for p in ps], axis=0)

        nk = 2 * blk if per_stream > 1 else blk
        heads = (lane_lo, jnp.logical_not(lane_lo))

        def keys(ref, rd, n):
            cur = gather(ref, pieces(rd, n))
            if per_stream == 1:
                return cur.astype(BF16)
            prev = gather(ref, pieces(rd, max(n - 1, 0)))
            return jnp.concatenate([prev, cur], axis=0).astype(BF16)

        def group(g):
            blocks = [((g * ATT_GROUP + b) % d, (g * ATT_GROUP + b) // d) for b in range(ATT_GROUP)]
            for b, (rd, n) in enumerate(blocks):
                q = gather(qp_ref, pieces(rd, n))
                kk = keys(kp_ref, rd, n)
                bias = bias_cur if per_stream == 1 else (bias_band if n > 0 else bias_first)
                for h, sel in enumerate(heads):
                    qh = jnp.where(sel, q, 0.0).astype(BF16)
                    s_ref[b, h, :, :nk] = _dot_nt(qh, kk) + bias
            for b in range(ATT_GROUP):
                for h in range(2):
                    m = jnp.max(s_ref[b, h, :, :nk], axis=-1, keepdims=True)
                    ms_ref[b, h] = jnp.broadcast_to(m, (blk, blk))
            for b, (rd, n) in enumerate(blocks):
                vv = jnp.concatenate([keys(vp_ref, rd, n), jnp.ones((nk, blk), BF16)], axis=1)
                res = []
                for h in range(2):
                    mh = ms_ref[b, h]
                    mh = jnp.concatenate([mh, mh], axis=1) if nk == 2 * blk else mh
                    res.append(_dot(jnp.exp(s_ref[b, h, :, :nk] - mh).astype(BF16), vv))
                u = jnp.where(lane_lo, res[0][:, :blk], res[1][:, :blk])
                l = jnp.where(lane_lo, res[0][:, blk:], res[1][:, blk:])
                m = jnp.where(lane_lo, ms_ref[b, 0], ms_ref[b, 1])
                for a, p in enumerate(pieces(rd, n)):
                    sl = slice(a * plen, (a + 1) * plen)
                    if first_pattern:
                        m_ref[p, :] = m[sl]
                        l_ref[p, :] = l[sl]
                        acc_ref[p, :] = u[sl]
                    else:
                        m_old = m_ref[p, :]
                        m_new = jnp.maximum(m_old, m[sl])
                        a_old = jnp.exp(m_old - m_new)
                        a_blk = jnp.exp(m[sl] - m_new)
                        m_ref[p, :] = m_new
                        l_ref[p, :] = a_old * l_ref[p, :] + a_blk * l[sl]
                        acc_ref[p, :] = a_old * acc_ref[p, :] + a_blk * u[sl]

        for g in range(d * per_stream // ATT_GROUP):
            group(g)

    for idx, (_, d) in enumerate(DIL_PATTERNS):
        run_pattern(d, idx == 0)

    for r4 in range(4):
        for a in range(4):
            rows = pl.ds((4 * a + r4) * blk, blk)
            qp_ref[pl.ds(r4 * (seq // 4) + a, blk, stride=4), :] = acc_ref[rows, :] / l_ref[rows, :]
    for r4 in range(4):
        o_ref[pl.ds(r4, seq // 4, stride=4), :] = qp_ref[pl.ds(r4 * (seq // 4), seq // 4), :]


def _attn_prompt(p, batch, seq):
    ns = DIL_PATTERNS[-1][1]
    assert seq == ns * ATT_BLOCK
    assert all(w // d == ATT_BLOCK and ns % d == 0 and ATT_BLOCK * d // ns >= 8 for w, d in DIL_PATTERNS)
    lanes = 2 * HD_ATT
    n_pairs = H_ATT // 2
    return pl.pallas_call(
        _attn_kernel,
        grid=(batch, n_pairs),
        in_specs=[
            pl.BlockSpec((seq, lanes), lambda b, h: (b, COL_QA // lanes + h)),
            pl.BlockSpec((seq, lanes), lambda b, h: (b, COL_KA // lanes + h)),
            pl.BlockSpec((seq, lanes), lambda b, h: (b, COL_VA // lanes + h)),
        ],
        out_specs=[
            pl.BlockSpec((seq, lanes), lambda b, h: (b, h)),
            pl.BlockSpec((1, 2, HD_ATT, seq), lambda b, h: (b, h, 0, 0)),
            pl.BlockSpec((1, 2, HD_ATT, seq), lambda b, h: (b, h, 0, 0)),
        ],
        out_shape=[
            jax.ShapeDtypeStruct((batch * seq, D_ATT), F32),
            jax.ShapeDtypeStruct((batch, H_ATT, HD_ATT, seq), F32),
            jax.ShapeDtypeStruct((batch, H_ATT, HD_ATT, seq), F32),
        ],
        scratch_shapes=[pltpu.VMEM((seq, lanes), F32)] * 6 + [
            pltpu.VMEM((ATT_GROUP, 2, ATT_BLOCK, 2 * ATT_BLOCK), F32),
            pltpu.VMEM((ATT_GROUP, 2, ATT_BLOCK, lanes), F32),
        ],
        compiler_params=_params("parallel", "parallel"),
        name="attn_prompt",
    )(p, p, p)


_GLA_LEVELS = (32, 16, 8, 4, 2, 1)


def _gla_exponent_matrix():
    c = GLA_CHUNK
    t = np.arange(c)[:, None]
    u = np.arange(c)[None, :]
    mats = [(u <= t), (u > t)]
    for h in _GLA_LEVELS:
        mid = (t // (2 * h)) * (2 * h) + h - 1
        upper = (t % (2 * h)) >= h
        mats.append(np.where(upper, (u > mid) & (u <= t), (u > t) & (u <= mid)))
    tmat = np.concatenate(mats, axis=0).astype(np.float32)
    return np.concatenate([tmat, tmat], axis=1)


def _gla_kernel(q_ref, k_ref, v_ref, g_ref, t_ref, o_ref, s_ref, st_ref):
    c = GLA_CHUNK
    scale = DK_GLA ** -0.5
    ti = lax.broadcasted_iota(jnp.int32, (c, c), 0)
    si = lax.broadcasted_iota(jnp.int32, (c, c), 1)
    txs = ti ^ si
    below = ti > si
    level_masks = [below & (txs >= h) & (txs < 2 * h) for h in _GLA_LEVELS]
    diag = ti == si

    st_ref[...] = jnp.zeros_like(st_ref)

    def chunk(ci, carry):
        r = pl.ds(pl.multiple_of(ci * c, c), c)
        g = g_ref[r, :]
        g_hi = g.astype(BF16)
        g_lo = (g - g_hi.astype(F32)).astype(BF16)
        f = jnp.exp(_dot(t_ref[...], jnp.concatenate([g_hi, g_lo], axis=0)))
        q = q_ref[r, :] * scale
        k = k_ref[r, :]
        v = v_ref[r, :].astype(BF16)
        a = jnp.where(diag, _dot_nt(q.astype(BF16), k.astype(BF16)), 0.0)
        for lvl in range(len(_GLA_LEVELS)):
            fl = f[2 * c + lvl * c: 3 * c + lvl * c]
            a = a + jnp.where(level_masks[lvl],
                              _dot_nt((q * fl).astype(BF16), (k * fl).astype(BF16)), 0.0)
        st = st_ref[...]
        qe = (q * f[0:c]).astype(BF16)
        o_ref[r, :] = _dot_nt(qe, st.astype(BF16)) + _dot(a.astype(BF16), v)
        kd = (k * f[c:2 * c]).astype(BF16)
        st_ref[...] = st * f[c - 1:c] + _dot_tn(v, kd)
        return carry

    lax.fori_loop(0, q_ref.shape[0] // c, chunk, 0, unroll=8)
    s_ref[0, 0] = st_ref[...].T


def _gla_prompt(p, la, tmat, batch, seq):
    return pl.pallas_call(
        _gla_kernel,
        grid=(batch, H_GLA),
        in_specs=[
            pl.BlockSpec((seq, DK_GLA), lambda b, h: (b, COL_QG // DK_GLA + h)),
            pl.BlockSpec((seq, DK_GLA), lambda b, h: (b, COL_KG // DK_GLA + h)),
            pl.BlockSpec((seq, DV_GLA), lambda b, h: (b, COL_VG // DV_GLA + h)),
            pl.BlockSpec((seq, DK_GLA), lambda b, h: (b, h)),
            pl.BlockSpec(tmat.shape, lambda b, h: (0, 0)),
        ],
        out_specs=[
            pl.BlockSpec((seq, DV_GLA), lambda b, h: (b, h)),
            pl.BlockSpec((1, 1, DK_GLA, DV_GLA), lambda b, h: (b, h, 0, 0)),
        ],
        out_shape=[
            jax.ShapeDtypeStruct((batch * seq, D_GLA_V), F32),
            jax.ShapeDtypeStruct((batch, H_GLA, DK_GLA, DV_GLA), F32),
        ],
        scratch_shapes=[pltpu.VMEM((DV_GLA, DK_GLA), F32)],
        compiler_params=_params("parallel", "parallel"),
        name="gla_prompt",
    )(p, p, p, la, tmat)


def _merge_kernel(x_ref, oa_ref, og_ref, rg_ref, ga_ref, gg_ref, w_ref, o_ref):
    a = (_rms(oa_ref[...]) * ga_ref[...]).astype(BF16)
    parts = []
    for h in range(H_GLA):
        cols = slice(h * DV_GLA, (h + 1) * DV_GLA)
        r = rg_ref[:, cols]
        parts.append((_rms(og_ref[:, cols]) * gg_ref[...] * (r * jax.nn.sigmoid(r))).astype(BF16))
    gg = jnp.concatenate(parts, axis=-1)
    o_ref[...] = x_ref[...] + _dot(a, w_ref[0:D_ATT, :]) + _dot(gg, w_ref[D_ATT:, :])


def _merge(x, o_att, o_gla, p, g_att, g_gla, w_out, *, tm):
    m = x.shape[0]
    return pl.pallas_call(
        _merge_kernel,
        grid=(m // tm,),
        in_specs=[
            pl.BlockSpec((tm, D_MODEL), lambda i: (i, 0)),
            pl.BlockSpec((tm, D_ATT), lambda i: (i, 0)),
            pl.BlockSpec((tm, D_GLA_V), lambda i: (i, 0)),
            pl.BlockSpec((tm, D_GLA_V), lambda i: (i, COL_RG // D_GLA_V)),
            pl.BlockSpec((1, D_ATT), lambda i: (0, 0)),
            pl.BlockSpec((1, DV_GLA), lambda i: (0, 0)),
            pl.BlockSpec((D_MODEL, D_MODEL), lambda i: (0, 0)),
        ],
        out_specs=pl.BlockSpec((tm, D_MODEL), lambda i: (i, 0)),
        out_shape=jax.ShapeDtypeStruct((m, D_MODEL), F32),
        compiler_params=_params("parallel"),
        name="merge",
    )(x, o_att, o_gla, p, g_att, g_gla, w_out)


SATT_HEADS_PER_STEP = 4


def _sattn_heads(q, kn, vn, k_ref, v_ref, s_ref):
    scale = HD_ATT ** -0.5
    wb = k_ref.shape[-1]
    t = lax.broadcasted_iota(jnp.int32, (1, wb), 1)
    cnt = jnp.zeros((1, wb), F32)
    for w, d in DIL_PATTERNS:
        cnt = cnt + jnp.where((t >= wb - w) & ((t & (d - 1)) == 0), 1.0, 0.0)
    bias = jnp.where(cnt > 0.0, 0.0, jnp.float32(-jnp.inf))
    n_pat = float(len(DIL_PATTERNS))
    n_heads = k_ref.shape[1]
    q = q * scale
    q16 = q.astype(BF16)
    s_new = jnp.sum(q * kn, axis=-1, keepdims=True)
    for h in range(n_heads):
        s_ref[h:h + 1, :] = _dot(q16, k_ref[0, h].astype(BF16))[h:h + 1, :]
    s = s_ref[...] + bias
    m = jnp.maximum(jnp.max(s, axis=-1, keepdims=True), s_new)
    e = cnt * jnp.exp(s - m)
    e_new = n_pat * jnp.exp(s_new - m)
    den = jnp.sum(e, axis=-1, keepdims=True) + e_new
    e16 = e.astype(BF16)
    head = lax.broadcasted_iota(jnp.int32, (n_heads, HD_ATT), 0)
    num = e_new * vn
    for h in range(n_heads):
        num = num + jnp.where(head == h, _dot_nt(e16, v_ref[0, h].astype(BF16)), 0.0)
    return num / den


SGLA_SEQS_PER_STEP = 4
_SPLIT = 3


def _bf16_pieces(x):
    pieces = []
    for _ in range(_SPLIT):
        p = x.astype(BF16)
        pieces.append(p)
        x = x - p.astype(F32)
    return pieces


def _sgla_kernel(q_ref, k_ref, g_ref, v_ref, s_ref, so_ref, o_ref):
    scale = DK_GLA ** -0.5
    n_vec = 3
    sel_rows = 16
    r = lax.broadcasted_iota(jnp.int32, (sel_rows, n_vec * DV_GLA), 0)
    c = lax.broadcasted_iota(jnp.int32, (sel_rows, n_vec * DV_GLA), 1)
    selector = jnp.where((r // _SPLIT == c // DV_GLA) & (r < n_vec * _SPLIT), 1.0, 0.0).astype(BF16)
    pad = jnp.zeros((sel_rows - n_vec * _SPLIT, DK_GLA), BF16)
    for s in range(q_ref.shape[0]):
        for h in range(H_GLA):
            hrow = slice(h, h + 1)
            vecs = (jnp.exp(g_ref[s, hrow, :]), k_ref[s, hrow, :], q_ref[s, hrow, :] * scale)
            lhs = jnp.concatenate([p for x in vecs for p in _bf16_pieces(x)] + [pad], axis=0)
            cols = _dot_tn(lhs, selector)
            decay, kcol, qcol = (cols[:, i * DV_GLA:(i + 1) * DV_GLA] for i in range(n_vec))
            s_new = decay * s_ref[s, h] + kcol * v_ref[s, hrow, :]
            so_ref[s, h] = s_new
            o_ref[s, hrow, :] = jnp.sum(qcol * s_new, axis=0, keepdims=True)


def _gla_sample(q, k, g, v, state):
    nb = state.shape[0]
    bs = SGLA_SEQS_PER_STEP
    krow = pl.BlockSpec((bs, H_GLA, DK_GLA), lambda b: (b, 0, 0))
    vrow = pl.BlockSpec((bs, H_GLA, DV_GLA), lambda b: (b, 0, 0))
    st = pl.BlockSpec((bs, H_GLA, DK_GLA, DV_GLA), lambda b: (b, 0, 0, 0))
    return pl.pallas_call(
        _sgla_kernel,
        grid=(nb // bs,),
        in_specs=[krow, krow, krow, vrow, st],
        out_specs=[st, vrow],
        out_shape=[jax.ShapeDtypeStruct((nb, H_GLA, DK_GLA, DV_GLA), F32),
                   jax.ShapeDtypeStruct((nb, H_GLA, DV_GLA), F32)],
        compiler_params=_params("parallel"),
        name="gla_sample",
    )(q, k, g, v, state)


def kernel(x_prompt, x_sample, cache_att_k, cache_att_v, state_gla, g_ffn1, w_ffn1_in, w_ffn1_out, g_mix, w_in, w_gate2, b_gate2, g_att_out, g_gla_out, w_out, g_ffn2, w_ffn2_in, w_ffn2_out, g_final):
    depth = w_in.shape[0]
    assert depth == 1
    batch, seq, _ = x_prompt.shape
    nb, dec_seq, _ = x_sample.shape
    assert dec_seq == 1
    xp = x_prompt.reshape(batch * seq, D_MODEL)
    xs = x_sample.reshape(nb, D_MODEL)
    row = lambda a: a.reshape(1, -1)
    l = 0

    w_in_t = w_in[l].T
    w_lr = w_in_t[D_PROJ_MAIN:].astype(BF16)
    w_g2 = w_gate2[l].astype(BF16)
    b_g2 = row(b_gate2[l])
    w_o = w_out[l].astype(BF16)
    gf = row(g_final)
    tmat = jnp.asarray(_gla_exponent_matrix(), dtype=BF16)

    xs, w1a, w1b, w1o = _ffn(xs, row(g_ffn1[l]), w_ffn1_in[l], w_ffn1_in[l], w_ffn1_out[l], gf,
                             tm=nb, tf=512, final_norm=False, emit_bf16=True)
    ps, las, w_main = _proj(xs, row(g_mix[l]), w_in_t, w_lr, w_g2, b_g2, tm=nb, tn=512, emit_bf16=True)
    heads = lambda c0: ps[:, c0:c0 + D_ATT].reshape(nb, H_ATT, HD_ATT)
    k_new, v_new = heads(COL_KA), heads(COL_VA)
    xp, o_att_s, w2i, w2o = _ffn(
        xp, row(g_ffn1[l]), w1a, w1b, w1o, gf, tm=1024, tf=256, final_norm=False,
        side=(heads(COL_QA), k_new, v_new,
              jnp.transpose(cache_att_k[l], (0, 2, 3, 1)), jnp.transpose(cache_att_v[l], (0, 2, 3, 1)),
              w_ffn2_in[l], w_ffn2_out[l]))
    krows = lambda a: a.reshape(nb, H_GLA, DK_GLA)
    s_new, o_gla_s = _gla_sample(
        krows(ps[:, COL_QG:COL_QG + D_GLA_K]), krows(ps[:, COL_KG:COL_KG + D_GLA_K]), krows(las),
        ps[:, COL_VG:COL_VG + D_GLA_V].reshape(nb, H_GLA, DV_GLA), state_gla[l])
    xs = _merge(xs, o_att_s.reshape(nb, D_ATT), o_gla_s.reshape(nb, D_GLA_V), ps,
                row(g_att_out[l]), row(g_gla_out[l]), w_o, tm=nb)
    ys = _ffn(xs, row(g_ffn2[l]), w2i, w2i, w2o, gf, tm=nb, tf=512, final_norm=True)
    nk_s = k_new.reshape(1, nb, 1, H_ATT, HD_ATT)
    nv_s = v_new.reshape(1, nb, 1, H_ATT, HD_ATT)

    pp, lap = _proj(xp, row(g_mix[l]), w_main, w_lr, w_g2, b_g2, tm=1024, tn=1024)
    o_att, kt_p, vt_p = _attn_prompt(pp, batch, seq)
    o_gla, s_fin = _gla_prompt(pp, lap, tmat, batch, seq)
    xp = _merge(xp, o_att, o_gla, pp, row(g_att_out[l]), row(g_gla_out[l]), w_o, tm=512)
    yp = _ffn(xp, row(g_ffn2[l]), w2i, w2i, w2o, gf, tm=1024, tf=256, final_norm=True)
    nk_p = jnp.transpose(kt_p, (0, 3, 1, 2))[None]
    nv_p = jnp.transpose(vt_p, (0, 3, 1, 2))[None]

    return (yp.reshape(batch, seq, D_MODEL), ys.reshape(nb, 1, D_MODEL), nk_p, nv_p,
            s_fin[None], nk_s, nv_s, s_new[None])
```

```python
import functools

import numpy as np
import jax
import jax.numpy as jnp
from jax import lax
from jax.experimental import pallas as pl
from jax.experimental.pallas import tpu as pltpu

F32 = jnp.float32
BF16 = jnp.bfloat16

D_MODEL = 2048
D_FF = 5632
D_ATT = 1024
HD_ATT = 64
H_ATT = 16
H_GLA = 4
DK_GLA = 128
DV_GLA = 256
D_GLA_K = H_GLA * DK_GLA
D_GLA_V = H_GLA * DV_GLA
GATE_RANK = 16
GATE_TAU = 16.0
NORM_EPS = 1e-6
DIL_PATTERNS = ((128, 1), (512, 4), (2048, 16))
ATT_BLOCK = 128
ATT_GROUP = 16
GLA_CHUNK = 64
FFN_TILE = 256
D_PROJ_MAIN = 3 * D_ATT + 2 * D_GLA_K + 2 * D_GLA_V

COL_QA, COL_KA, COL_VA = 0, D_ATT, 2 * D_ATT
COL_QG = 3 * D_ATT
COL_KG = COL_QG + D_GLA_K
COL_VG = COL_KG + D_GLA_K
COL_RG = COL_VG + D_GLA_V

VMEM_LIMIT_BYTES = 56 * 1024 * 1024


def _rms(x):
    return x * lax.rsqrt(jnp.mean(x * x, axis=-1, keepdims=True) + NORM_EPS)


def _dot(a, b):
    return jnp.dot(a, b, preferred_element_type=F32)


def _dot_nt(a, b):
    return lax.dot_general(a, b, (((1,), (1,)), ((), ())), preferred_element_type=F32)


def _dot_tn(a, b):
    return lax.dot_general(a, b, (((0,), (0,)), ((), ())), preferred_element_type=F32)


def _params(*sem):
    return pltpu.CompilerParams(dimension_semantics=sem, vmem_limit_bytes=VMEM_LIMIT_BYTES)


def _store_col_tiles(dst_ref, w):
    tile = dst_ref.shape[-1]
    for t in range(dst_ref.shape[0]):
        dst_ref[t] = w[:, t * tile:(t + 1) * tile]


def _ffn_step(x_ref, g_ref, wa, wb, wo, gf_ref, o_ref, h_ref, final_norm, side_jobs=None):
    j = pl.program_id(1)

    @pl.when(j == 0)
    def _():
        h_ref[...] = (_rms(x_ref[...]) * g_ref[...]).astype(BF16)
        o_ref[...] = jnp.zeros_like(o_ref)

    h = h_ref[...]
    a = _dot(h, wa)
    b = _dot(h, wb)
    act = (a * jax.nn.sigmoid(a) * b).astype(BF16)
    o_ref[...] += _dot(act, wo)
    if side_jobs is not None:
        side_jobs()

    @pl.when(j == pl.num_programs(1) - 1)
    def _():
        y = x_ref[...] + 0.5 * o_ref[...]
        if final_norm:
            y = _rms(y) * gf_ref[...]
        o_ref[...] = y


def _ffn_kernel(x_ref, g_ref, wa_ref, wb_ref, wo_ref, gf_ref, o_ref, *rest, final_norm, emit_bf16):
    wa, wb, wo = wa_ref[...], wb_ref[...], wo_ref[...]
    if emit_bf16:
        wa, wb, wo = wa.astype(BF16), wb.astype(BF16), wo.astype(BF16)
        _store_col_tiles(rest[0], wa)
        _store_col_tiles(rest[1], wb)
        rest[2][...] = wo
    _ffn_step(x_ref, g_ref, wa, wb, wo, gf_ref, o_ref, rest[-1], final_norm)


def _ffn_side_kernel(x_ref, g_ref, wa_ref, wb_ref, wo_ref, gf_ref,
                     q_ref, kn_ref, vn_ref, k_ref, v_ref, cwi_ref, cwo_ref,
                     o_ref, so_ref, cwi16_ref, cwo16_ref, h_ref, s_ref, *, final_norm, n_side_blocks):
    def side_jobs():
        hs = k_ref.shape[1]
        groups = H_ATT // hs
        step = pl.program_id(0) * pl.num_programs(1) + pl.program_id(1)
        blk = jnp.minimum(step, n_side_blocks - 1)
        heads = pl.ds(pl.multiple_of((blk % groups) * hs, hs), hs)
        so_ref[0, heads, :] = _sattn_heads(q_ref[0, heads, :], kn_ref[0, heads, :], vn_ref[0, heads, :],
                                           k_ref, v_ref, s_ref)
        _store_col_tiles(cwi16_ref, cwi_ref[...].astype(BF16))
        cwo16_ref[...] = cwo_ref[...].astype(BF16)

    _ffn_step(x_ref, g_ref, wa_ref[...], wb_ref[...], wo_ref[...], gf_ref, o_ref, h_ref, final_norm, side_jobs)


def _ffn(x, g, wa, wb, wo, g_final, *, tm, tf, final_norm, emit_bf16=False, side=None, w_tile=None):
    m = x.shape[0]
    nf = D_FF // tf
    ni = m // tm

    def w_in_spec(w, off):
        if w.ndim == 3:
            assert w.shape[1:] == (D_MODEL, tf)
            return pl.BlockSpec((None, D_MODEL, tf), lambda i, j: (j + off, 0, 0))
        return pl.BlockSpec((D_MODEL, tf), lambda i, j: (0, j + off))

    both_halves = wb.shape[0] == 2 * nf if wb.ndim == 3 else wb.shape[1] == 2 * D_FF
    in_specs = [
        pl.BlockSpec((tm, D_MODEL), lambda i, j: (i, 0)),
        pl.BlockSpec((1, D_MODEL), lambda i, j: (0, 0)),
        w_in_spec(wa, 0),
        w_in_spec(wb, nf if both_halves else 0),
        pl.BlockSpec((tf, D_MODEL), lambda i, j: (j, 0)),
        pl.BlockSpec((1, D_MODEL), lambda i, j: (0, 0)),
    ]
    out_specs = [pl.BlockSpec((tm, D_MODEL), lambda i, j: (i, 0))]
    out_shape = [jax.ShapeDtypeStruct((m, D_MODEL), F32)]
    scratch = [pltpu.VMEM((tm, D_MODEL), BF16)]
    operands = (x, g, wa, wb, wo, g_final)
    if side is None:
        body = functools.partial(_ffn_kernel, final_norm=final_norm, emit_bf16=emit_bf16)
        sem = ("parallel", "arbitrary")
    if emit_bf16:
        assert m == tm and side is None
        half = pl.BlockSpec((tf // w_tile, D_MODEL, w_tile), lambda i, j: (j, 0, 0))
        out_specs += [half, half, pl.BlockSpec((tf, D_MODEL), lambda i, j: (j, 0))]
        out_shape += [jax.ShapeDtypeStruct((D_FF // w_tile, D_MODEL, w_tile), BF16),
                      jax.ShapeDtypeStruct((D_FF // w_tile, D_MODEL, w_tile), BF16),
                      jax.ShapeDtypeStruct((D_FF, D_MODEL), BF16)]
    if side is not None:
        q, kn, vn, cache_kt, cache_vt, cw_in, cw_out = side
        nb, _, _, wb_len = cache_kt.shape
        assert all(wb_len % d == 0 and w <= wb_len for w, d in DIL_PATTERNS)
        hs = SATT_HEADS_PER_STEP
        groups = H_ATT // hs
        n_blocks = nb * groups
        assert ni * nf >= n_blocks and D_MODEL % ni == 0 and (2 * D_FF) % nf == 0
        sblk = lambda i, j: jnp.minimum(i * nf + j, n_blocks - 1)
        seq_row = pl.BlockSpec((1, H_ATT, HD_ATT), lambda i, j: (sblk(i, j) // groups, 0, 0))
        cache = pl.BlockSpec((1, hs, HD_ATT, wb_len),
                             lambda i, j: (sblk(i, j) // groups, sblk(i, j) % groups, 0, 0))
        slab = 2 * D_FF // nf
        assert slab % w_tile == 0
        cwi = pl.BlockSpec((D_MODEL // ni, slab), lambda i, j: (i, j))
        cwi16 = pl.BlockSpec((slab // w_tile, D_MODEL // ni, w_tile), lambda i, j: (j, i, 0))
        cwo = pl.BlockSpec((D_FF // nf, D_MODEL // ni), lambda i, j: (j, i))
        in_specs += [seq_row, seq_row, seq_row, cache, cache, cwi, cwo]
        out_specs += [seq_row, cwi16, cwo]
        out_shape += [jax.ShapeDtypeStruct((nb, H_ATT, HD_ATT), F32),
                      jax.ShapeDtypeStruct((2 * D_FF // w_tile, D_MODEL, w_tile), BF16),
                      jax.ShapeDtypeStruct(cw_out.shape, BF16)]
        scratch.append(pltpu.VMEM((hs, wb_len), F32))
        operands += (q, kn, vn, cache_kt, cache_vt, cw_in, cw_out)
        body = functools.partial(_ffn_side_kernel, final_norm=final_norm, n_side_blocks=n_blocks)
        sem = ("arbitrary", "arbitrary")
    outs = pl.pallas_call(
        body,
        grid=(ni, nf),
        in_specs=in_specs,
        out_specs=out_specs,
        out_shape=out_shape,
        scratch_shapes=scratch,
        compiler_params=_params(*sem),
        name="ffn",
    )(*operands)
    return outs if (emit_bf16 or side is not None) else outs[0]


def _proj_kernel(x_ref, g_ref, w_ref, wlr_ref, wg2_ref, bg2_ref, p_ref, la_ref, *rest, emit_bf16):
    h_ref = rest[-1]
    j = pl.program_id(1)

    @pl.when(j == 0)
    def _():
        h = (_rms(x_ref[...]) * g_ref[...]).astype(BF16)
        h_ref[...] = h
        lr = _dot_nt(h, wlr_ref[...])
        z = _dot(lr.astype(BF16), wg2_ref[...]) + bg2_ref[...]
        log_sig = jnp.minimum(z, 0.0) - jnp.log1p(jnp.exp(-jnp.abs(z)))
        la_ref[...] = log_sig * (1.0 / GATE_TAU)

    w = w_ref[...]
    if emit_bf16:
        w = w.astype(BF16)
        rest[0][...] = w
    p_ref[...] = _dot_nt(h_ref[...], w)


def _proj(x, g, w_t, w_lr_t, w_g2, b_g2, *, tm, tn, emit_bf16=False):
    m = x.shape[0]
    out_specs = [
        pl.BlockSpec((tm, tn), lambda i, j: (i, j)),
        pl.BlockSpec((tm, D_GLA_K), lambda i, j: (i, 0)),
    ]
    out_shape = [
        jax.ShapeDtypeStruct((m, D_PROJ_MAIN), F32),
        jax.ShapeDtypeStruct((m, D_GLA_K), F32),
    ]
    if emit_bf16:
        assert m == tm
        out_specs.append(pl.BlockSpec((tn, D_MODEL), lambda i, j: (j, 0)))
        out_shape.append(jax.ShapeDtypeStruct((D_PROJ_MAIN, D_MODEL), BF16))
    return pl.pallas_call(
        functools.partial(_proj_kernel, emit_bf16=emit_bf16),
        grid=(m // tm, D_PROJ_MAIN // tn),
        in_specs=[
            pl.BlockSpec((tm, D_MODEL), lambda i, j: (i, 0)),
            pl.BlockSpec((1, D_MODEL), lambda i, j: (0, 0)),
            pl.BlockSpec((tn, D_MODEL), lambda i, j: (j, 0)),
            pl.BlockSpec((GATE_RANK, D_MODEL), lambda i, j: (0, 0)),
            pl.BlockSpec((GATE_RANK, D_GLA_K), lambda i, j: (0, 0)),
            pl.BlockSpec((1, D_GLA_K), lambda i, j: (0, 0)),
        ],
        out_specs=out_specs,
        out_shape=out_shape,
        scratch_shapes=[pltpu.VMEM((tm, D_MODEL), BF16)],
        compiler_params=_params("parallel", "arbitrary"),
        name="proj",
    )(x, g, w_t, w_lr_t, w_g2, b_g2)


def _attn_kernel(q_ref, k_ref, v_ref, o_ref, kt_ref, vt_ref,
                 qp_ref, kp_ref, vp_ref, m_ref, l_ref, acc_ref, s_ref, ms_ref):
    blk = ATT_BLOCK
    for src_ref, dst_ref in ((k_ref, kt_ref), (v_ref, vt_ref)):
        t = src_ref[...].T
        dst_ref[0, 0] = t[:HD_ATT]
        dst_ref[0, 1] = t[HD_ATT:]

    scale = HD_ATT ** -0.5
    seq = q_ref.shape[0]
    ns = DIL_PATTERNS[-1][1]
    lane_lo = lax.broadcasted_iota(jnp.int32, (1, 2 * HD_ATT), 1) < HD_ATT
    neg = jnp.float32(-jnp.inf)

    assert ns == 16
    for src_ref, tmp_ref, dst_ref in ((q_ref, m_ref, qp_ref), (k_ref, l_ref, kp_ref), (v_ref, acc_ref, vp_ref)):
        for r4 in range(4):
            x = src_ref[pl.ds(r4, seq // 4, stride=4), :]
            tmp_ref[pl.ds(r4 * (seq // 4), seq // 4), :] = x * scale if src_ref is q_ref else x
        for r4 in range(4):
            for a in range(4):
                dst_ref[pl.ds((4 * a + r4) * blk, blk), :] = tmp_ref[pl.ds(r4 * (seq // 4) + a, blk, stride=4), :]

    def run_pattern(d, first_pattern):
        na = ns // d
        plen = blk // na
        per_stream = seq // (d * blk)

        def offset(idx):
            return na * (idx & (plen - 1)) + idx // plen

        qpos = offset(lax.broadcasted_iota(jnp.int32, (blk, blk), 0))
        kpos = offset(lax.broadcasted_iota(jnp.int32, (blk, blk), 1))
        bias_cur = jnp.where(kpos <= qpos, 0.0, neg)
        if per_stream > 1:
            bias_prev = jnp.where(kpos >= qpos, 0.0, neg)
            bias_band = jnp.concatenate([bias_prev, bias_cur], axis=1)
            bias_first = jnp.concatenate([jnp.full((blk, blk), neg, F32), bias_cur], axis=1)

        def pieces(rd, n):
            return [pl.ds((a * d + rd) * blk + plen * n, plen) for a in range(na)]

        def gather(ref, ps):
            return jnp.concatenate([ref[p, :] for p in ps], axis=0)

        nk = 2 * blk if per_stream > 1 else blk
        heads = (lane_lo, jnp.logical_not(lane_lo))

        def keys(ref, rd, n):
            cur = gather(ref, pieces(rd, n))
            if per_stream == 1:
                return cur.astype(BF16)
            prev = gather(ref, pieces(rd, max(n - 1, 0)))
            return jnp.concatenate([prev, cur], axis=0).astype(BF16)

        def group(g):
            blocks = [((g * ATT_GROUP + b) % d, (g * ATT_GROUP + b) // d) for b in range(ATT_GROUP)]
            for b, (rd, n) in enumerate(blocks):
                q = gather(qp_ref, pieces(rd, n))
                kk = keys(kp_ref, rd, n)
                bias = bias_cur if per_stream == 1 else (bias_band if n > 0 else bias_first)
                for h, sel in enumerate(heads):
                    qh = jnp.where(sel, q, 0.0).astype(BF16)
                    s_ref[b, h, :, :nk] = _dot_nt(qh, kk) + bias
            for b in range(ATT_GROUP):
                for h in range(2):
                    m = jnp.max(s_ref[b, h, :, :nk], axis=-1, keepdims=True)
                    ms_ref[b, h] = jnp.broadcast_to(m, (blk, blk))
            for b, (rd, n) in enumerate(blocks):
                vv = jnp.concatenate([keys(vp_ref, rd, n), jnp.ones((nk, blk), BF16)], axis=1)
                res = []
                for h in range(2):
                    mh = ms_ref[b, h]
                    mh = jnp.concatenate([mh, mh], axis=1) if nk == 2 * blk else mh
                    res.append(_dot(jnp.exp(s_ref[b, h, :, :nk] - mh).astype(BF16), vv))
                u = jnp.where(lane_lo, res[0][:, :blk], res[1][:, :blk])
                l = jnp.where(lane_lo, res[0][:, blk:], res[1][:, blk:])
                m = jnp.where(lane_lo, ms_ref[b, 0], ms_ref[b, 1])
                for a, p in enumerate(pieces(rd, n)):
                    sl = slice(a * plen, (a + 1) * plen)
                    if first_pattern:
                        m_ref[p, :] = m[sl]
                        l_ref[p, :] = l[sl]
                        acc_ref[p, :] = u[sl]
                    else:
                        m_old = m_ref[p, :]
                        m_new = jnp.maximum(m_old, m[sl])
                        a_old = jnp.exp(m_old - m_new)
                        a_blk = jnp.exp(m[sl] - m_new)
                        m_ref[p, :] = m_new
                        l_ref[p, :] = a_old * l_ref[p, :] + a_blk * l[sl]
                        acc_ref[p, :] = a_old * acc_ref[p, :] + a_blk * u[sl]

        for g in range(d * per_stream // ATT_GROUP):
            group(g)

    for idx, (_, d) in enumerate(DIL_PATTERNS):
        run_pattern(d, idx == 0)

    for r4 in range(4):
        for a in range(4):
            rows = pl.ds((4 * a + r4) * blk, blk)
            qp_ref[pl.ds(r4 * (seq // 4) + a, blk, stride=4), :] = acc_ref[rows, :] / l_ref[rows, :]
    for r4 in range(4):
        o_ref[pl.ds(r4, seq // 4, stride=4), :] = qp_ref[pl.ds(r4 * (seq // 4), seq // 4), :]


def _attn_prompt(p, batch, seq):
    ns = DIL_PATTERNS[-1][1]
    assert seq == ns * ATT_BLOCK
    assert all(w // d == ATT_BLOCK and ns % d == 0 and ATT_BLOCK * d // ns >= 8 for w, d in DIL_PATTERNS)
    lanes = 2 * HD_ATT
    n_pairs = H_ATT // 2
    return pl.pallas_call(
        _attn_kernel,
        grid=(batch, n_pairs),
        in_specs=[
            pl.BlockSpec((seq, lanes), lambda b, h: (b, COL_QA // lanes + h)),
            pl.BlockSpec((seq, lanes), lambda b, h: (b, COL_KA // lanes + h)),
            pl.BlockSpec((seq, lanes), lambda b, h: (b, COL_VA // lanes + h)),
        ],
        out_specs=[
            pl.BlockSpec((seq, lanes), lambda b, h: (b, h)),
            pl.BlockSpec((1, 2, HD_ATT, seq), lambda b, h: (b, h, 0, 0)),
            pl.BlockSpec((1, 2, HD_ATT, seq), lambda b, h: (b, h, 0, 0)),
        ],
        out_shape=[
            jax.ShapeDtypeStruct((batch * seq, D_ATT), F32),
            jax.ShapeDtypeStruct((batch, H_ATT, HD_ATT, seq), F32),
            jax.ShapeDtypeStruct((batch, H_ATT, HD_ATT, seq), F32),
        ],
        scratch_shapes=[pltpu.VMEM((seq, lanes), F32)] * 6 + [
            pltpu.VMEM((ATT_GROUP, 2, ATT_BLOCK, 2 * ATT_BLOCK), F32),
            pltpu.VMEM((ATT_GROUP, 2, ATT_BLOCK, lanes), F32),
        ],
        compiler_params=_params("parallel", "parallel"),
        name="attn_prompt",
    )(p, p, p)


_GLA_LEVELS = (32, 16, 8, 4, 2, 1)


def _gla_exponent_matrix():
    c = GLA_CHUNK
    t = np.arange(c)[:, None]
    u = np.arange(c)[None, :]
    mats = [(u <= t), (u > t)]
    for h in _GLA_LEVELS:
        mid = (t // (2 * h)) * (2 * h) + h - 1
        upper = (t % (2 * h)) >= h
        mats.append(np.where(upper, (u > mid) & (u <= t), (u > t) & (u <= mid)))
    tmat = np.concatenate(mats, axis=0).astype(np.float32)
    return np.concatenate([tmat, tmat], axis=1)


def _gla_kernel(q_ref, k_ref, v_ref, g_ref, t_ref, o_ref, s_ref, st_ref):
    c = GLA_CHUNK
    scale = DK_GLA ** -0.5
    ti = lax.broadcasted_iota(jnp.int32, (c, c), 0)
    si = lax.broadcasted_iota(jnp.int32, (c, c), 1)
    txs = ti ^ si
    below = ti > si
    level_masks = [below & (txs >= h) & (txs < 2 * h) for h in _GLA_LEVELS]
    diag = ti == si

    st_ref[...] = jnp.zeros_like(st_ref)

    def chunk(ci, carry):
        r = pl.ds(pl.multiple_of(ci * c, c), c)
        g = g_ref[r, :]
        g_hi = g.astype(BF16)
        g_lo = (g - g_hi.astype(F32)).astype(BF16)
        f = jnp.exp(_dot(t_ref[...], jnp.concatenate([g_hi, g_lo], axis=0)))
        q = q_ref[r, :] * scale
        k = k_ref[r, :]
        v = v_ref[r, :].astype(BF16)
        a = jnp.where(diag, _dot_nt(q.astype(BF16), k.astype(BF16)), 0.0)
        for lvl in range(len(_GLA_LEVELS)):
            fl = f[2 * c + lvl * c: 3 * c + lvl * c]
            a = a + jnp.where(level_masks[lvl],
                              _dot_nt((q * fl).astype(BF16), (k * fl).astype(BF16)), 0.0)
        st = st_ref[...]
        qe = (q * f[0:c]).astype(BF16)
        o_ref[r, :] = _dot_nt(qe, st.astype(BF16)) + _dot(a.astype(BF16), v)
        kd = (k * f[c:2 * c]).astype(BF16)
        st_ref[...] = st * f[c - 1:c] + _dot_tn(v, kd)
        return carry

    lax.fori_loop(0, q_ref.shape[0] // c, chunk, 0, unroll=8)
    s_ref[0, 0] = st_ref[...].T


def _gla_prompt(p, la, tmat, batch, seq):
    return pl.pallas_call(
        _gla_kernel,
        grid=(batch, H_GLA),
        in_specs=[
            pl.BlockSpec((seq, DK_GLA), lambda b, h: (b, COL_QG // DK_GLA + h)),
            pl.BlockSpec((seq, DK_GLA), lambda b, h: (b, COL_KG // DK_GLA + h)),
            pl.BlockSpec((seq, DV_GLA), lambda b, h: (b, COL_VG // DV_GLA + h)),
            pl.BlockSpec((seq, DK_GLA), lambda b, h: (b, h)),
            pl.BlockSpec(tmat.shape, lambda b, h: (0, 0)),
        ],
        out_specs=[
            pl.BlockSpec((seq, DV_GLA), lambda b, h: (b, h)),
            pl.BlockSpec((1, 1, DK_GLA, DV_GLA), lambda b, h: (b, h, 0, 0)),
        ],
        out_shape=[
            jax.ShapeDtypeStruct((batch * seq, D_GLA_V), F32),
            jax.ShapeDtypeStruct((batch, H_GLA, DK_GLA, DV_GLA), F32),
        ],
        scratch_shapes=[pltpu.VMEM((DV_GLA, DK_GLA), F32)],
        compiler_params=_params("parallel", "parallel"),
        name="gla_prompt",
    )(p, p, p, la, tmat)


def _merge_kernel(x_ref, oa_ref, og_ref, rg_ref, ga_ref, gg_ref, w_ref, o_ref):
    a = (_rms(oa_ref[...]) * ga_ref[...]).astype(BF16)
    parts = []
    for h in range(H_GLA):
        cols = slice(h * DV_GLA, (h + 1) * DV_GLA)
        r = rg_ref[:, cols]
        parts.append((_rms(og_ref[:, cols]) * gg_ref[...] * (r * jax.nn.sigmoid(r))).astype(BF16))
    gg = jnp.concatenate(parts, axis=-1)
    o_ref[...] = x_ref[...] + _dot(a, w_ref[0:D_ATT, :]) + _dot(gg, w_ref[D_ATT:, :])


def _merge(x, o_att, o_gla, p, g_att, g_gla, w_out, *, tm):
    m = x.shape[0]
    return pl.pallas_call(
        _merge_kernel,
        grid=(m // tm,),
        in_specs=[
            pl.BlockSpec((tm, D_MODEL), lambda i: (i, 0)),
            pl.BlockSpec((tm, D_ATT), lambda i: (i, 0)),
            pl.BlockSpec((tm, D_GLA_V), lambda i: (i, 0)),
            pl.BlockSpec((tm, D_GLA_V), lambda i: (i, COL_RG // D_GLA_V)),
            pl.BlockSpec((1, D_ATT), lambda i: (0, 0)),
            pl.BlockSpec((1, DV_GLA), lambda i: (0, 0)),
            pl.BlockSpec((D_MODEL, D_MODEL), lambda i: (0, 0)),
        ],
        out_specs=pl.BlockSpec((tm, D_MODEL), lambda i: (i, 0)),
        out_shape=jax.ShapeDtypeStruct((m, D_MODEL), F32),
        compiler_params=_params("parallel"),
        name="merge",
    )(x, o_att, o_gla, p, g_att, g_gla, w_out)


SATT_HEADS_PER_STEP = 4


def _sattn_heads(q, kn, vn, k_ref, v_ref, s_ref):
    scale = HD_ATT ** -0.5
    wb = k_ref.shape[-1]
    t = lax.broadcasted_iota(jnp.int32, (1, wb), 1)
    cnt = jnp.zeros((1, wb), F32)
    for w, d in DIL_PATTERNS:
        cnt = cnt + jnp.where((t >= wb - w) & ((t & (d - 1)) == 0), 1.0, 0.0)
    bias = jnp.where(cnt > 0.0, 0.0, jnp.float32(-jnp.inf))
    n_pat = float(len(DIL_PATTERNS))
    n_heads = k_ref.shape[1]
    q = q * scale
    q16 = q.astype(BF16)
    s_new = jnp.sum(q * kn, axis=-1, keepdims=True)
    for h in range(n_heads):
        s_ref[h:h + 1, :] = _dot(q16, k_ref[0, h].astype(BF16))[h:h + 1, :]
    s = s_ref[...] + bias
    m = jnp.maximum(jnp.max(s, axis=-1, keepdims=True), s_new)
    e = cnt * jnp.exp(s - m)
    e_new = n_pat * jnp.exp(s_new - m)
    den = jnp.sum(e, axis=-1, keepdims=True) + e_new
    e16 = e.astype(BF16)
    head = lax.broadcasted_iota(jnp.int32, (n_heads, HD_ATT), 0)
    num = e_new * vn
    for h in range(n_heads):
        num = num + jnp.where(head == h, _dot_nt(e16, v_ref[0, h].astype(BF16)), 0.0)
    return num / den


SGLA_SEQS_PER_STEP = 4
_SPLIT = 3


def _bf16_pieces(x):
    pieces = []
    for _ in range(_SPLIT):
        p = x.astype(BF16)
        pieces.append(p)
        x = x - p.astype(F32)
    return pieces


def _sgla_kernel(q_ref, k_ref, g_ref, v_ref, s_ref, so_ref, o_ref):
    scale = DK_GLA ** -0.5
    n_vec = 3
    sel_rows = 16
    r = lax.broadcasted_iota(jnp.int32, (sel_rows, n_vec * DV_GLA), 0)
    c = lax.broadcasted_iota(jnp.int32, (sel_rows, n_vec * DV_GLA), 1)
    selector = jnp.where((r // _SPLIT == c // DV_GLA) & (r < n_vec * _SPLIT), 1.0, 0.0).astype(BF16)
    pad = jnp.zeros((sel_rows - n_vec * _SPLIT, DK_GLA), BF16)
    for s in range(q_ref.shape[0]):
        for h in range(H_GLA):
            hrow = slice(h, h + 1)
            vecs = (jnp.exp(g_ref[s, hrow, :]), k_ref[s, hrow, :], q_ref[s, hrow, :] * scale)
            lhs = jnp.concatenate([p for x in vecs for p in _bf16_pieces(x)] + [pad], axis=0)
            cols = _dot_tn(lhs, selector)
            decay, kcol, qcol = (cols[:, i * DV_GLA:(i + 1) * DV_GLA] for i in range(n_vec))
            s_new = decay * s_ref[s, h] + kcol * v_ref[s, hrow, :]
            so_ref[s, h] = s_new
            o_ref[s, hrow, :] = jnp.sum(qcol * s_new, axis=0, keepdims=True)


def _gla_sample(q, k, g, v, state):
    nb = state.shape[0]
    bs = SGLA_SEQS_PER_STEP
    krow = pl.BlockSpec((bs, H_GLA, DK_GLA), lambda b: (b, 0, 0))
    vrow = pl.BlockSpec((bs, H_GLA, DV_GLA), lambda b: (b, 0, 0))
    st = pl.BlockSpec((bs, H_GLA, DK_GLA, DV_GLA), lambda b: (b, 0, 0, 0))
    return pl.pallas_call(
        _sgla_kernel,
        grid=(nb // bs,),
        in_specs=[krow, krow, krow, vrow, st],
        out_specs=[st, vrow],
        out_shape=[jax.ShapeDtypeStruct((nb, H_GLA, DK_GLA, DV_GLA), F32),
                   jax.ShapeDtypeStruct((nb, H_GLA, DV_GLA), F32)],
        compiler_params=_params("parallel"),
        name="gla_sample",
    )(q, k, g, v, state)


def kernel(x_prompt, x_sample, cache_att_k, cache_att_v, state_gla, g_ffn1, w_ffn1_in, w_ffn1_out, g_mix, w_in, w_gate2, b_gate2, g_att_out, g_gla_out, w_out, g_ffn2, w_ffn2_in, w_ffn2_out, g_final):
    depth = w_in.shape[0]
    assert depth == 1
    batch, seq, _ = x_prompt.shape
    nb, dec_seq, _ = x_sample.shape
    assert dec_seq == 1
    xp = x_prompt.reshape(batch * seq, D_MODEL)
    xs = x_sample.reshape(nb, D_MODEL)
    row = lambda a: a.reshape(1, -1)
    l = 0

    w_in_t = w_in[l].T
    w_lr = w_in_t[D_PROJ_MAIN:].astype(BF16)
    w_g2 = w_gate2[l].astype(BF16)
    b_g2 = row(b_gate2[l])
    w_o = w_out[l].astype(BF16)
    gf = row(g_final)
    tmat = jnp.asarray(_gla_exponent_matrix(), dtype=BF16)

    xs, w1a, w1b, w1o = _ffn(xs, row(g_ffn1[l]), w_ffn1_in[l], w_ffn1_in[l], w_ffn1_out[l], gf,
                             tm=nb, tf=512, final_norm=False, emit_bf16=True, w_tile=FFN_TILE)
    ps, las, w_main = _proj(xs, row(g_mix[l]), w_in_t, w_lr, w_g2, b_g2, tm=nb, tn=512, emit_bf16=True)
    heads = lambda c0: ps[:, c0:c0 + D_ATT].reshape(nb, H_ATT, HD_ATT)
    k_new, v_new = heads(COL_KA), heads(COL_VA)
    xp, o_att_s, w2i, w2o = _ffn(
        xp, row(g_ffn1[l]), w1a, w1b, w1o, gf, tm=1024, tf=FFN_TILE, final_norm=False, w_tile=FFN_TILE,
        side=(heads(COL_QA), k_new, v_new,
              jnp.transpose(cache_att_k[l], (0, 2, 3, 1)), jnp.transpose(cache_att_v[l], (0, 2, 3, 1)),
              w_ffn2_in[l], w_ffn2_out[l]))
    krows = lambda a: a.reshape(nb, H_GLA, DK_GLA)
    s_new, o_gla_s = _gla_sample(
        krows(ps[:, COL_QG:COL_QG + D_GLA_K]), krows(ps[:, COL_KG:COL_KG + D_GLA_K]), krows(las),
        ps[:, COL_VG:COL_VG + D_GLA_V].reshape(nb, H_GLA, DV_GLA), state_gla[l])
    xs = _merge(xs, o_att_s.reshape(nb, D_ATT), o_gla_s.reshape(nb, D_GLA_V), ps,
                row(g_att_out[l]), row(g_gla_out[l]), w_o, tm=nb)
    ys = _ffn(xs, row(g_ffn2[l]), w2i, w2i, w2o, gf, tm=nb, tf=FFN_TILE, final_norm=True)
    nk_s = k_new.reshape(1, nb, 1, H_ATT, HD_ATT)
    nv_s = v_new.reshape(1, nb, 1, H_ATT, HD_ATT)

    pp, lap = _proj(xp, row(g_mix[l]), w_main, w_lr, w_g2, b_g2, tm=1024, tn=1024)
    o_att, kt_p, vt_p = _attn_prompt(pp, batch, seq)
    o_gla, s_fin = _gla_prompt(pp, lap, tmat, batch, seq)
    xp = _merge(xp, o_att, o_gla, pp, row(g_att_out[l]), row(g_gla_out[l]), w_o, tm=512)
    yp = _ffn(xp, row(g_ffn2[l]), w2i, w2i, w2o, gf, tm=1024, tf=FFN_TILE, final_norm=True)
    nk_p = jnp.transpose(kt_p, (0, 3, 1, 2))[None]
    nv_p = jnp.transpose(vt_p, (0, 3, 1, 2))[None]

    return (yp.reshape(batch, seq, D_MODEL), ys.reshape(nb, 1, D_MODEL), nk_p, nv_p,
            s_fin[None], nk_s, nv_s, s_new[None])
```

```python
import functools

import numpy as np
import jax
import jax.numpy as jnp
from jax import lax
from jax.experimental import pallas as pl
from jax.experimental.pallas import tpu as pltpu

F32 = jnp.float32
BF16 = jnp.bfloat16

D_MODEL = 2048
D_FF = 5632
D_ATT = 1024
HD_ATT = 64
H_ATT = 16
H_GLA = 4
DK_GLA = 128
DV_GLA = 256
D_GLA_K = H_GLA * DK_GLA
D_GLA_V = H_GLA * DV_GLA
GATE_RANK = 16
GATE_TAU = 16.0
NORM_EPS = 1e-6
DIL_PATTERNS = ((128, 1), (512, 4), (2048, 16))
ATT_BLOCK = 128
ATT_GROUP = 16
GLA_CHUNK = 64
FFN_ROWS = 512
FFN_TILE = 512
D_PROJ_MAIN = 3 * D_ATT + 2 * D_GLA_K + 2 * D_GLA_V

COL_QA, COL_KA, COL_VA = 0, D_ATT, 2 * D_ATT
COL_QG = 3 * D_ATT
COL_KG = COL_QG + D_GLA_K
COL_VG = COL_KG + D_GLA_K
COL_RG = COL_VG + D_GLA_V

VMEM_LIMIT_BYTES = 56 * 1024 * 1024


def _rms(x):
    return x * lax.rsqrt(jnp.mean(x * x, axis=-1, keepdims=True) + NORM_EPS)


def _dot(a, b):
    return jnp.dot(a, b, preferred_element_type=F32)


def _dot_nt(a, b):
    return lax.dot_general(a, b, (((1,), (1,)), ((), ())), preferred_element_type=F32)


def _dot_tn(a, b):
    return lax.dot_general(a, b, (((0,), (0,)), ((), ())), preferred_element_type=F32)


def _params(*sem):
    return pltpu.CompilerParams(dimension_semantics=sem, vmem_limit_bytes=VMEM_LIMIT_BYTES)


def _store_col_tiles(dst_ref, w):
    tile = dst_ref.shape[-1]
    for t in range(dst_ref.shape[0]):
        dst_ref[t] = w[:, t * tile:(t + 1) * tile]


def _ffn_step(x_ref, g_ref, wa, wb, wo, gf_ref, o_ref, h_ref, final_norm, side_jobs=(None, None)):
    j = pl.program_id(1)

    @pl.when(j == 0)
    def _():
        h_ref[...] = (_rms(x_ref[...]) * g_ref[...]).astype(BF16)
        o_ref[...] = jnp.zeros_like(o_ref)

    h = h_ref[...]
    a = _dot(h, wa)
    if side_jobs[0] is not None:
        side_jobs[0]()
    b = _dot(h, wb)
    act = (a * jax.nn.sigmoid(a) * b).astype(BF16)
    o_ref[...] += _dot(act, wo)
    if side_jobs[1] is not None:
        side_jobs[1]()

    @pl.when(j == pl.num_programs(1) - 1)
    def _():
        y = x_ref[...] + 0.5 * o_ref[...]
        if final_norm:
            y = _rms(y) * gf_ref[...]
        o_ref[...] = y


def _ffn_kernel(x_ref, g_ref, wa_ref, wb_ref, wo_ref, gf_ref, o_ref, *rest, final_norm, emit_bf16):
    wa, wb, wo = wa_ref[...], wb_ref[...], wo_ref[...]
    if emit_bf16:
        wa, wb, wo = wa.astype(BF16), wb.astype(BF16), wo.astype(BF16)
        _store_col_tiles(rest[0], wa)
        _store_col_tiles(rest[1], wb)
        rest[2][...] = wo
    _ffn_step(x_ref, g_ref, wa, wb, wo, gf_ref, o_ref, rest[-1], final_norm)


def _ffn_side_kernel(x_ref, g_ref, wa_ref, wb_ref, wo_ref, gf_ref,
                     q_ref, kn_ref, vn_ref, k_ref, v_ref, cwi_ref, cwo_ref,
                     o_ref, so_ref, cwi16_ref, cwo16_ref, h_ref, s_ref, *, final_norm, n_side_blocks):
    hs = k_ref.shape[1]
    groups = H_ATT // hs
    step = pl.program_id(0) * pl.num_programs(1) + pl.program_id(1)
    blk = jnp.minimum(step, n_side_blocks - 1)
    heads = pl.ds(pl.multiple_of((blk % groups) * hs, hs), hs)

    def scores():
        _sattn_scores(q_ref[0, heads, :], k_ref, s_ref)
        _store_col_tiles(cwi16_ref, cwi_ref[...].astype(BF16))
        cwo16_ref[...] = cwo_ref[...].astype(BF16)

    def rest():
        so_ref[0, heads, :] = _sattn_finish(q_ref[0, heads, :], kn_ref[0, heads, :], vn_ref[0, heads, :],
                                            v_ref, s_ref)

    _ffn_step(x_ref, g_ref, wa_ref[...], wb_ref[...], wo_ref[...], gf_ref, o_ref, h_ref, final_norm,
              (scores, rest))


def _ffn(x, g, wa, wb, wo, g_final, *, tm, tf, final_norm, emit_bf16=False, side=None, w_tile=None):
    m = x.shape[0]
    nf = D_FF // tf
    ni = m // tm

    def w_in_spec(w, off):
        if w.ndim == 3:
            assert w.shape[1:] == (D_MODEL, tf)
            return pl.BlockSpec((None, D_MODEL, tf), lambda i, j: (j + off, 0, 0))
        return pl.BlockSpec((D_MODEL, tf), lambda i, j: (0, j + off))

    both_halves = wb.shape[0] == 2 * nf if wb.ndim == 3 else wb.shape[1] == 2 * D_FF
    in_specs = [
        pl.BlockSpec((tm, D_MODEL), lambda i, j: (i, 0)),
        pl.BlockSpec((1, D_MODEL), lambda i, j: (0, 0)),
        w_in_spec(wa, 0),
        w_in_spec(wb, nf if both_halves else 0),
        pl.BlockSpec((tf, D_MODEL), lambda i, j: (j, 0)),
        pl.BlockSpec((1, D_MODEL), lambda i, j: (0, 0)),
    ]
    out_specs = [pl.BlockSpec((tm, D_MODEL), lambda i, j: (i, 0))]
    out_shape = [jax.ShapeDtypeStruct((m, D_MODEL), F32)]
    scratch = [pltpu.VMEM((tm, D_MODEL), BF16)]
    operands = (x, g, wa, wb, wo, g_final)
    if side is None:
        body = functools.partial(_ffn_kernel, final_norm=final_norm, emit_bf16=emit_bf16)
        sem = ("parallel", "arbitrary")
    if emit_bf16:
        assert m == tm and side is None
        half = pl.BlockSpec((tf // w_tile, D_MODEL, w_tile), lambda i, j: (j, 0, 0))
        out_specs += [half, half, pl.BlockSpec((tf, D_MODEL), lambda i, j: (j, 0))]
        out_shape += [jax.ShapeDtypeStruct((D_FF // w_tile, D_MODEL, w_tile), BF16),
                      jax.ShapeDtypeStruct((D_FF // w_tile, D_MODEL, w_tile), BF16),
                      jax.ShapeDtypeStruct((D_FF, D_MODEL), BF16)]
    if side is not None:
        q, kn, vn, cache_kt, cache_vt, cw_in, cw_out = side
        nb, _, _, wb_len = cache_kt.shape
        assert all(wb_len % d == 0 and w <= wb_len for w, d in DIL_PATTERNS)
        hs = SATT_HEADS_PER_STEP
        groups = H_ATT // hs
        n_blocks = nb * groups
        assert ni * nf >= n_blocks and D_MODEL % ni == 0 and (2 * D_FF) % nf == 0
        sblk = lambda i, j: jnp.minimum(i * nf + j, n_blocks - 1)
        seq_row = pl.BlockSpec((1, H_ATT, HD_ATT), lambda i, j: (sblk(i, j) // groups, 0, 0))
        cache = pl.BlockSpec((1, hs, HD_ATT, wb_len),
                             lambda i, j: (sblk(i, j) // groups, sblk(i, j) % groups, 0, 0))
        slab = 2 * D_FF // nf
        assert slab % w_tile == 0
        cwi = pl.BlockSpec((D_MODEL // ni, slab), lambda i, j: (i, j))
        cwi16 = pl.BlockSpec((slab // w_tile, D_MODEL // ni, w_tile), lambda i, j: (j, i, 0))
        cwo = pl.BlockSpec((D_FF // nf, D_MODEL // ni), lambda i, j: (j, i))
        in_specs += [seq_row, seq_row, seq_row, cache, cache, cwi, cwo]
        out_specs += [seq_row, cwi16, cwo]
        out_shape += [jax.ShapeDtypeStruct((nb, H_ATT, HD_ATT), F32),
                      jax.ShapeDtypeStruct((2 * D_FF // w_tile, D_MODEL, w_tile), BF16),
                      jax.ShapeDtypeStruct(cw_out.shape, BF16)]
        scratch.append(pltpu.VMEM((hs, wb_len), F32))
        operands += (q, kn, vn, cache_kt, cache_vt, cw_in, cw_out)
        body = functools.partial(_ffn_side_kernel, final_norm=final_norm, n_side_blocks=n_blocks)
        sem = ("arbitrary", "arbitrary")
    outs = pl.pallas_call(
        body,
        grid=(ni, nf),
        in_specs=in_specs,
        out_specs=out_specs,
        out_shape=out_shape,
        scratch_shapes=scratch,
        compiler_params=_params(*sem),
        name="ffn",
    )(*operands)
    return outs if (emit_bf16 or side is not None) else outs[0]


def _proj_kernel(x_ref, g_ref, w_ref, wlr_ref, wg2_ref, bg2_ref, p_ref, la_ref, *rest, emit_bf16):
    h_ref = rest[-1]
    j = pl.program_id(1)

    @pl.when(j == 0)
    def _():
        h = (_rms(x_ref[...]) * g_ref[...]).astype(BF16)
        h_ref[...] = h
        lr = _dot_nt(h, wlr_ref[...])
        z = _dot(lr.astype(BF16), wg2_ref[...]) + bg2_ref[...]
        log_sig = jnp.minimum(z, 0.0) - jnp.log1p(jnp.exp(-jnp.abs(z)))
        la_ref[...] = log_sig * (1.0 / GATE_TAU)

    w = w_ref[...]
    if emit_bf16:
        w = w.astype(BF16)
        rest[0][...] = w
    p_ref[...] = _dot_nt(h_ref[...], w)


def _proj(x, g, w_t, w_lr_t, w_g2, b_g2, *, tm, tn, emit_bf16=False):
    m = x.shape[0]
    out_specs = [
        pl.BlockSpec((tm, tn), lambda i, j: (i, j)),
        pl.BlockSpec((tm, D_GLA_K), lambda i, j: (i, 0)),
    ]
    out_shape = [
        jax.ShapeDtypeStruct((m, D_PROJ_MAIN), F32),
        jax.ShapeDtypeStruct((m, D_GLA_K), F32),
    ]
    if emit_bf16:
        assert m == tm
        out_specs.append(pl.BlockSpec((tn, D_MODEL), lambda i, j: (j, 0)))
        out_shape.append(jax.ShapeDtypeStruct((D_PROJ_MAIN, D_MODEL), BF16))
    return pl.pallas_call(
        functools.partial(_proj_kernel, emit_bf16=emit_bf16),
        grid=(m // tm, D_PROJ_MAIN // tn),
        in_specs=[
            pl.BlockSpec((tm, D_MODEL), lambda i, j: (i, 0)),
            pl.BlockSpec((1, D_MODEL), lambda i, j: (0, 0)),
            pl.BlockSpec((tn, D_MODEL), lambda i, j: (j, 0)),
            pl.BlockSpec((GATE_RANK, D_MODEL), lambda i, j: (0, 0)),
            pl.BlockSpec((GATE_RANK, D_GLA_K), lambda i, j: (0, 0)),
            pl.BlockSpec((1, D_GLA_K), lambda i, j: (0, 0)),
        ],
        out_specs=out_specs,
        out_shape=out_shape,
        scratch_shapes=[pltpu.VMEM((tm, D_MODEL), BF16)],
        compiler_params=_params("parallel", "arbitrary"),
        name="proj",
    )(x, g, w_t, w_lr_t, w_g2, b_g2)


def _attn_kernel(q_ref, k_ref, v_ref, o_ref, kt_ref, vt_ref,
                 qp_ref, kp_ref, vp_ref, m_ref, l_ref, acc_ref, s_ref, ms_ref):
    blk = ATT_BLOCK
    for src_ref, dst_ref in ((k_ref, kt_ref), (v_ref, vt_ref)):
        t = src_ref[...].T
        dst_ref[0, 0] = t[:HD_ATT]
        dst_ref[0, 1] = t[HD_ATT:]

    scale = HD_ATT ** -0.5
    seq = q_ref.shape[0]
    ns = DIL_PATTERNS[-1][1]
    lane_lo = lax.broadcasted_iota(jnp.int32, (1, 2 * HD_ATT), 1) < HD_ATT
    neg = jnp.float32(-jnp.inf)

    assert ns == 16
    for src_ref, tmp_ref, dst_ref in ((q_ref, m_ref, qp_ref), (k_ref, l_ref, kp_ref), (v_ref, acc_ref, vp_ref)):
        for r4 in range(4):
            x = src_ref[pl.ds(r4, seq // 4, stride=4), :]
            tmp_ref[pl.ds(r4 * (seq // 4), seq // 4), :] = x * scale if src_ref is q_ref else x
        for r4 in range(4):
            for a in range(4):
                dst_ref[pl.ds((4 * a + r4) * blk, blk), :] = tmp_ref[pl.ds(r4 * (seq // 4) + a, blk, stride=4), :]

    def run_pattern(d, first_pattern):
        na = ns // d
        plen = blk // na
        per_stream = seq // (d * blk)

        def offset(idx):
            return na * (idx & (plen - 1)) + idx // plen

        qpos = offset(lax.broadcasted_iota(jnp.int32, (blk, blk), 0))
        kpos = offset(lax.broadcasted_iota(jnp.int32, (blk, blk), 1))
        bias_cur = jnp.where(kpos <= qpos, 0.0, neg)
        if per_stream > 1:
            bias_prev = jnp.where(kpos >= qpos, 0.0, neg)
            bias_band = jnp.concatenate([bias_prev, bias_cur], axis=1)
            bias_first = jnp.concatenate([jnp.full((blk, blk), neg, F32), bias_cur], axis=1)

        def pieces(rd, n):
            return [pl.ds((a * d + rd) * blk + plen * n, plen) for a in range(na)]

        def gather(ref, ps):
            return jnp.concatenate([ref[p, :] for p in ps], axis=0)

        nk = 2 * blk if per_stream > 1 else blk
        heads = (lane_lo, jnp.logical_not(lane_lo))

        def keys(ref, rd, n):
            cur = gather(ref, pieces(rd, n))
            if per_stream == 1:
                return cur.astype(BF16)
            prev = gather(ref, pieces(rd, max(n - 1, 0)))
            return jnp.concatenate([prev, cur], axis=0).astype(BF16)

        def group(g):
            blocks = [((g * ATT_GROUP + b) % d, (g * ATT_GROUP + b) // d) for b in range(ATT_GROUP)]
            for b, (rd, n) in enumerate(blocks):
                q = gather(qp_ref, pieces(rd, n))
                kk = keys(kp_ref, rd, n)
                bias = bias_cur if per_stream == 1 else (bias_band if n > 0 else bias_first)
                for h, sel in enumerate(heads):
                    qh = jnp.where(sel, q, 0.0).astype(BF16)
                    s_ref[b, h, :, :nk] = _dot_nt(qh, kk) + bias
            for b in range(ATT_GROUP):
                for h in range(2):
                    m = jnp.max(s_ref[b, h, :, :nk], axis=-1, keepdims=True)
                    ms_ref[b, h] = jnp.broadcast_to(m, (blk, blk))
            for b, (rd, n) in enumerate(blocks):
                vv = jnp.concatenate([keys(vp_ref, rd, n), jnp.ones((nk, blk), BF16)], axis=1)
                res = []
                for h in range(2):
                    mh = ms_ref[b, h]
                    mh = jnp.concatenate([mh, mh], axis=1) if nk == 2 * blk else mh
                    res.append(_dot(jnp.exp(s_ref[b, h, :, :nk] - mh).astype(BF16), vv))
                u = jnp.where(lane_lo, res[0][:, :blk], res[1][:, :blk])
                l = jnp.where(lane_lo, res[0][:, blk:], res[1][:, blk:])
                m = jnp.where(lane_lo, ms_ref[b, 0], ms_ref[b, 1])
                for a, p in enumerate(pieces(rd, n)):
                    sl = slice(a * plen, (a + 1) * plen)
                    if first_pattern:
                        m_ref[p, :] = m[sl]
                        l_ref[p, :] = l[sl]
                        acc_ref[p, :] = u[sl]
                    else:
                        m_old = m_ref[p, :]
                        m_new = jnp.maximum(m_old, m[sl])
                        a_old = jnp.exp(m_old - m_new)
                        a_blk = jnp.exp(m[sl] - m_new)
                        m_ref[p, :] = m_new
                        l_ref[p, :] = a_old * l_ref[p, :] + a_blk * l[sl]
                        acc_ref[p, :] = a_old * acc_ref[p, :] + a_blk * u[sl]

        for g in range(d * per_stream // ATT_GROUP):
            group(g)

    for idx, (_, d) in enumerate(DIL_PATTERNS):
        run_pattern(d, idx == 0)

    for r4 in range(4):
        for a in range(4):
            rows = pl.ds((4 * a + r4) * blk, blk)
            qp_ref[pl.ds(r4 * (seq // 4) + a, blk, stride=4), :] = acc_ref[rows, :] / l_ref[rows, :]
    for r4 in range(4):
        o_ref[pl.ds(r4, seq // 4, stride=4), :] = qp_ref[pl.ds(r4 * (seq // 4), seq // 4), :]


def _attn_prompt(p, batch, seq):
    ns = DIL_PATTERNS[-1][1]
    assert seq == ns * ATT_BLOCK
    assert all(w // d == ATT_BLOCK and ns % d == 0 and ATT_BLOCK * d // ns >= 8 for w, d in DIL_PATTERNS)
    lanes = 2 * HD_ATT
    n_pairs = H_ATT // 2
    return pl.pallas_call(
        _attn_kernel,
        grid=(batch, n_pairs),
        in_specs=[
            pl.BlockSpec((seq, lanes), lambda b, h: (b, COL_QA // lanes + h)),
            pl.BlockSpec((seq, lanes), lambda b, h: (b, COL_KA // lanes + h)),
            pl.BlockSpec((seq, lanes), lambda b, h: (b, COL_VA // lanes + h)),
        ],
        out_specs=[
            pl.BlockSpec((seq, lanes), lambda b, h: (b, h)),
            pl.BlockSpec((1, 2, HD_ATT, seq), lambda b, h: (b, h, 0, 0)),
            pl.BlockSpec((1, 2, HD_ATT, seq), lambda b, h: (b, h, 0, 0)),
        ],
        out_shape=[
            jax.ShapeDtypeStruct((batch * seq, D_ATT), F32),
            jax.ShapeDtypeStruct((batch, H_ATT, HD_ATT, seq), F32),
            jax.ShapeDtypeStruct((batch, H_ATT, HD_ATT, seq), F32),
        ],
        scratch_shapes=[pltpu.VMEM((seq, lanes), F32)] * 6 + [
            pltpu.VMEM((ATT_GROUP, 2, ATT_BLOCK, 2 * ATT_BLOCK), F32),
            pltpu.VMEM((ATT_GROUP, 2, ATT_BLOCK, lanes), F32),
        ],
        compiler_params=_params("parallel", "parallel"),
        name="attn_prompt",
    )(p, p, p)


_GLA_LEVELS = (32, 16, 8, 4, 2, 1)


def _gla_exponent_matrix():
    c = GLA_CHUNK
    t = np.arange(c)[:, None]
    u = np.arange(c)[None, :]
    mats = [(u <= t), (u > t)]
    for h in _GLA_LEVELS:
        mid = (t // (2 * h)) * (2 * h) + h - 1
        upper = (t % (2 * h)) >= h
        mats.append(np.where(upper, (u > mid) & (u <= t), (u > t) & (u <= mid)))
    tmat = np.concatenate(mats, axis=0).astype(np.float32)
    return np.concatenate([tmat, tmat], axis=1)


def _gla_kernel(q_ref, k_ref, v_ref, g_ref, t_ref, o_ref, s_ref, st_ref):
    c = GLA_CHUNK
    scale = DK_GLA ** -0.5
    ti = lax.broadcasted_iota(jnp.int32, (c, c), 0)
    si = lax.broadcasted_iota(jnp.int32, (c, c), 1)
    txs = ti ^ si
    below = ti > si
    level_masks = [below & (txs >= h) & (txs < 2 * h) for h in _GLA_LEVELS]
    diag = ti == si

    st_ref[...] = jnp.zeros_like(st_ref)

    def chunk(ci, carry):
        r = pl.ds(pl.multiple_of(ci * c, c), c)
        g = g_ref[r, :]
        g_hi = g.astype(BF16)
        g_lo = (g - g_hi.astype(F32)).astype(BF16)
        f = jnp.exp(_dot(t_ref[...], jnp.concatenate([g_hi, g_lo], axis=0)))
        q = q_ref[r, :] * scale
        k = k_ref[r, :]
        v = v_ref[r, :].astype(BF16)
        a = jnp.where(diag, _dot_nt(q.astype(BF16), k.astype(BF16)), 0.0)
        for lvl in range(len(_GLA_LEVELS)):
            fl = f[2 * c + lvl * c: 3 * c + lvl * c]
            a = a + jnp.where(level_masks[lvl],
                              _dot_nt((q * fl).astype(BF16), (k * fl).astype(BF16)), 0.0)
        st = st_ref[...]
        qe = (q * f[0:c]).astype(BF16)
        o_ref[r, :] = _dot_nt(qe, st.astype(BF16)) + _dot(a.astype(BF16), v)
        kd = (k * f[c:2 * c]).astype(BF16)
        st_ref[...] = st * f[c - 1:c] + _dot_tn(v, kd)
        return carry

    lax.fori_loop(0, q_ref.shape[0] // c, chunk, 0, unroll=8)
    s_ref[0, 0] = st_ref[...].T


def _gla_prompt(p, la, tmat, batch, seq):
    return pl.pallas_call(
        _gla_kernel,
        grid=(batch, H_GLA),
        in_specs=[
            pl.BlockSpec((seq, DK_GLA), lambda b, h: (b, COL_QG // DK_GLA + h)),
            pl.BlockSpec((seq, DK_GLA), lambda b, h: (b, COL_KG // DK_GLA + h)),
            pl.BlockSpec((seq, DV_GLA), lambda b, h: (b, COL_VG // DV_GLA + h)),
            pl.BlockSpec((seq, DK_GLA), lambda b, h: (b, h)),
            pl.BlockSpec(tmat.shape, lambda b, h: (0, 0)),
        ],
        out_specs=[
            pl.BlockSpec((seq, DV_GLA), lambda b, h: (b, h)),
            pl.BlockSpec((1, 1, DK_GLA, DV_GLA), lambda b, h: (b, h, 0, 0)),
        ],
        out_shape=[
            jax.ShapeDtypeStruct((batch * seq, D_GLA_V), F32),
            jax.ShapeDtypeStruct((batch, H_GLA, DK_GLA, DV_GLA), F32),
        ],
        scratch_shapes=[pltpu.VMEM((DV_GLA, DK_GLA), F32)],
        compiler_params=_params("parallel", "parallel"),
        name="gla_prompt",
    )(p, p, p, la, tmat)


def _merge_kernel(x_ref, oa_ref, og_ref, rg_ref, ga_ref, gg_ref, w_ref, o_ref):
    a = (_rms(oa_ref[...]) * ga_ref[...]).astype(BF16)
    parts = []
    for h in range(H_GLA):
        cols = slice(h * DV_GLA, (h + 1) * DV_GLA)
        r = rg_ref[:, cols]
        parts.append((_rms(og_ref[:, cols]) * gg_ref[...] * (r * jax.nn.sigmoid(r))).astype(BF16))
    gg = jnp.concatenate(parts, axis=-1)
    o_ref[...] = x_ref[...] + _dot(a, w_ref[0:D_ATT, :]) + _dot(gg, w_ref[D_ATT:, :])


def _merge(x, o_att, o_gla, p, g_att, g_gla, w_out, *, tm):
    m = x.shape[0]
    return pl.pallas_call(
        _merge_kernel,
        grid=(m // tm,),
        in_specs=[
            pl.BlockSpec((tm, D_MODEL), lambda i: (i, 0)),
            pl.BlockSpec((tm, D_ATT), lambda i: (i, 0)),
            pl.BlockSpec((tm, D_GLA_V), lambda i: (i, 0)),
            pl.BlockSpec((tm, D_GLA_V), lambda i: (i, COL_RG // D_GLA_V)),
            pl.BlockSpec((1, D_ATT), lambda i: (0, 0)),
            pl.BlockSpec((1, DV_GLA), lambda i: (0, 0)),
            pl.BlockSpec((D_MODEL, D_MODEL), lambda i: (0, 0)),
        ],
        out_specs=pl.BlockSpec((tm, D_MODEL), lambda i: (i, 0)),
        out_shape=jax.ShapeDtypeStruct((m, D_MODEL), F32),
        compiler_params=_params("parallel"),
        name="merge",
    )(x, o_att, o_gla, p, g_att, g_gla, w_out)


SATT_HEADS_PER_STEP = 4


def _sattn_scores(q, k_ref, s_ref):
    q16 = (q * HD_ATT ** -0.5).astype(BF16)
    for h in range(k_ref.shape[1]):
        s_ref[h:h + 1, :] = _dot(q16, k_ref[0, h].astype(BF16))[h:h + 1, :]


def _sattn_finish(q, kn, vn, v_ref, s_ref):
    scale = HD_ATT ** -0.5
    wb = v_ref.shape[-1]
    t = lax.broadcasted_iota(jnp.int32, (1, wb), 1)
    cnt = jnp.zeros((1, wb), F32)
    for w, d in DIL_PATTERNS:
        cnt = cnt + jnp.where((t >= wb - w) & ((t & (d - 1)) == 0), 1.0, 0.0)
    bias = jnp.where(cnt > 0.0, 0.0, jnp.float32(-jnp.inf))
    n_pat = float(len(DIL_PATTERNS))
    n_heads = v_ref.shape[1]
    s_new = jnp.sum(q * scale * kn, axis=-1, keepdims=True)
    s = s_ref[...] + bias
    m = jnp.maximum(jnp.max(s, axis=-1, keepdims=True), s_new)
    e = cnt * jnp.exp(s - m)
    e_new = n_pat * jnp.exp(s_new - m)
    den = jnp.sum(e, axis=-1, keepdims=True) + e_new
    assert n_heads <= 8
    lane = lax.broadcasted_iota(jnp.int32, (HD_ATT, 128), 1)
    cols = jnp.zeros((HD_ATT, 128), F32)
    for h in range(n_heads):
        col = jnp.sum(v_ref[0, h] * e[h:h + 1, :], axis=-1, keepdims=True)
        for r, piece in enumerate(_bf16_pieces(col)):
            cols = jnp.where(lane == 8 * r + h, piece.astype(F32), cols)
    eye = (lax.broadcasted_iota(jnp.int32, (HD_ATT, HD_ATT), 0)
           == lax.broadcasted_iota(jnp.int32, (HD_ATT, HD_ATT), 1))
    rows = _dot_tn(cols.astype(BF16), jnp.where(eye, 1.0, 0.0).astype(BF16))
    pv = sum(rows[8 * r:8 * r + n_heads] for r in range(_SPLIT))
    return (pv + e_new * vn) / den


SGLA_SEQS_PER_STEP = 4
_SPLIT = 3


def _bf16_pieces(x):
    pieces = []
    for _ in range(_SPLIT):
        p = x.astype(BF16)
        pieces.append(p)
        x = x - p.astype(F32)
    return pieces


def _sgla_kernel(q_ref, k_ref, g_ref, v_ref, s_ref, so_ref, o_ref):
    scale = DK_GLA ** -0.5
    n_vec = 3
    sel_rows = 16
    r = lax.broadcasted_iota(jnp.int32, (sel_rows, n_vec * DV_GLA), 0)
    c = lax.broadcasted_iota(jnp.int32, (sel_rows, n_vec * DV_GLA), 1)
    selector = jnp.where((r // _SPLIT == c // DV_GLA) & (r < n_vec * _SPLIT), 1.0, 0.0).astype(BF16)
    pad = jnp.zeros((sel_rows - n_vec * _SPLIT, DK_GLA), BF16)
    for s in range(q_ref.shape[0]):
        for h in range(H_GLA):
            hrow = slice(h, h + 1)
            vecs = (jnp.exp(g_ref[s, hrow, :]), k_ref[s, hrow, :], q_ref[s, hrow, :] * scale)
            lhs = jnp.concatenate([p for x in vecs for p in _bf16_pieces(x)] + [pad], axis=0)
            cols = _dot_tn(lhs, selector)
            decay, kcol, qcol = (cols[:, i * DV_GLA:(i + 1) * DV_GLA] for i in range(n_vec))
            s_new = decay * s_ref[s, h] + kcol * v_ref[s, hrow, :]
            so_ref[s, h] = s_new
            o_ref[s, hrow, :] = jnp.sum(qcol * s_new, axis=0, keepdims=True)


def _gla_sample(q, k, g, v, state):
    nb = state.shape[0]
    bs = SGLA_SEQS_PER_STEP
    krow = pl.BlockSpec((bs, H_GLA, DK_GLA), lambda b: (b, 0, 0))
    vrow = pl.BlockSpec((bs, H_GLA, DV_GLA), lambda b: (b, 0, 0))
    st = pl.BlockSpec((bs, H_GLA, DK_GLA, DV_GLA), lambda b: (b, 0, 0, 0))
    return pl.pallas_call(
        _sgla_kernel,
        grid=(nb // bs,),
        in_specs=[krow, krow, krow, vrow, st],
        out_specs=[st, vrow],
        out_shape=[jax.ShapeDtypeStruct((nb, H_GLA, DK_GLA, DV_GLA), F32),
                   jax.ShapeDtypeStruct((nb, H_GLA, DV_GLA), F32)],
        compiler_params=_params("parallel"),
        name="gla_sample",
    )(q, k, g, v, state)


def kernel(x_prompt, x_sample, cache_att_k, cache_att_v, state_gla, g_ffn1, w_ffn1_in, w_ffn1_out, g_mix, w_in, w_gate2, b_gate2, g_att_out, g_gla_out, w_out, g_ffn2, w_ffn2_in, w_ffn2_out, g_final):
    depth = w_in.shape[0]
    assert depth == 1
    batch, seq, _ = x_prompt.shape
    nb, dec_seq, _ = x_sample.shape
    assert dec_seq == 1
    xp = x_prompt.reshape(batch * seq, D_MODEL)
    xs = x_sample.reshape(nb, D_MODEL)
    row = lambda a: a.reshape(1, -1)
    l = 0

    w_in_t = w_in[l].T
    w_lr = w_in_t[D_PROJ_MAIN:].astype(BF16)
    w_g2 = w_gate2[l].astype(BF16)
    b_g2 = row(b_gate2[l])
    w_o = w_out[l].astype(BF16)
    gf = row(g_final)
    tmat = jnp.asarray(_gla_exponent_matrix(), dtype=BF16)

    xs, w1a, w1b, w1o = _ffn(xs, row(g_ffn1[l]), w_ffn1_in[l], w_ffn1_in[l], w_ffn1_out[l], gf,
                             tm=nb, tf=512, final_norm=False, emit_bf16=True, w_tile=FFN_TILE)
    ps, las, w_main = _proj(xs, row(g_mix[l]), w_in_t, w_lr, w_g2, b_g2, tm=nb, tn=512, emit_bf16=True)
    heads = lambda c0: ps[:, c0:c0 + D_ATT].reshape(nb, H_ATT, HD_ATT)
    k_new, v_new = heads(COL_KA), heads(COL_VA)
    xp, o_att_s, w2i, w2o = _ffn(
        xp, row(g_ffn1[l]), w1a, w1b, w1o, gf, tm=FFN_ROWS, tf=FFN_TILE, final_norm=False, w_tile=FFN_TILE,
        side=(heads(COL_QA), k_new, v_new,
              jnp.transpose(cache_att_k[l], (0, 2, 3, 1)), jnp.transpose(cache_att_v[l], (0, 2, 3, 1)),
              w_ffn2_in[l], w_ffn2_out[l]))
    krows = lambda a: a.reshape(nb, H_GLA, DK_GLA)
    s_new, o_gla_s = _gla_sample(
        krows(ps[:, COL_QG:COL_QG + D_GLA_K]), krows(ps[:, COL_KG:COL_KG + D_GLA_K]), krows(las),
        ps[:, COL_VG:COL_VG + D_GLA_V].reshape(nb, H_GLA, DV_GLA), state_gla[l])
    xs = _merge(xs, o_att_s.reshape(nb, D_ATT), o_gla_s.reshape(nb, D_GLA_V), ps,
                row(g_att_out[l]), row(g_gla_out[l]), w_o, tm=nb)
    ys = _ffn(xs, row(g_ffn2[l]), w2i, w2i, w2o, gf, tm=nb, tf=FFN_TILE, final_norm=True)
    nk_s = k_new.reshape(1, nb, 1, H_ATT, HD_ATT)
    nv_s = v_new.reshape(1, nb, 1, H_ATT, HD_ATT)

    pp, lap = _proj(xp, row(g_mix[l]), w_main, w_lr, w_g2, b_g2, tm=1024, tn=1024)
    o_att, kt_p, vt_p = _attn_prompt(pp, batch, seq)
    o_gla, s_fin = _gla_prompt(pp, lap, tmat, batch, seq)
    xp = _merge(xp, o_att, o_gla, pp, row(g_att_out[l]), row(g_gla_out[l]), w_o, tm=512)
    yp = _ffn(xp, row(g_ffn2[l]), w2i, w2i, w2o, gf, tm=FFN_ROWS, tf=FFN_TILE, final_norm=True)
    nk_p = jnp.transpose(kt_p, (0, 3, 1, 2))[None]
    nv_p = jnp.transpose(vt_p, (0, 3, 1, 2))[None]

    return (yp.reshape(batch, seq, D_MODEL), ys.reshape(nb, 1, D_MODEL), nk_p, nv_p,
            s_fin[None], nk_s, nv_s, s_new[None])
```

```python
import functools

import numpy as np
import jax
import jax.numpy as jnp
from jax import lax
from jax.experimental import pallas as pl
from jax.experimental.pallas import tpu as pltpu

F32 = jnp.float32
BF16 = jnp.bfloat16

D_MODEL = 2048
D_FF = 5632
D_ATT = 1024
HD_ATT = 64
H_ATT = 16
H_GLA = 4
DK_GLA = 128
DV_GLA = 256
D_GLA_K = H_GLA * DK_GLA
D_GLA_V = H_GLA * DV_GLA
GATE_RANK = 16
GATE_TAU = 16.0
NORM_EPS = 1e-6
DIL_PATTERNS = ((128, 1), (512, 4), (2048, 16))
ATT_BLOCK = 128
ATT_GROUP = 16
GLA_CHUNK = 64
FFN_ROWS = 512
FFN_TILE = 512
D_PROJ_MAIN = 3 * D_ATT + 2 * D_GLA_K + 2 * D_GLA_V

COL_QA, COL_KA, COL_VA = 0, D_ATT, 2 * D_ATT
COL_QG = 3 * D_ATT
COL_KG = COL_QG + D_GLA_K
COL_VG = COL_KG + D_GLA_K
COL_RG = COL_VG + D_GLA_V

VMEM_LIMIT_BYTES = 56 * 1024 * 1024


def _rms(x):
    return x * lax.rsqrt(jnp.mean(x * x, axis=-1, keepdims=True) + NORM_EPS)


def _dot(a, b):
    return jnp.dot(a, b, preferred_element_type=F32)


def _dot_nt(a, b):
    return lax.dot_general(a, b, (((1,), (1,)), ((), ())), preferred_element_type=F32)


def _dot_tn(a, b):
    return lax.dot_general(a, b, (((0,), (0,)), ((), ())), preferred_element_type=F32)


def _params(*sem):
    return pltpu.CompilerParams(dimension_semantics=sem, vmem_limit_bytes=VMEM_LIMIT_BYTES)


def _ffn_step(x_ref, g_ref, wa, wb, wo, gf_ref, o_ref, h_ref, final_norm, side_jobs=(None, None)):
    j = pl.program_id(1)

    @pl.when(j == 0)
    def _():
        h_ref[...] = (_rms(x_ref[...]) * g_ref[...]).astype(BF16)
        o_ref[...] = jnp.zeros_like(o_ref)

    h = h_ref[...]
    a = _dot(h, wa)
    if side_jobs[0] is not None:
        side_jobs[0]()
    b = _dot(h, wb)
    act = (a * jax.nn.sigmoid(a) * b).astype(BF16)
    o_ref[...] += _dot(act, wo)
    if side_jobs[1] is not None:
        side_jobs[1]()

    @pl.when(j == pl.num_programs(1) - 1)
    def _():
        y = x_ref[...] + 0.5 * o_ref[...]
        if final_norm:
            y = _rms(y) * gf_ref[...]
        o_ref[...] = y


def _ffn_kernel(x_ref, g_ref, wa_ref, wb_ref, wo_ref, gf_ref, o_ref, *rest, final_norm, emit_bf16):
    wa, wb, wo = wa_ref[...], wb_ref[...], wo_ref[...]
    if emit_bf16:
        wa, wb, wo = wa.astype(BF16), wb.astype(BF16), wo.astype(BF16)
        for dst_ref, w in zip(rest[:3], (wa, wb, wo)):
            dst_ref[...] = w
    _ffn_step(x_ref, g_ref, wa, wb, wo, gf_ref, o_ref, rest[-1], final_norm)


def _ffn_side_kernel(x_ref, g_ref, wa_ref, wb_ref, wo_ref, gf_ref,
                     q_ref, kn_ref, vn_ref, k_ref, v_ref, cwi_ref, cwo_ref,
                     o_ref, so_ref, cwi16_ref, cwo16_ref, h_ref, s_ref, *, final_norm, n_side_blocks):
    hs = k_ref.shape[1]
    groups = H_ATT // hs
    step = pl.program_id(0) * pl.num_programs(1) + pl.program_id(1)
    blk = jnp.minimum(step, n_side_blocks - 1)
    heads = pl.ds(pl.multiple_of((blk % groups) * hs, hs), hs)

    def scores():
        _sattn_scores(q_ref[0, heads, :], k_ref, s_ref)
        cwi16_ref[...] = cwi_ref[...].astype(BF16)
        cwo16_ref[...] = cwo_ref[...].astype(BF16)

    def rest():
        so_ref[0, heads, :] = _sattn_finish(q_ref[0, heads, :], kn_ref[0, heads, :], vn_ref[0, heads, :],
                                            v_ref, s_ref)

    _ffn_step(x_ref, g_ref, wa_ref[...], wb_ref[...], wo_ref[...], gf_ref, o_ref, h_ref, final_norm,
              (scores, rest))


def _ffn(x, g, wa, wb, wo, g_final, *, tm, tf, final_norm, emit_bf16=False, side=None):
    m = x.shape[0]
    nf = D_FF // tf
    ni = m // tm
    b_off = nf if wb.shape[1] == 2 * D_FF else 0
    in_specs = [
        pl.BlockSpec((tm, D_MODEL), lambda i, j: (i, 0)),
        pl.BlockSpec((1, D_MODEL), lambda i, j: (0, 0)),
        pl.BlockSpec((D_MODEL, tf), lambda i, j: (0, j)),
        pl.BlockSpec((D_MODEL, tf), lambda i, j: (0, j + b_off)),
        pl.BlockSpec((tf, D_MODEL), lambda i, j: (j, 0)),
        pl.BlockSpec((1, D_MODEL), lambda i, j: (0, 0)),
    ]
    out_specs = [pl.BlockSpec((tm, D_MODEL), lambda i, j: (i, 0))]
    out_shape = [jax.ShapeDtypeStruct((m, D_MODEL), F32)]
    scratch = [pltpu.VMEM((tm, D_MODEL), BF16)]
    operands = (x, g, wa, wb, wo, g_final)
    if side is None:
        body = functools.partial(_ffn_kernel, final_norm=final_norm, emit_bf16=emit_bf16)
        sem = ("parallel", "arbitrary")
    if emit_bf16:
        assert m == tm and side is None
        half = pl.BlockSpec((D_MODEL, tf), lambda i, j: (0, j))
        out_specs += [half, half, pl.BlockSpec((tf, D_MODEL), lambda i, j: (j, 0))]
        out_shape += [jax.ShapeDtypeStruct((D_MODEL, D_FF), BF16),
                      jax.ShapeDtypeStruct((D_MODEL, D_FF), BF16),
                      jax.ShapeDtypeStruct((D_FF, D_MODEL), BF16)]
    if side is not None:
        q, kn, vn, cache_kt, cache_vt, cw_in, cw_out = side
        nb, _, _, wb_len = cache_kt.shape
        assert all(wb_len % d == 0 and w <= wb_len for w, d in DIL_PATTERNS)
        hs = SATT_HEADS_PER_STEP
        groups = H_ATT // hs
        n_blocks = nb * groups
        assert ni * nf >= n_blocks and D_MODEL % ni == 0 and (2 * D_FF) % nf == 0
        sblk = lambda i, j: jnp.minimum(i * nf + j, n_blocks - 1)
        seq_row = pl.BlockSpec((1, H_ATT, HD_ATT), lambda i, j: (sblk(i, j) // groups, 0, 0))
        cache = pl.BlockSpec((1, hs, HD_ATT, wb_len),
                             lambda i, j: (sblk(i, j) // groups, sblk(i, j) % groups, 0, 0))
        cwi = pl.BlockSpec((D_MODEL // ni, 2 * D_FF // nf), lambda i, j: (i, j))
        cwo = pl.BlockSpec((D_FF // nf, D_MODEL // ni), lambda i, j: (j, i))
        in_specs += [seq_row, seq_row, seq_row, cache, cache, cwi, cwo]
        out_specs += [seq_row, cwi, cwo]
        out_shape += [jax.ShapeDtypeStruct((nb, H_ATT, HD_ATT), F32),
                      jax.ShapeDtypeStruct(cw_in.shape, BF16),
                      jax.ShapeDtypeStruct(cw_out.shape, BF16)]
        scratch.append(pltpu.VMEM((hs, wb_len), F32))
        operands += (q, kn, vn, cache_kt, cache_vt, cw_in, cw_out)
        body = functools.partial(_ffn_side_kernel, final_norm=final_norm, n_side_blocks=n_blocks)
        sem = ("arbitrary", "arbitrary")
    outs = pl.pallas_call(
        body,
        grid=(ni, nf),
        in_specs=in_specs,
        out_specs=out_specs,
        out_shape=out_shape,
        scratch_shapes=scratch,
        compiler_params=_params(*sem),
        name="ffn",
    )(*operands)
    return outs if (emit_bf16 or side is not None) else outs[0]


def _proj_kernel(x_ref, g_ref, w_ref, wlr_ref, wg2_ref, bg2_ref, p_ref, la_ref, *rest, emit_bf16):
    h_ref = rest[-1]
    j = pl.program_id(1)

    @pl.when(j == 0)
    def _():
        h = (_rms(x_ref[...]) * g_ref[...]).astype(BF16)
        h_ref[...] = h
        lr = _dot_nt(h, wlr_ref[...])
        z = _dot(lr.astype(BF16), wg2_ref[...]) + bg2_ref[...]
        log_sig = jnp.minimum(z, 0.0) - jnp.log1p(jnp.exp(-jnp.abs(z)))
        la_ref[...] = log_sig * (1.0 / GATE_TAU)

    w = w_ref[...]
    if emit_bf16:
        w = w.astype(BF16)
        rest[0][...] = w
    p_ref[...] = _dot_nt(h_ref[...], w)


def _proj(x, g, w_t, w_lr_t, w_g2, b_g2, *, tm, tn, emit_bf16=False):
    m = x.shape[0]
    out_specs = [
        pl.BlockSpec((tm, tn), lambda i, j: (i, j)),
        pl.BlockSpec((tm, D_GLA_K), lambda i, j: (i, 0)),
    ]
    out_shape = [
        jax.ShapeDtypeStruct((m, D_PROJ_MAIN), F32),
        jax.ShapeDtypeStruct((m, D_GLA_K), F32),
    ]
    if emit_bf16:
        assert m == tm
        out_specs.append(pl.BlockSpec((tn, D_MODEL), lambda i, j: (j, 0)))
        out_shape.append(jax.ShapeDtypeStruct((D_PROJ_MAIN, D_MODEL), BF16))
    return pl.pallas_call(
        functools.partial(_proj_kernel, emit_bf16=emit_bf16),
        grid=(m // tm, D_PROJ_MAIN // tn),
        in_specs=[
            pl.BlockSpec((tm, D_MODEL), lambda i, j: (i, 0)),
            pl.BlockSpec((1, D_MODEL), lambda i, j: (0, 0)),
            pl.BlockSpec((tn, D_MODEL), lambda i, j: (j, 0)),
            pl.BlockSpec((GATE_RANK, D_MODEL), lambda i, j: (0, 0)),
            pl.BlockSpec((GATE_RANK, D_GLA_K), lambda i, j: (0, 0)),
            pl.BlockSpec((1, D_GLA_K), lambda i, j: (0, 0)),
        ],
        out_specs=out_specs,
        out_shape=out_shape,
        scratch_shapes=[pltpu.VMEM((tm, D_MODEL), BF16)],
        compiler_params=_params("parallel", "arbitrary"),
        name="proj",
    )(x, g, w_t, w_lr_t, w_g2, b_g2)


def _attn_kernel(q_ref, k_ref, v_ref, o_ref, kt_ref, vt_ref,
                 qp_ref, kp_ref, vp_ref, m_ref, l_ref, acc_ref, s_ref, ms_ref):
    blk = ATT_BLOCK
    for src_ref, dst_ref in ((k_ref, kt_ref), (v_ref, vt_ref)):
        t = src_ref[...].T
        dst_ref[0, 0] = t[:HD_ATT]
        dst_ref[0, 1] = t[HD_ATT:]

    scale = HD_ATT ** -0.5
    seq = q_ref.shape[0]
    ns = DIL_PATTERNS[-1][1]
    lane_lo = lax.broadcasted_iota(jnp.int32, (1, 2 * HD_ATT), 1) < HD_ATT
    neg = jnp.float32(-jnp.inf)

    assert ns == 16
    for src_ref, tmp_ref, dst_ref in ((q_ref, m_ref, qp_ref), (k_ref, l_ref, kp_ref), (v_ref, acc_ref, vp_ref)):
        for r4 in range(4):
            x = src_ref[pl.ds(r4, seq // 4, stride=4), :]
            tmp_ref[pl.ds(r4 * (seq // 4), seq // 4), :] = x * scale if src_ref is q_ref else x
        for r4 in range(4):
            for a in range(4):
                dst_ref[pl.ds((4 * a + r4) * blk, blk), :] = tmp_ref[pl.ds(r4 * (seq // 4) + a, blk, stride=4), :]

    def run_pattern(d, first_pattern):
        na = ns // d
        plen = blk // na
        per_stream = seq // (d * blk)

        def offset(idx):
            return na * (idx & (plen - 1)) + idx // plen

        qpos = offset(lax.broadcasted_iota(jnp.int32, (blk, blk), 0))
        kpos = offset(lax.broadcasted_iota(jnp.int32, (blk, blk), 1))
        bias_cur = jnp.where(kpos <= qpos, 0.0, neg)
        if per_stream > 1:
            bias_prev = jnp.where(kpos >= qpos, 0.0, neg)
            bias_band = jnp.concatenate([bias_prev, bias_cur], axis=1)
            bias_first = jnp.concatenate([jnp.full((blk, blk), neg, F32), bias_cur], axis=1)

        def pieces(rd, n):
            return [pl.ds((a * d + rd) * blk + plen * n, plen) for a in range(na)]

        def gather(ref, ps):
            return jnp.concatenate([ref[p, :] for p in ps], axis=0)

        nk = 2 * blk if per_stream > 1 else blk
        heads = (lane_lo, jnp.logical_not(lane_lo))

        def keys(ref, rd, n):
            cur = gather(ref, pieces(rd, n))
            if per_stream == 1:
                return cur.astype(BF16)
            prev = gather(ref, pieces(rd, max(n - 1, 0)))
            return jnp.concatenate([prev, cur], axis=0).astype(BF16)

        def group(g):
            blocks = [((g * ATT_GROUP + b) % d, (g * ATT_GROUP + b) // d) for b in range(ATT_GROUP)]
            for b, (rd, n) in enumerate(blocks):
                q = gather(qp_ref, pieces(rd, n))
                kk = keys(kp_ref, rd, n)
                bias = bias_cur if per_stream == 1 else (bias_band if n > 0 else bias_first)
                for h, sel in enumerate(heads):
                    qh = jnp.where(sel, q, 0.0).astype(BF16)
                    s_ref[b, h, :, :nk] = _dot_nt(qh, kk) + bias
            for b in range(ATT_GROUP):
                for h in range(2):
                    m = jnp.max(s_ref[b, h, :, :nk], axis=-1, keepdims=True)
                    ms_ref[b, h] = jnp.broadcast_to(m, (blk, blk))
            for b, (rd, n) in enumerate(blocks):
                vv = jnp.concatenate([keys(vp_ref, rd, n), jnp.ones((nk, blk), BF16)], axis=1)
                res = []
                for h in range(2):
                    mh = ms_ref[b, h]
                    mh = jnp.concatenate([mh, mh], axis=1) if nk == 2 * blk else mh
                    res.append(_dot(jnp.exp(s_ref[b, h, :, :nk] - mh).astype(BF16), vv))
                u = jnp.where(lane_lo, res[0][:, :blk], res[1][:, :blk])
                l = jnp.where(lane_lo, res[0][:, blk:], res[1][:, blk:])
                m = jnp.where(lane_lo, ms_ref[b, 0], ms_ref[b, 1])
                for a, p in enumerate(pieces(rd, n)):
                    sl = slice(a * plen, (a + 1) * plen)
                    if first_pattern:
                        m_ref[p, :] = m[sl]
                        l_ref[p, :] = l[sl]
                        acc_ref[p, :] = u[sl]
                    else:
                        m_old = m_ref[p, :]
                        m_new = jnp.maximum(m_old, m[sl])
                        a_old = jnp.exp(m_old - m_new)
                        a_blk = jnp.exp(m[sl] - m_new)
                        m_ref[p, :] = m_new
                        l_ref[p, :] = a_old * l_ref[p, :] + a_blk * l[sl]
                        acc_ref[p, :] = a_old * acc_ref[p, :] + a_blk * u[sl]

        for g in range(d * per_stream // ATT_GROUP):
            group(g)

    for idx, (_, d) in enumerate(DIL_PATTERNS):
        run_pattern(d, idx == 0)

    for r4 in range(4):
        for a in range(4):
            rows = pl.ds((4 * a + r4) * blk, blk)
            qp_ref[pl.ds(r4 * (seq // 4) + a, blk, stride=4), :] = acc_ref[rows, :] / l_ref[rows, :]
    for r4 in range(4):
        o_ref[pl.ds(r4, seq // 4, stride=4), :] = qp_ref[pl.ds(r4 * (seq // 4), seq // 4), :]


def _attn_prompt(p, batch, seq):
    ns = DIL_PATTERNS[-1][1]
    assert seq == ns * ATT_BLOCK
    assert all(w // d == ATT_BLOCK and ns % d == 0 and ATT_BLOCK * d // ns >= 8 for w, d in DIL_PATTERNS)
    lanes = 2 * HD_ATT
    n_pairs = H_ATT // 2
    return pl.pallas_call(
        _attn_kernel,
        grid=(batch, n_pairs),
        in_specs=[
            pl.BlockSpec((seq, lanes), lambda b, h: (b, COL_QA // lanes + h)),
            pl.BlockSpec((seq, lanes), lambda b, h: (b, COL_KA // lanes + h)),
            pl.BlockSpec((seq, lanes), lambda b, h: (b, COL_VA // lanes + h)),
        ],
        out_specs=[
            pl.BlockSpec((seq, lanes), lambda b, h: (b, h)),
            pl.BlockSpec((1, 2, HD_ATT, seq), lambda b, h: (b, h, 0, 0)),
            pl.BlockSpec((1, 2, HD_ATT, seq), lambda b, h: (b, h, 0, 0)),
        ],
        out_shape=[
            jax.ShapeDtypeStruct((batch * seq, D_ATT), F32),
            jax.ShapeDtypeStruct((batch, H_ATT, HD_ATT, seq), F32),
            jax.ShapeDtypeStruct((batch, H_ATT, HD_ATT, seq), F32),
        ],
        scratch_shapes=[pltpu.VMEM((seq, lanes), F32)] * 6 + [
            pltpu.VMEM((ATT_GROUP, 2, ATT_BLOCK, 2 * ATT_BLOCK), F32),
            pltpu.VMEM((ATT_GROUP, 2, ATT_BLOCK, lanes), F32),
        ],
        compiler_params=_params("parallel", "parallel"),
        name="attn_prompt",
    )(p, p, p)


_GLA_LEVELS = (32, 16, 8, 4, 2, 1)


def _gla_exponent_matrix():
    c = GLA_CHUNK
    t = np.arange(c)[:, None]
    u = np.arange(c)[None, :]
    mats = [(u <= t), (u > t)]
    for h in _GLA_LEVELS:
        mid = (t // (2 * h)) * (2 * h) + h - 1
        upper = (t % (2 * h)) >= h
        mats.append(np.where(upper, (u > mid) & (u <= t), (u > t) & (u <= mid)))
    tmat = np.concatenate(mats, axis=0).astype(np.float32)
    return np.concatenate([tmat, tmat], axis=1)


def _gla_kernel(q_ref, k_ref, v_ref, g_ref, t_ref, o_ref, s_ref, st_ref):
    c = GLA_CHUNK
    scale = DK_GLA ** -0.5
    ti = lax.broadcasted_iota(jnp.int32, (c, c), 0)
    si = lax.broadcasted_iota(jnp.int32, (c, c), 1)
    txs = ti ^ si
    below = ti > si
    level_masks = [below & (txs >= h) & (txs < 2 * h) for h in _GLA_LEVELS]
    diag = ti == si

    st_ref[...] = jnp.zeros_like(st_ref)

    def chunk(ci, carry):
        r = pl.ds(pl.multiple_of(ci * c, c), c)
        g = g_ref[r, :]
        g_hi = g.astype(BF16)
        g_lo = (g - g_hi.astype(F32)).astype(BF16)
        f = jnp.exp(_dot(t_ref[...], jnp.concatenate([g_hi, g_lo], axis=0)))
        q = q_ref[r, :] * scale
        k = k_ref[r, :]
        v = v_ref[r, :].astype(BF16)
        a = jnp.where(diag, _dot_nt(q.astype(BF16), k.astype(BF16)), 0.0)
        for lvl in range(len(_GLA_LEVELS)):
            fl = f[2 * c + lvl * c: 3 * c + lvl * c]
            a = a + jnp.where(level_masks[lvl],
                              _dot_nt((q * fl).astype(BF16), (k * fl).astype(BF16)), 0.0)
        st = st_ref[...]
        qe = (q * f[0:c]).astype(BF16)
        o_ref[r, :] = _dot_nt(qe, st.astype(BF16)) + _dot(a.astype(BF16), v)
        kd = (k * f[c:2 * c]).astype(BF16)
        st_ref[...] = st * f[c - 1:c] + _dot_tn(v, kd)
        return carry

    lax.fori_loop(0, q_ref.shape[0] // c, chunk, 0, unroll=8)
    s_ref[0, 0] = st_ref[...].T


def _gla_prompt(p, la, tmat, batch, seq):
    return pl.pallas_call(
        _gla_kernel,
        grid=(batch, H_GLA),
        in_specs=[
            pl.BlockSpec((seq, DK_GLA), lambda b, h: (b, COL_QG // DK_GLA + h)),
            pl.BlockSpec((seq, DK_GLA), lambda b, h: (b, COL_KG // DK_GLA + h)),
            pl.BlockSpec((seq, DV_GLA), lambda b, h: (b, COL_VG // DV_GLA + h)),
            pl.BlockSpec((seq, DK_GLA), lambda b, h: (b, h)),
            pl.BlockSpec(tmat.shape, lambda b, h: (0, 0)),
        ],
        out_specs=[
            pl.BlockSpec((seq, DV_GLA), lambda b, h: (b, h)),
            pl.BlockSpec((1, 1, DK_GLA, DV_GLA), lambda b, h: (b, h, 0, 0)),
        ],
        out_shape=[
            jax.ShapeDtypeStruct((batch * seq, D_GLA_V), F32),
            jax.ShapeDtypeStruct((batch, H_GLA, DK_GLA, DV_GLA), F32),
        ],
        scratch_shapes=[pltpu.VMEM((DV_GLA, DK_GLA), F32)],
        compiler_params=_params("parallel", "parallel"),
        name="gla_prompt",
    )(p, p, p, la, tmat)


def _merge_kernel(x_ref, oa_ref, og_ref, rg_ref, ga_ref, gg_ref, w_ref, o_ref):
    a = (_rms(oa_ref[...]) * ga_ref[...]).astype(BF16)
    parts = []
    for h in range(H_GLA):
        cols = slice(h * DV_GLA, (h + 1) * DV_GLA)
        r = rg_ref[:, cols]
        parts.append((_rms(og_ref[:, cols]) * gg_ref[...] * (r * jax.nn.sigmoid(r))).astype(BF16))
    gg = jnp.concatenate(parts, axis=-1)
    o_ref[...] = x_ref[...] + _dot(a, w_ref[0:D_ATT, :]) + _dot(gg, w_ref[D_ATT:, :])


def _merge(x, o_att, o_gla, p, g_att, g_gla, w_out, *, tm):
    m = x.shape[0]
    return pl.pallas_call(
        _merge_kernel,
        grid=(m // tm,),
        in_specs=[
            pl.BlockSpec((tm, D_MODEL), lambda i: (i, 0)),
            pl.BlockSpec((tm, D_ATT), lambda i: (i, 0)),
            pl.BlockSpec((tm, D_GLA_V), lambda i: (i, 0)),
            pl.BlockSpec((tm, D_GLA_V), lambda i: (i, COL_RG // D_GLA_V)),
            pl.BlockSpec((1, D_ATT), lambda i: (0, 0)),
            pl.BlockSpec((1, DV_GLA), lambda i: (0, 0)),
            pl.BlockSpec((D_MODEL, D_MODEL), lambda i: (0, 0)),
        ],
        out_specs=pl.BlockSpec((tm, D_MODEL), lambda i: (i, 0)),
        out_shape=jax.ShapeDtypeStruct((m, D_MODEL), F32),
        compiler_params=_params("parallel"),
        name="merge",
    )(x, o_att, o_gla, p, g_att, g_gla, w_out)


SATT_HEADS_PER_STEP = 4


def _sattn_scores(q, k_ref, s_ref):
    q16 = (q * HD_ATT ** -0.5).astype(BF16)
    for h in range(k_ref.shape[1]):
        s_ref[h:h + 1, :] = _dot(q16, k_ref[0, h].astype(BF16))[h:h + 1, :]


def _sattn_finish(q, kn, vn, v_ref, s_ref):
    scale = HD_ATT ** -0.5
    wb = v_ref.shape[-1]
    t = lax.broadcasted_iota(jnp.int32, (1, wb), 1)
    cnt = jnp.zeros((1, wb), F32)
    for w, d in DIL_PATTERNS:
        cnt = cnt + jnp.where((t >= wb - w) & ((t & (d - 1)) == 0), 1.0, 0.0)
    bias = jnp.where(cnt > 0.0, 0.0, jnp.float32(-jnp.inf))
    n_pat = float(len(DIL_PATTERNS))
    n_heads = v_ref.shape[1]
    s_new = jnp.sum(q * scale * kn, axis=-1, keepdims=True)
    s = s_ref[...] + bias
    m = jnp.maximum(jnp.max(s, axis=-1, keepdims=True), s_new)
    e = cnt * jnp.exp(s - m)
    e_new = n_pat * jnp.exp(s_new - m)
    den = jnp.sum(e, axis=-1, keepdims=True) + e_new
    assert n_heads <= 8
    lane = lax.broadcasted_iota(jnp.int32, (HD_ATT, 128), 1)
    cols = jnp.zeros((HD_ATT, 128), F32)
    for h in range(n_heads):
        col = jnp.sum(v_ref[0, h] * e[h:h + 1, :], axis=-1, keepdims=True)
        for r, piece in enumerate(_bf16_pieces(col)):
            cols = jnp.where(lane == 8 * r + h, piece.astype(F32), cols)
    eye = (lax.broadcasted_iota(jnp.int32, (HD_ATT, HD_ATT), 0)
           == lax.broadcasted_iota(jnp.int32, (HD_ATT, HD_ATT), 1))
    rows = _dot_tn(cols.astype(BF16), jnp.where(eye, 1.0, 0.0).astype(BF16))
    pv = sum(rows[8 * r:8 * r + n_heads] for r in range(_SPLIT))
    return (pv + e_new * vn) / den


SGLA_SEQS_PER_STEP = 4
_SPLIT = 3


def _bf16_pieces(x):
    pieces = []
    for _ in range(_SPLIT):
        p = x.astype(BF16)
        pieces.append(p)
        x = x - p.astype(F32)
    return pieces


def _sgla_kernel(q_ref, k_ref, g_ref, v_ref, s_ref, so_ref, o_ref):
    scale = DK_GLA ** -0.5
    n_vec = 3
    sel_rows = 16
    r = lax.broadcasted_iota(jnp.int32, (sel_rows, n_vec * DV_GLA), 0)
    c = lax.broadcasted_iota(jnp.int32, (sel_rows, n_vec * DV_GLA), 1)
    selector = jnp.where((r // _SPLIT == c // DV_GLA) & (r < n_vec * _SPLIT), 1.0, 0.0).astype(BF16)
    pad = jnp.zeros((sel_rows - n_vec * _SPLIT, DK_GLA), BF16)
    for s in range(q_ref.shape[0]):
        for h in range(H_GLA):
            hrow = slice(h, h + 1)
            vecs = (jnp.exp(g_ref[s, hrow, :]), k_ref[s, hrow, :], q_ref[s, hrow, :] * scale)
            lhs = jnp.concatenate([p for x in vecs for p in _bf16_pieces(x)] + [pad], axis=0)
            cols = _dot_tn(lhs, selector)
            decay, kcol, qcol = (cols[:, i * DV_GLA:(i + 1) * DV_GLA] for i in range(n_vec))
            s_new = decay * s_ref[s, h] + kcol * v_ref[s, hrow, :]
            so_ref[s, h] = s_new
            o_ref[s, hrow, :] = jnp.sum(qcol * s_new, axis=0, keepdims=True)


def _gla_sample(q, k, g, v, state):
    nb = state.shape[0]
    bs = SGLA_SEQS_PER_STEP
    krow = pl.BlockSpec((bs, H_GLA, DK_GLA), lambda b: (b, 0, 0))
    vrow = pl.BlockSpec((bs, H_GLA, DV_GLA), lambda b: (b, 0, 0))
    st = pl.BlockSpec((bs, H_GLA, DK_GLA, DV_GLA), lambda b: (b, 0, 0, 0))
    return pl.pallas_call(
        _sgla_kernel,
        grid=(nb // bs,),
        in_specs=[krow, krow, krow, vrow, st],
        out_specs=[st, vrow],
        out_shape=[jax.ShapeDtypeStruct((nb, H_GLA, DK_GLA, DV_GLA), F32),
                   jax.ShapeDtypeStruct((nb, H_GLA, DV_GLA), F32)],
        compiler_params=_params("parallel"),
        name="gla_sample",
    )(q, k, g, v, state)


def kernel(x_prompt, x_sample, cache_att_k, cache_att_v, state_gla, g_ffn1, w_ffn1_in, w_ffn1_out, g_mix, w_in, w_gate2, b_gate2, g_att_out, g_gla_out, w_out, g_ffn2, w_ffn2_in, w_ffn2_out, g_final):
    depth = w_in.shape[0]
    assert depth == 1
    batch, seq, _ = x_prompt.shape
    nb, dec_seq, _ = x_sample.shape
    assert dec_seq == 1
    xp = x_prompt.reshape(batch * seq, D_MODEL)
    xs = x_sample.reshape(nb, D_MODEL)
    row = lambda a: a.reshape(1, -1)
    l = 0

    w_in_t = w_in[l].T
    w_lr = w_in_t[D_PROJ_MAIN:].astype(BF16)
    w_g2 = w_gate2[l].astype(BF16)
    b_g2 = row(b_gate2[l])
    w_o = w_out[l].astype(BF16)
    gf = row(g_final)
    tmat = jnp.asarray(_gla_exponent_matrix(), dtype=BF16)

    xs, w1a, w1b, w1o = _ffn(xs, row(g_ffn1[l]), w_ffn1_in[l], w_ffn1_in[l], w_ffn1_out[l], gf,
                             tm=nb, tf=512, final_norm=False, emit_bf16=True)
    ps, las, w_main = _proj(xs, row(g_mix[l]), w_in_t, w_lr, w_g2, b_g2, tm=nb, tn=512, emit_bf16=True)
    heads = lambda c0: ps[:, c0:c0 + D_ATT].reshape(nb, H_ATT, HD_ATT)
    k_new, v_new = heads(COL_KA), heads(COL_VA)
    xp, o_att_s, w2i, w2o = _ffn(
        xp, row(g_ffn1[l]), w1a, w1b, w1o, gf, tm=FFN_ROWS, tf=FFN_TILE, final_norm=False,
        side=(heads(COL_QA), k_new, v_new,
              jnp.transpose(cache_att_k[l], (0, 2, 3, 1)), jnp.transpose(cache_att_v[l], (0, 2, 3, 1)),
              w_ffn2_in[l], w_ffn2_out[l]))
    krows = lambda a: a.reshape(nb, H_GLA, DK_GLA)
    s_new, o_gla_s = _gla_sample(
        krows(ps[:, COL_QG:COL_QG + D_GLA_K]), krows(ps[:, COL_KG:COL_KG + D_GLA_K]), krows(las),
        ps[:, COL_VG:COL_VG + D_GLA_V].reshape(nb, H_GLA, DV_GLA), state_gla[l])
    xs = _merge(xs, o_att_s.reshape(nb, D_ATT), o_gla_s.reshape(nb, D_GLA_V), ps,
                row(g_att_out[l]), row(g_gla_out[l]), w_o, tm=nb)
    ys = _ffn(xs, row(g_ffn2[l]), w2i, w2i, w2o, gf, tm=nb, tf=FFN_TILE, final_norm=True)
    nk_s = k_new.reshape(1, nb, 1, H_ATT, HD_ATT)
    nv_s = v_new.reshape(1, nb, 1, H_ATT, HD_ATT)

    pp, lap = _proj(xp, row(g_mix[l]), w_main, w_lr, w_g2, b_g2, tm=1024, tn=1024)
    o_att, kt_p, vt_p = _attn_prompt(pp, batch, seq)
    o_gla, s_fin = _gla_prompt(pp, lap, tmat, batch, seq)
    xp = _merge(xp, o_att, o_gla, pp, row(g_att_out[l]), row(g_gla_out[l]), w_o, tm=512)
    yp = _ffn(xp, row(g_ffn2[l]), w2i, w2i, w2o, gf, tm=FFN_ROWS, tf=FFN_TILE, final_norm=True)
    nk_p = jnp.transpose(kt_p, (0, 3, 1, 2))[None]
    nv_p = jnp.transpose(vt_p, (0, 3, 1, 2))[None]

    return (yp.reshape(batch, seq, D_MODEL), ys.reshape(nb, 1, D_MODEL), nk_p, nv_p,
            s_fin[None], nk_s, nv_s, s_new[None])
```

```python
import functools

import numpy as np
import jax
import jax.numpy as jnp
from jax import lax
from jax.experimental import pallas as pl
from jax.experimental.pallas import tpu as pltpu

F32 = jnp.float32
BF16 = jnp.bfloat16

D_MODEL = 2048
D_FF = 5632
D_ATT = 1024
HD_ATT = 64
H_ATT = 16
H_GLA = 4
DK_GLA = 128
DV_GLA = 256
D_GLA_K = H_GLA * DK_GLA
D_GLA_V = H_GLA * DV_GLA
GATE_RANK = 16
GATE_TAU = 16.0
NORM_EPS = 1e-6
DIL_PATTERNS = ((128, 1), (512, 4), (2048, 16))
ATT_BLOCK = 128
ATT_GROUP = 16
GLA_CHUNK = 64
FFN_ROWS, FFN_TILE = 512, 512
FFN_ROWS_SIDE, FFN_TILE_SIDE = 1024, 256
FFN_ROW_CHUNK = 512
D_PROJ_MAIN = 3 * D_ATT + 2 * D_GLA_K + 2 * D_GLA_V

COL_QA, COL_KA, COL_VA = 0, D_ATT, 2 * D_ATT
COL_QG = 3 * D_ATT
COL_KG = COL_QG + D_GLA_K
COL_VG = COL_KG + D_GLA_K
COL_RG = COL_VG + D_GLA_V

VMEM_LIMIT_BYTES = 58 * 1024 * 1024


def _rms(x):
    return x * lax.rsqrt(jnp.mean(x * x, axis=-1, keepdims=True) + NORM_EPS)


def _dot(a, b):
    return jnp.dot(a, b, preferred_element_type=F32)


def _dot_nt(a, b):
    return lax.dot_general(a, b, (((1,), (1,)), ((), ())), preferred_element_type=F32)


def _dot_tn(a, b):
    return lax.dot_general(a, b, (((0,), (0,)), ((), ())), preferred_element_type=F32)


def _params(*sem):
    return pltpu.CompilerParams(dimension_semantics=sem, vmem_limit_bytes=VMEM_LIMIT_BYTES)


def _ffn_step(x_ref, g_ref, wa, wb, wo, gf_ref, o_ref, h_ref, final_norm, side_jobs=(None, None)):
    j = pl.program_id(1)

    @pl.when(j == 0)
    def _():
        h_ref[...] = (_rms(x_ref[...]) * g_ref[...]).astype(BF16)
        o_ref[...] = jnp.zeros_like(o_ref)

    tm = h_ref.shape[0]
    rc = min(tm, FFN_ROW_CHUNK)
    for c in range(tm // rc):
        rows = pl.ds(c * rc, rc)
        h = h_ref[rows, :]
        a = _dot(h, wa)
        if c == 0 and side_jobs[0] is not None:
            side_jobs[0]()
        b = _dot(h, wb)
        act = (a * jax.nn.sigmoid(a) * b).astype(BF16)
        o_ref[rows, :] += _dot(act, wo)
    if side_jobs[1] is not None:
        side_jobs[1]()

    @pl.when(j == pl.num_programs(1) - 1)
    def _():
        y = x_ref[...] + 0.5 * o_ref[...]
        if final_norm:
            y = _rms(y) * gf_ref[...]
        o_ref[...] = y


def _ffn_kernel(x_ref, g_ref, wa_ref, wb_ref, wo_ref, gf_ref, o_ref, *rest, final_norm, emit_bf16):
    wa, wb, wo = wa_ref[...], wb_ref[...], wo_ref[...]
    if emit_bf16:
        wa, wb, wo = wa.astype(BF16), wb.astype(BF16), wo.astype(BF16)
        for dst_ref, w in zip(rest[:3], (wa, wb, wo)):
            dst_ref[...] = w
    _ffn_step(x_ref, g_ref, wa, wb, wo, gf_ref, o_ref, rest[-1], final_norm)


def _ffn_side_kernel(x_ref, g_ref, wa_ref, wb_ref, wo_ref, gf_ref,
                     q_ref, kn_ref, vn_ref, k_ref, v_ref, cwa_ref, cwb_ref, cwo_ref,
                     o_ref, so_ref, cwa16_ref, cwb16_ref, cwo16_ref, h_ref, s_ref,
                     *, final_norm, n_side_blocks):
    hs = k_ref.shape[1]
    groups = H_ATT // hs
    step = pl.program_id(0) * pl.num_programs(1) + pl.program_id(1)
    blk = jnp.minimum(step, n_side_blocks - 1)
    heads = pl.ds(pl.multiple_of((blk % groups) * hs, hs), hs)

    def scores():
        _sattn_scores(q_ref[0, heads, :], k_ref, s_ref)
        for src_ref, dst_ref in ((cwa_ref, cwa16_ref), (cwb_ref, cwb16_ref), (cwo_ref, cwo16_ref)):
            dst_ref[...] = src_ref[...].astype(BF16)

    def rest():
        so_ref[0, heads, :] = _sattn_finish(q_ref[0, heads, :], kn_ref[0, heads, :], vn_ref[0, heads, :],
                                            v_ref, s_ref)

    _ffn_step(x_ref, g_ref, wa_ref[...], wb_ref[...], wo_ref[...], gf_ref, o_ref, h_ref, final_norm,
              (scores, rest))


def _ffn(x, g, wa, wb, wo, g_final, *, tm, tf, final_norm, emit_bf16=False, side=None):
    m = x.shape[0]
    nf = D_FF // tf
    ni = m // tm
    b_off = nf if wb.shape[1] == 2 * D_FF else 0
    in_specs = [
        pl.BlockSpec((tm, D_MODEL), lambda i, j: (i, 0)),
        pl.BlockSpec((1, D_MODEL), lambda i, j: (0, 0)),
        pl.BlockSpec((D_MODEL, tf), lambda i, j: (0, j)),
        pl.BlockSpec((D_MODEL, tf), lambda i, j: (0, j + b_off)),
        pl.BlockSpec((tf, D_MODEL), lambda i, j: (j, 0)),
        pl.BlockSpec((1, D_MODEL), lambda i, j: (0, 0)),
    ]
    out_specs = [pl.BlockSpec((tm, D_MODEL), lambda i, j: (i, 0))]
    out_shape = [jax.ShapeDtypeStruct((m, D_MODEL), F32)]
    scratch = [pltpu.VMEM((tm, D_MODEL), BF16)]
    operands = (x, g, wa, wb, wo, g_final)
    if side is None:
        body = functools.partial(_ffn_kernel, final_norm=final_norm, emit_bf16=emit_bf16)
        sem = ("parallel", "arbitrary")
    if emit_bf16:
        assert m == tm and side is None
        half = pl.BlockSpec((D_MODEL, tf), lambda i, j: (0, j))
        out_specs += [half, half, pl.BlockSpec((tf, D_MODEL), lambda i, j: (j, 0))]
        out_shape += [jax.ShapeDtypeStruct((D_MODEL, D_FF), BF16),
                      jax.ShapeDtypeStruct((D_MODEL, D_FF), BF16),
                      jax.ShapeDtypeStruct((D_FF, D_MODEL), BF16)]
    if side is not None:
        q, kn, vn, cache_kt, cache_vt, cw_in, cw_out = side
        nb, _, _, wb_len = cache_kt.shape
        assert all(wb_len % d == 0 and w <= wb_len for w, d in DIL_PATTERNS)
        hs = SATT_HEADS_PER_STEP
        groups = H_ATT // hs
        n_blocks = nb * groups
        assert ni * nf >= n_blocks and D_MODEL % ni == 0
        sblk = lambda i, j: jnp.minimum(i * nf + j, n_blocks - 1)
        seq_row = pl.BlockSpec((1, H_ATT, HD_ATT), lambda i, j: (sblk(i, j) // groups, 0, 0))
        cache = pl.BlockSpec((1, hs, HD_ATT, wb_len),
                             lambda i, j: (sblk(i, j) // groups, sblk(i, j) % groups, 0, 0))
        cwa = pl.BlockSpec((D_MODEL // ni, tf), lambda i, j: (i, j))
        cwb = pl.BlockSpec((D_MODEL // ni, tf), lambda i, j: (i, j + nf))
        cwo = pl.BlockSpec((tf, D_MODEL // ni), lambda i, j: (j, i))
        in_specs += [seq_row, seq_row, seq_row, cache, cache, cwa, cwb, cwo]
        out_specs += [seq_row, cwa, cwa, cwo]
        out_shape += [jax.ShapeDtypeStruct((nb, H_ATT, HD_ATT), F32),
                      jax.ShapeDtypeStruct((D_MODEL, D_FF), BF16),
                      jax.ShapeDtypeStruct((D_MODEL, D_FF), BF16),
                      jax.ShapeDtypeStruct(cw_out.shape, BF16)]
        scratch.append(pltpu.VMEM((hs, wb_len), F32))
        operands += (q, kn, vn, cache_kt, cache_vt, cw_in, cw_in, cw_out)
        body = functools.partial(_ffn_side_kernel, final_norm=final_norm, n_side_blocks=n_blocks)
        sem = ("arbitrary", "arbitrary")
    outs = pl.pallas_call(
        body,
        grid=(ni, nf),
        in_specs=in_specs,
        out_specs=out_specs,
        out_shape=out_shape,
        scratch_shapes=scratch,
        compiler_params=_params(*sem),
        name="ffn",
    )(*operands)
    return outs if (emit_bf16 or side is not None) else outs[0]


def _proj_kernel(x_ref, g_ref, w_ref, wlr_ref, wg2_ref, bg2_ref, p_ref, la_ref, *rest, emit_bf16):
    h_ref = rest[-1]
    j = pl.program_id(1)

    @pl.when(j == 0)
    def _():
        h = (_rms(x_ref[...]) * g_ref[...]).astype(BF16)
        h_ref[...] = h
        lr = _dot_nt(h, wlr_ref[...])
        z = _dot(lr.astype(BF16), wg2_ref[...]) + bg2_ref[...]
        log_sig = jnp.minimum(z, 0.0) - jnp.log1p(jnp.exp(-jnp.abs(z)))
        la_ref[...] = log_sig * (1.0 / GATE_TAU)

    w = w_ref[...]
    if emit_bf16:
        w = w.astype(BF16)
        rest[0][...] = w
    p_ref[...] = _dot_nt(h_ref[...], w)


def _proj(x, g, w_t, w_lr_t, w_g2, b_g2, *, tm, tn, emit_bf16=False):
    m = x.shape[0]
    out_specs = [
        pl.BlockSpec((tm, tn), lambda i, j: (i, j)),
        pl.BlockSpec((tm, D_GLA_K), lambda i, j: (i, 0)),
    ]
    out_shape = [
        jax.ShapeDtypeStruct((m, D_PROJ_MAIN), F32),
        jax.ShapeDtypeStruct((m, D_GLA_K), F32),
    ]
    if emit_bf16:
        assert m == tm
        out_specs.append(pl.BlockSpec((tn, D_MODEL), lambda i, j: (j, 0)))
        out_shape.append(jax.ShapeDtypeStruct((D_PROJ_MAIN, D_MODEL), BF16))
    return pl.pallas_call(
        functools.partial(_proj_kernel, emit_bf16=emit_bf16),
        grid=(m // tm, D_PROJ_MAIN // tn),
        in_specs=[
            pl.BlockSpec((tm, D_MODEL), lambda i, j: (i, 0)),
            pl.BlockSpec((1, D_MODEL), lambda i, j: (0, 0)),
            pl.BlockSpec((tn, D_MODEL), lambda i, j: (j, 0)),
            pl.BlockSpec((GATE_RANK, D_MODEL), lambda i, j: (0, 0)),
            pl.BlockSpec((GATE_RANK, D_GLA_K), lambda i, j: (0, 0)),
            pl.BlockSpec((1, D_GLA_K), lambda i, j: (0, 0)),
        ],
        out_specs=out_specs,
        out_shape=out_shape,
        scratch_shapes=[pltpu.VMEM((tm, D_MODEL), BF16)],
        compiler_params=_params("parallel", "arbitrary"),
        name="proj",
    )(x, g, w_t, w_lr_t, w_g2, b_g2)


def _attn_kernel(q_ref, k_ref, v_ref, o_ref, kt_ref, vt_ref,
                 qp_ref, kp_ref, vp_ref, m_ref, l_ref, acc_ref, s_ref, ms_ref):
    blk = ATT_BLOCK
    for src_ref, dst_ref in ((k_ref, kt_ref), (v_ref, vt_ref)):
        t = src_ref[...].T
        dst_ref[0, 0] = t[:HD_ATT]
        dst_ref[0, 1] = t[HD_ATT:]

    scale = HD_ATT ** -0.5
    seq = q_ref.shape[0]
    ns = DIL_PATTERNS[-1][1]
    lane_lo = lax.broadcasted_iota(jnp.int32, (1, 2 * HD_ATT), 1) < HD_ATT
    neg = jnp.float32(-jnp.inf)

    assert ns == 16
    for src_ref, tmp_ref, dst_ref in ((q_ref, m_ref, qp_ref), (k_ref, l_ref, kp_ref), (v_ref, acc_ref, vp_ref)):
        for r4 in range(4):
            x = src_ref[pl.ds(r4, seq // 4, stride=4), :]
            tmp_ref[pl.ds(r4 * (seq // 4), seq // 4), :] = x * scale if src_ref is q_ref else x
        for r4 in range(4):
            for a in range(4):
                dst_ref[pl.ds((4 * a + r4) * blk, blk), :] = tmp_ref[pl.ds(r4 * (seq // 4) + a, blk, stride=4), :]

    def run_pattern(d, first_pattern):
        na = ns // d
        plen = blk // na
        per_stream = seq // (d * blk)

        def offset(idx):
            return na * (idx & (plen - 1)) + idx // plen

        qpos = offset(lax.broadcasted_iota(jnp.int32, (blk, blk), 0))
        kpos = offset(lax.broadcasted_iota(jnp.int32, (blk, blk), 1))
        bias_cur = jnp.where(kpos <= qpos, 0.0, neg)
        if per_stream > 1:
            bias_prev = jnp.where(kpos >= qpos, 0.0, neg)
            bias_band = jnp.concatenate([bias_prev, bias_cur], axis=1)
            bias_first = jnp.concatenate([jnp.full((blk, blk), neg, F32), bias_cur], axis=1)

        def pieces(rd, n):
            return [pl.ds((a * d + rd) * blk + plen * n, plen) for a in range(na)]

        def gather(ref, ps):
            return jnp.concatenate([ref[p, :] for p in ps], axis=0)

        nk = 2 * blk if per_stream > 1 else blk
        heads = (lane_lo, jnp.logical_not(lane_lo))

        def keys(ref, rd, n):
            cur = gather(ref, pieces(rd, n))
            if per_stream == 1:
                return cur.astype(BF16)
            prev = gather(ref, pieces(rd, max(n - 1, 0)))
            return jnp.concatenate([prev, cur], axis=0).astype(BF16)

        def group(g):
            blocks = [((g * ATT_GROUP + b) % d, (g * ATT_GROUP + b) // d) for b in range(ATT_GROUP)]
            for b, (rd, n) in enumerate(blocks):
                q = gather(qp_ref, pieces(rd, n))
                kk = keys(kp_ref, rd, n)
                bias = bias_cur if per_stream == 1 else (bias_band if n > 0 else bias_first)
                for h, sel in enumerate(heads):
                    qh = jnp.where(sel, q, 0.0).astype(BF16)
                    s_ref[b, h, :, :nk] = _dot_nt(qh, kk) + bias
            for b in range(ATT_GROUP):
                for h in range(2):
                    m = jnp.max(s_ref[b, h, :, :nk], axis=-1, keepdims=True)
                    ms_ref[b, h] = jnp.broadcast_to(m, (blk, blk))
            for b, (rd, n) in enumerate(blocks):
                vv = jnp.concatenate([keys(vp_ref, rd, n), jnp.ones((nk, blk), BF16)], axis=1)
                res = []
                for h in range(2):
                    mh = ms_ref[b, h]
                    mh = jnp.concatenate([mh, mh], axis=1) if nk == 2 * blk else mh
                    res.append(_dot(jnp.exp(s_ref[b, h, :, :nk] - mh).astype(BF16), vv))
                u = jnp.where(lane_lo, res[0][:, :blk], res[1][:, :blk])
                l = jnp.where(lane_lo, res[0][:, blk:], res[1][:, blk:])
                m = jnp.where(lane_lo, ms_ref[b, 0], ms_ref[b, 1])
                for a, p in enumerate(pieces(rd, n)):
                    sl = slice(a * plen, (a + 1) * plen)
                    if first_pattern:
                        m_ref[p, :] = m[sl]
                        l_ref[p, :] = l[sl]
                        acc_ref[p, :] = u[sl]
                    else:
                        m_old = m_ref[p, :]
                        m_new = jnp.maximum(m_old, m[sl])
                        a_old = jnp.exp(m_old - m_new)
                        a_blk = jnp.exp(m[sl] - m_new)
                        m_ref[p, :] = m_new
                        l_ref[p, :] = a_old * l_ref[p, :] + a_blk * l[sl]
                        acc_ref[p, :] = a_old * acc_ref[p, :] + a_blk * u[sl]

        for g in range(d * per_stream // ATT_GROUP):
            group(g)

    for idx, (_, d) in enumerate(DIL_PATTERNS):
        run_pattern(d, idx == 0)

    for r4 in range(4):
        for a in range(4):
            rows = pl.ds((4 * a + r4) * blk, blk)
            qp_ref[pl.ds(r4 * (seq // 4) + a, blk, stride=4), :] = acc_ref[rows, :] / l_ref[rows, :]
    for r4 in range(4):
        o_ref[pl.ds(r4, seq // 4, stride=4), :] = qp_ref[pl.ds(r4 * (seq // 4), seq // 4), :]


def _attn_prompt(p, batch, seq):
    ns = DIL_PATTERNS[-1][1]
    assert seq == ns * ATT_BLOCK
    assert all(w // d == ATT_BLOCK and ns % d == 0 and ATT_BLOCK * d // ns >= 8 for w, d in DIL_PATTERNS)
    lanes = 2 * HD_ATT
    n_pairs = H_ATT // 2
    return pl.pallas_call(
        _attn_kernel,
        grid=(batch, n_pairs),
        in_specs=[
            pl.BlockSpec((seq, lanes), lambda b, h: (b, COL_QA // lanes + h)),
            pl.BlockSpec((seq, lanes), lambda b, h: (b, COL_KA // lanes + h)),
            pl.BlockSpec((seq, lanes), lambda b, h: (b, COL_VA // lanes + h)),
        ],
        out_specs=[
            pl.BlockSpec((seq, lanes), lambda b, h: (b, h)),
            pl.BlockSpec((1, 2, HD_ATT, seq), lambda b, h: (b, h, 0, 0)),
            pl.BlockSpec((1, 2, HD_ATT, seq), lambda b, h: (b, h, 0, 0)),
        ],
        out_shape=[
            jax.ShapeDtypeStruct((batch * seq, D_ATT), F32),
            jax.ShapeDtypeStruct((batch, H_ATT, HD_ATT, seq), F32),
            jax.ShapeDtypeStruct((batch, H_ATT, HD_ATT, seq), F32),
        ],
        scratch_shapes=[pltpu.VMEM((seq, lanes), F32)] * 6 + [
            pltpu.VMEM((ATT_GROUP, 2, ATT_BLOCK, 2 * ATT_BLOCK), F32),
            pltpu.VMEM((ATT_GROUP, 2, ATT_BLOCK, lanes), F32),
        ],
        compiler_params=_params("parallel", "parallel"),
        name="attn_prompt",
    )(p, p, p)


_GLA_LEVELS = (32, 16, 8, 4, 2, 1)


def _gla_exponent_matrix():
    c = GLA_CHUNK
    t = np.arange(c)[:, None]
    u = np.arange(c)[None, :]
    mats = [(u <= t), (u > t)]
    for h in _GLA_LEVELS:
        mid = (t // (2 * h)) * (2 * h) + h - 1
        upper = (t % (2 * h)) >= h
        mats.append(np.where(upper, (u > mid) & (u <= t), (u > t) & (u <= mid)))
    tmat = np.concatenate(mats, axis=0).astype(np.float32)
    return np.concatenate([tmat, tmat], axis=1)


def _gla_kernel(q_ref, k_ref, v_ref, g_ref, t_ref, o_ref, s_ref, st_ref):
    c = GLA_CHUNK
    scale = DK_GLA ** -0.5
    ti = lax.broadcasted_iota(jnp.int32, (c, c), 0)
    si = lax.broadcasted_iota(jnp.int32, (c, c), 1)
    txs = ti ^ si
    below = ti > si
    level_masks = [below & (txs >= h) & (txs < 2 * h) for h in _GLA_LEVELS]
    diag = ti == si

    st_ref[...] = jnp.zeros_like(st_ref)

    def chunk(ci, carry):
        r = pl.ds(pl.multiple_of(ci * c, c), c)
        g = g_ref[r, :]
        g_hi = g.astype(BF16)
        g_lo = (g - g_hi.astype(F32)).astype(BF16)
        f = jnp.exp(_dot(t_ref[...], jnp.concatenate([g_hi, g_lo], axis=0)))
        q = q_ref[r, :] * scale
        k = k_ref[r, :]
        v = v_ref[r, :].astype(BF16)
        a = jnp.where(diag, _dot_nt(q.astype(BF16), k.astype(BF16)), 0.0)
        for lvl in range(len(_GLA_LEVELS)):
            fl = f[2 * c + lvl * c: 3 * c + lvl * c]
            a = a + jnp.where(level_masks[lvl],
                              _dot_nt((q * fl).astype(BF16), (k * fl).astype(BF16)), 0.0)
        st = st_ref[...]
        qe = (q * f[0:c]).astype(BF16)
        o_ref[r, :] = _dot_nt(qe, st.astype(BF16)) + _dot(a.astype(BF16), v)
        kd = (k * f[c:2 * c]).astype(BF16)
        st_ref[...] = st * f[c - 1:c] + _dot_tn(v, kd)
        return carry

    lax.fori_loop(0, q_ref.shape[0] // c, chunk, 0, unroll=8)
    s_ref[0, 0] = st_ref[...].T


def _gla_prompt(p, la, tmat, batch, seq):
    return pl.pallas_call(
        _gla_kernel,
        grid=(batch, H_GLA),
        in_specs=[
            pl.BlockSpec((seq, DK_GLA), lambda b, h: (b, COL_QG // DK_GLA + h)),
            pl.BlockSpec((seq, DK_GLA), lambda b, h: (b, COL_KG // DK_GLA + h)),
            pl.BlockSpec((seq, DV_GLA), lambda b, h: (b, COL_VG // DV_GLA + h)),
            pl.BlockSpec((seq, DK_GLA), lambda b, h: (b, h)),
            pl.BlockSpec(tmat.shape, lambda b, h: (0, 0)),
        ],
        out_specs=[
            pl.BlockSpec((seq, DV_GLA), lambda b, h: (b, h)),
            pl.BlockSpec((1, 1, DK_GLA, DV_GLA), lambda b, h: (b, h, 0, 0)),
        ],
        out_shape=[
            jax.ShapeDtypeStruct((batch * seq, D_GLA_V), F32),
            jax.ShapeDtypeStruct((batch, H_GLA, DK_GLA, DV_GLA), F32),
        ],
        scratch_shapes=[pltpu.VMEM((DV_GLA, DK_GLA), F32)],
        compiler_params=_params("parallel", "parallel"),
        name="gla_prompt",
    )(p, p, p, la, tmat)


def _merge_kernel(x_ref, oa_ref, og_ref, rg_ref, ga_ref, gg_ref, w_ref, o_ref):
    a = (_rms(oa_ref[...]) * ga_ref[...]).astype(BF16)
    parts = []
    for h in range(H_GLA):
        cols = slice(h * DV_GLA, (h + 1) * DV_GLA)
        r = rg_ref[:, cols]
        parts.append((_rms(og_ref[:, cols]) * gg_ref[...] * (r * jax.nn.sigmoid(r))).astype(BF16))
    gg = jnp.concatenate(parts, axis=-1)
    o_ref[...] = x_ref[...] + _dot(a, w_ref[0:D_ATT, :]) + _dot(gg, w_ref[D_ATT:, :])


def _merge(x, o_att, o_gla, p, g_att, g_gla, w_out, *, tm):
    m = x.shape[0]
    return pl.pallas_call(
        _merge_kernel,
        grid=(m // tm,),
        in_specs=[
            pl.BlockSpec((tm, D_MODEL), lambda i: (i, 0)),
            pl.BlockSpec((tm, D_ATT), lambda i: (i, 0)),
            pl.BlockSpec((tm, D_GLA_V), lambda i: (i, 0)),
            pl.BlockSpec((tm, D_GLA_V), lambda i: (i, COL_RG // D_GLA_V)),
            pl.BlockSpec((1, D_ATT), lambda i: (0, 0)),
            pl.BlockSpec((1, DV_GLA), lambda i: (0, 0)),
            pl.BlockSpec((D_MODEL, D_MODEL), lambda i: (0, 0)),
        ],
        out_specs=pl.BlockSpec((tm, D_MODEL), lambda i: (i, 0)),
        out_shape=jax.ShapeDtypeStruct((m, D_MODEL), F32),
        compiler_params=_params("parallel"),
        name="merge",
    )(x, o_att, o_gla, p, g_att, g_gla, w_out)


SATT_HEADS_PER_STEP = 4


def _sattn_scores(q, k_ref, s_ref):
    q16 = (q * HD_ATT ** -0.5).astype(BF16)
    for h in range(k_ref.shape[1]):
        s_ref[h:h + 1, :] = _dot(q16, k_ref[0, h].astype(BF16))[h:h + 1, :]


def _sattn_finish(q, kn, vn, v_ref, s_ref):
    scale = HD_ATT ** -0.5
    wb = v_ref.shape[-1]
    t = lax.broadcasted_iota(jnp.int32, (1, wb), 1)
    cnt = jnp.zeros((1, wb), F32)
    for w, d in DIL_PATTERNS:
        cnt = cnt + jnp.where((t >= wb - w) & ((t & (d - 1)) == 0), 1.0, 0.0)
    bias = jnp.where(cnt > 0.0, 0.0, jnp.float32(-jnp.inf))
    n_pat = float(len(DIL_PATTERNS))
    n_heads = v_ref.shape[1]
    s_new = jnp.sum(q * scale * kn, axis=-1, keepdims=True)
    s = s_ref[...] + bias
    m = jnp.maximum(jnp.max(s, axis=-1, keepdims=True), s_new)
    e = cnt * jnp.exp(s - m)
    e_new = n_pat * jnp.exp(s_new - m)
    den = jnp.sum(e, axis=-1, keepdims=True) + e_new
    assert n_heads <= 8
    lane = lax.broadcasted_iota(jnp.int32, (HD_ATT, 128), 1)
    cols = jnp.zeros((HD_ATT, 128), F32)
    for h in range(n_heads):
        col = jnp.sum(v_ref[0, h] * e[h:h + 1, :], axis=-1, keepdims=True)
        for r, piece in enumerate(_bf16_pieces(col)):
            cols = jnp.where(lane == 8 * r + h, piece.astype(F32), cols)
    eye = (lax.broadcasted_iota(jnp.int32, (HD_ATT, HD_ATT), 0)
           == lax.broadcasted_iota(jnp.int32, (HD_ATT, HD_ATT), 1))
    rows = _dot_tn(cols.astype(BF16), jnp.where(eye, 1.0, 0.0).astype(BF16))
    pv = sum(rows[8 * r:8 * r + n_heads] for r in range(_SPLIT))
    return (pv + e_new * vn) / den


SGLA_SEQS_PER_STEP = 4
_SPLIT = 3


def _bf16_pieces(x):
    pieces = []
    for _ in range(_SPLIT):
        p = x.astype(BF16)
        pieces.append(p)
        x = x - p.astype(F32)
    return pieces


def _sgla_kernel(q_ref, k_ref, g_ref, v_ref, s_ref, so_ref, o_ref):
    scale = DK_GLA ** -0.5
    n_vec = 3
    sel_rows = 16
    r = lax.broadcasted_iota(jnp.int32, (sel_rows, n_vec * DV_GLA), 0)
    c = lax.broadcasted_iota(jnp.int32, (sel_rows, n_vec * DV_GLA), 1)
    selector = jnp.where((r // _SPLIT == c // DV_GLA) & (r < n_vec * _SPLIT), 1.0, 0.0).astype(BF16)
    pad = jnp.zeros((sel_rows - n_vec * _SPLIT, DK_GLA), BF16)
    for s in range(q_ref.shape[0]):
        for h in range(H_GLA):
            hrow = slice(h, h + 1)
            vecs = (jnp.exp(g_ref[s, hrow, :]), k_ref[s, hrow, :], q_ref[s, hrow, :] * scale)
            lhs = jnp.concatenate([p for x in vecs for p in _bf16_pieces(x)] + [pad], axis=0)
            cols = _dot_tn(lhs, selector)
            decay, kcol, qcol = (cols[:, i * DV_GLA:(i + 1) * DV_GLA] for i in range(n_vec))
            s_new = decay * s_ref[s, h] + kcol * v_ref[s, hrow, :]
            so_ref[s, h] = s_new
            o_ref[s, hrow, :] = jnp.sum(qcol * s_new, axis=0, keepdims=True)


def _gla_sample(q, k, g, v, state):
    nb = state.shape[0]
    bs = SGLA_SEQS_PER_STEP
    krow = pl.BlockSpec((bs, H_GLA, DK_GLA), lambda b: (b, 0, 0))
    vrow = pl.BlockSpec((bs, H_GLA, DV_GLA), lambda b: (b, 0, 0))
    st = pl.BlockSpec((bs, H_GLA, DK_GLA, DV_GLA), lambda b: (b, 0, 0, 0))
    return pl.pallas_call(
        _sgla_kernel,
        grid=(nb // bs,),
        in_specs=[krow, krow, krow, vrow, st],
        out_specs=[st, vrow],
        out_shape=[jax.ShapeDtypeStruct((nb, H_GLA, DK_GLA, DV_GLA), F32),
                   jax.ShapeDtypeStruct((nb, H_GLA, DV_GLA), F32)],
        compiler_params=_params("parallel"),
        name="gla_sample",
    )(q, k, g, v, state)


def kernel(x_prompt, x_sample, cache_att_k, cache_att_v, state_gla, g_ffn1, w_ffn1_in, w_ffn1_out, g_mix, w_in, w_gate2, b_gate2, g_att_out, g_gla_out, w_out, g_ffn2, w_ffn2_in, w_ffn2_out, g_final):
    depth = w_in.shape[0]
    assert depth == 1
    batch, seq, _ = x_prompt.shape
    nb, dec_seq, _ = x_sample.shape
    assert dec_seq == 1
    xp = x_prompt.reshape(batch * seq, D_MODEL)
    xs = x_sample.reshape(nb, D_MODEL)
    row = lambda a: a.reshape(1, -1)
    l = 0

    w_in_t = w_in[l].T
    w_lr = w_in_t[D_PROJ_MAIN:].astype(BF16)
    w_g2 = w_gate2[l].astype(BF16)
    b_g2 = row(b_gate2[l])
    w_o = w_out[l].astype(BF16)
    gf = row(g_final)
    tmat = jnp.asarray(_gla_exponent_matrix(), dtype=BF16)

    xs, w1a, w1b, w1o = _ffn(xs, row(g_ffn1[l]), w_ffn1_in[l], w_ffn1_in[l], w_ffn1_out[l], gf,
                             tm=nb, tf=512, final_norm=False, emit_bf16=True)
    ps, las, w_main = _proj(xs, row(g_mix[l]), w_in_t, w_lr, w_g2, b_g2, tm=nb, tn=512, emit_bf16=True)
    heads = lambda c0: ps[:, c0:c0 + D_ATT].reshape(nb, H_ATT, HD_ATT)
    k_new, v_new = heads(COL_KA), heads(COL_VA)
    xp, o_att_s, w2a, w2b, w2o = _ffn(
        xp, row(g_ffn1[l]), w1a, w1b, w1o, gf, tm=FFN_ROWS_SIDE, tf=FFN_TILE_SIDE, final_norm=False,
        side=(heads(COL_QA), k_new, v_new,
              jnp.transpose(cache_att_k[l], (0, 2, 3, 1)), jnp.transpose(cache_att_v[l], (0, 2, 3, 1)),
              w_ffn2_in[l], w_ffn2_out[l]))
    krows = lambda a: a.reshape(nb, H_GLA, DK_GLA)
    s_new, o_gla_s = _gla_sample(
        krows(ps[:, COL_QG:COL_QG + D_GLA_K]), krows(ps[:, COL_KG:COL_KG + D_GLA_K]), krows(las),
        ps[:, COL_VG:COL_VG + D_GLA_V].reshape(nb, H_GLA, DV_GLA), state_gla[l])
    xs = _merge(xs, o_att_s.reshape(nb, D_ATT), o_gla_s.reshape(nb, D_GLA_V), ps,
                row(g_att_out[l]), row(g_gla_out[l]), w_o, tm=nb)
    ys = _ffn(xs, row(g_ffn2[l]), w2a, w2b, w2o, gf, tm=nb, tf=FFN_TILE, final_norm=True)
    nk_s = k_new.reshape(1, nb, 1, H_ATT, HD_ATT)
    nv_s = v_new.reshape(1, nb, 1, H_ATT, HD_ATT)

    pp, lap = _proj(xp, row(g_mix[l]), w_main, w_lr, w_g2, b_g2, tm=1024, tn=1024)
    o_att, kt_p, vt_p = _attn_prompt(pp, batch, seq)
    o_gla, s_fin = _gla_prompt(pp, lap, tmat, batch, seq)
    xp = _merge(xp, o_att, o_gla, pp, row(g_att_out[l]), row(g_gla_out[l]), w_o, tm=512)
    yp = _ffn(xp, row(g_ffn2[l]), w2a, w2b, w2o, gf, tm=FFN_ROWS, tf=FFN_TILE, final_norm=True)
    nk_p = jnp.transpose(kt_p, (0, 3, 1, 2))[None]
    nv_p = jnp.transpose(vt_p, (0, 3, 1, 2))[None]

    return (yp.reshape(batch, seq, D_MODEL), ys.reshape(nb, 1, D_MODEL), nk_p, nv_p,
            s_fin[None], nk_s, nv_s, s_new[None])
```

```python
import functools

import numpy as np
import jax
import jax.numpy as jnp
from jax import lax
from jax.experimental import pallas as pl
from jax.experimental.pallas import tpu as pltpu

F32 = jnp.float32
BF16 = jnp.bfloat16

D_MODEL = 2048
D_FF = 5632
D_ATT = 1024
HD_ATT = 64
H_ATT = 16
H_GLA = 4
DK_GLA = 128
DV_GLA = 256
D_GLA_K = H_GLA * DK_GLA
D_GLA_V = H_GLA * DV_GLA
GATE_RANK = 16
GATE_TAU = 16.0
NORM_EPS = 1e-6
DIL_PATTERNS = ((128, 1), (512, 4), (2048, 16))
ATT_BLOCK = 128
ATT_GROUP = 16
GLA_CHUNK = 64
D_PROJ_MAIN = 3 * D_ATT + 2 * D_GLA_K + 2 * D_GLA_V

COL_QA, COL_KA, COL_VA = 0, D_ATT, 2 * D_ATT
COL_QG = 3 * D_ATT
COL_KG = COL_QG + D_GLA_K
COL_VG = COL_KG + D_GLA_K
COL_RG = COL_VG + D_GLA_V

VMEM_LIMIT_BYTES = 56 * 1024 * 1024

FFN_ROWS, FFN_TILE = 512, 512
FFN_ROWS_SIDE, FFN_TILE_SIDE = 1024, 256


def _rms(x):
    return x * lax.rsqrt(jnp.mean(x * x, axis=-1, keepdims=True) + NORM_EPS)


def _dot(a, b):
    return jnp.dot(a, b, preferred_element_type=F32)


def _dot_nt(a, b):
    return lax.dot_general(a, b, (((1,), (1,)), ((), ())), preferred_element_type=F32)


def _dot_tn(a, b):
    return lax.dot_general(a, b, (((0,), (0,)), ((), ())), preferred_element_type=F32)


def _params(*sem):
    return pltpu.CompilerParams(dimension_semantics=sem, vmem_limit_bytes=VMEM_LIMIT_BYTES)


def _ffn_step(x_ref, g_ref, wa, wb, wo, gf_ref, o_ref, h_ref, final_norm, side_jobs=None):
    j = pl.program_id(1)

    @pl.when(j == 0)
    def _():
        h_ref[...] = (_rms(x_ref[...]) * g_ref[...]).astype(BF16)
        o_ref[...] = jnp.zeros_like(o_ref)

    h = h_ref[...]
    a = _dot(h, wa)
    b = _dot(h, wb)
    act = (a * jax.nn.sigmoid(a) * b).astype(BF16)
    o_ref[...] += _dot(act, wo)
    if side_jobs is not None:
        side_jobs()

    @pl.when(j == pl.num_programs(1) - 1)
    def _():
        y = x_ref[...] + 0.5 * o_ref[...]
        if final_norm:
            y = _rms(y) * gf_ref[...]
        o_ref[...] = y


def _ffn_kernel(x_ref, g_ref, wa_ref, wb_ref, wo_ref, gf_ref, o_ref, *rest, final_norm, emit_bf16):
    wa, wb, wo = wa_ref[...], wb_ref[...], wo_ref[...]
    if emit_bf16:
        wa, wb, wo = wa.astype(BF16), wb.astype(BF16), wo.astype(BF16)
        for dst_ref, w in zip(rest[:3], (wa, wb, wo)):
            dst_ref[...] = w
    _ffn_step(x_ref, g_ref, wa, wb, wo, gf_ref, o_ref, rest[-1], final_norm)


def _ffn_side_kernel(x_ref, g_ref, wa_ref, wb_ref, wo_ref, gf_ref,
                     q_ref, kn_ref, vn_ref, k_ref, v_ref, cwi_ref, cwo_ref,
                     o_ref, so_ref, cwi16_ref, cwo16_ref, h_ref, s_ref, *, final_norm, n_side_blocks):
    def side_jobs():
        hs = k_ref.shape[1]
        groups = H_ATT // hs
        step = pl.program_id(0) * pl.num_programs(1) + pl.program_id(1)
        blk = jnp.minimum(step, n_side_blocks - 1)
        heads = pl.ds(pl.multiple_of((blk % groups) * hs, hs), hs)
        so_ref[0, heads, :] = _sattn_heads(q_ref[0, heads, :], kn_ref[0, heads, :], vn_ref[0, heads, :],
                                           k_ref, v_ref, s_ref)
        cwi16_ref[...] = cwi_ref[...].astype(BF16)
        cwo16_ref[...] = cwo_ref[...].astype(BF16)

    _ffn_step(x_ref, g_ref, wa_ref[...], wb_ref[...], wo_ref[...], gf_ref, o_ref, h_ref, final_norm, side_jobs)


def _ffn(x, g, wa, wb, wo, g_final, *, tm, tf, final_norm, emit_bf16=False, side=None):
    m = x.shape[0]
    nf = D_FF // tf
    ni = m // tm
    b_off = nf if wb.shape[1] == 2 * D_FF else 0
    in_specs = [
        pl.BlockSpec((tm, D_MODEL), lambda i, j: (i, 0)),
        pl.BlockSpec((1, D_MODEL), lambda i, j: (0, 0)),
        pl.BlockSpec((D_MODEL, tf), lambda i, j: (0, j)),
        pl.BlockSpec((D_MODEL, tf), lambda i, j: (0, j + b_off)),
        pl.BlockSpec((tf, D_MODEL), lambda i, j: (j, 0)),
        pl.BlockSpec((1, D_MODEL), lambda i, j: (0, 0)),
    ]
    out_specs = [pl.BlockSpec((tm, D_MODEL), lambda i, j: (i, 0))]
    out_shape = [jax.ShapeDtypeStruct((m, D_MODEL), F32)]
    scratch = [pltpu.VMEM((tm, D_MODEL), BF16)]
    operands = (x, g, wa, wb, wo, g_final)
    if side is None:
        body = functools.partial(_ffn_kernel, final_norm=final_norm, emit_bf16=emit_bf16)
        sem = ("parallel", "arbitrary")
    if emit_bf16:
        assert m == tm and side is None
        out_specs += [pl.BlockSpec((D_MODEL, tf), lambda i, j: (0, j)),
                      pl.BlockSpec((D_MODEL, tf), lambda i, j: (0, j)),
                      pl.BlockSpec((tf, D_MODEL), lambda i, j: (j, 0))]
        out_shape += [jax.ShapeDtypeStruct((D_MODEL, D_FF), BF16),
                      jax.ShapeDtypeStruct((D_MODEL, D_FF), BF16),
                      jax.ShapeDtypeStruct((D_FF, D_MODEL), BF16)]
    if side is not None:
        q, kn, vn, cache_kt, cache_vt, cw_in, cw_out = side
        nb, _, _, wb_len = cache_kt.shape
        assert all(wb_len % d == 0 and w <= wb_len for w, d in DIL_PATTERNS)
        hs = SATT_HEADS_PER_STEP
        groups = H_ATT // hs
        n_blocks = nb * groups
        assert ni * nf >= n_blocks and D_MODEL % ni == 0 and (2 * D_FF) % nf == 0
        sblk = lambda i, j: jnp.minimum(i * nf + j, n_blocks - 1)
        seq_row = pl.BlockSpec((1, H_ATT, HD_ATT), lambda i, j: (sblk(i, j) // groups, 0, 0))
        cache = pl.BlockSpec((1, hs, HD_ATT, wb_len),
                             lambda i, j: (sblk(i, j) // groups, sblk(i, j) % groups, 0, 0))
        cwi = pl.BlockSpec((D_MODEL // ni, 2 * D_FF // nf), lambda i, j: (i, j))
        cwo = pl.BlockSpec((D_FF // nf, D_MODEL // ni), lambda i, j: (j, i))
        in_specs += [seq_row, seq_row, seq_row, cache, cache, cwi, cwo]
        out_specs += [seq_row, cwi, cwo]
        out_shape += [jax.ShapeDtypeStruct((nb, H_ATT, HD_ATT), F32),
                      jax.ShapeDtypeStruct(cw_in.shape, BF16),
                      jax.ShapeDtypeStruct(cw_out.shape, BF16)]
        scratch.append(pltpu.VMEM((hs, wb_len), F32))
        operands += (q, kn, vn, cache_kt, cache_vt, cw_in, cw_out)
        body = functools.partial(_ffn_side_kernel, final_norm=final_norm, n_side_blocks=n_blocks)
        sem = ("arbitrary", "arbitrary")
    outs = pl.pallas_call(
        body,
        grid=(ni, nf),
        in_specs=in_specs,
        out_specs=out_specs,
        out_shape=out_shape,
        scratch_shapes=scratch,
        compiler_params=_params(*sem),
        name="ffn",
    )(*operands)
    return outs if (emit_bf16 or side is not None) else outs[0]


def _proj_kernel(x_ref, g_ref, w_ref, wlr_ref, wg2_ref, bg2_ref, p_ref, la_ref, *rest, emit_bf16):
    h_ref = rest[-1]
    j = pl.program_id(1)

    @pl.when(j == 0)
    def _():
        h = (_rms(x_ref[...]) * g_ref[...]).astype(BF16)
        h_ref[...] = h
        lr = _dot_nt(h, wlr_ref[...])
        z = _dot(lr.astype(BF16), wg2_ref[...]) + bg2_ref[...]
        log_sig = jnp.minimum(z, 0.0) - jnp.log1p(jnp.exp(-jnp.abs(z)))
        la_ref[...] = log_sig * (1.0 / GATE_TAU)

    w = w_ref[...]
    if emit_bf16:
        w = w.astype(BF16)
        rest[0][...] = w
    p_ref[...] = _dot_nt(h_ref[...], w)


def _proj(x, g, w_t, w_lr_t, w_g2, b_g2, *, tm, tn, emit_bf16=False):
    m = x.shape[0]
    out_specs = [
        pl.BlockSpec((tm, tn), lambda i, j: (i, j)),
        pl.BlockSpec((tm, D_GLA_K), lambda i, j: (i, 0)),
    ]
    out_shape = [
        jax.ShapeDtypeStruct((m, D_PROJ_MAIN), F32),
        jax.ShapeDtypeStruct((m, D_GLA_K), F32),
    ]
    if emit_bf16:
        assert m == tm
        out_specs.append(pl.BlockSpec((tn, D_MODEL), lambda i, j: (j, 0)))
        out_shape.append(jax.ShapeDtypeStruct((D_PROJ_MAIN, D_MODEL), BF16))
    return pl.pallas_call(
        functools.partial(_proj_kernel, emit_bf16=emit_bf16),
        grid=(m // tm, D_PROJ_MAIN // tn),
        in_specs=[
            pl.BlockSpec((tm, D_MODEL), lambda i, j: (i, 0)),
            pl.BlockSpec((1, D_MODEL), lambda i, j: (0, 0)),
            pl.BlockSpec((tn, D_MODEL), lambda i, j: (j, 0)),
            pl.BlockSpec((GATE_RANK, D_MODEL), lambda i, j: (0, 0)),
            pl.BlockSpec((GATE_RANK, D_GLA_K), lambda i, j: (0, 0)),
            pl.BlockSpec((1, D_GLA_K), lambda i, j: (0, 0)),
        ],
        out_specs=out_specs,
        out_shape=out_shape,
        scratch_shapes=[pltpu.VMEM((tm, D_MODEL), BF16)],
        compiler_params=_params("parallel", "arbitrary"),
        name="proj",
    )(x, g, w_t, w_lr_t, w_g2, b_g2)


def _attn_kernel(q_ref, k_ref, v_ref, o_ref, kt_ref, vt_ref,
                 qp_ref, kp_ref, vp_ref, m_ref, l_ref, acc_ref, s_ref, ms_ref):
    blk = ATT_BLOCK
    for src_ref, dst_ref in ((k_ref, kt_ref), (v_ref, vt_ref)):
        t = src_ref[...].T
        dst_ref[0, 0] = t[:HD_ATT]
        dst_ref[0, 1] = t[HD_ATT:]

    scale = HD_ATT ** -0.5
    seq = q_ref.shape[0]
    ns = DIL_PATTERNS[-1][1]
    lane_lo = lax.broadcasted_iota(jnp.int32, (1, 2 * HD_ATT), 1) < HD_ATT
    neg = jnp.float32(-jnp.inf)

    assert ns == 16
    for src_ref, tmp_ref, dst_ref in ((q_ref, m_ref, qp_ref), (k_ref, l_ref, kp_ref), (v_ref, acc_ref, vp_ref)):
        for r4 in range(4):
            x = src_ref[pl.ds(r4, seq // 4, stride=4), :]
            tmp_ref[pl.ds(r4 * (seq // 4), seq // 4), :] = x * scale if src_ref is q_ref else x
        for r4 in range(4):
            for a in range(4):
                dst_ref[pl.ds((4 * a + r4) * blk, blk), :] = tmp_ref[pl.ds(r4 * (seq // 4) + a, blk, stride=4), :]

    def run_pattern(d, first_pattern):
        na = ns // d
        plen = blk // na
        per_stream = seq // (d * blk)

        def offset(idx):
            return na * (idx & (plen - 1)) + idx // plen

        qpos = offset(lax.broadcasted_iota(jnp.int32, (blk, blk), 0))
        kpos = offset(lax.broadcasted_iota(jnp.int32, (blk, blk), 1))
        bias_cur = jnp.where(kpos <= qpos, 0.0, neg)
        if per_stream > 1:
            bias_prev = jnp.where(kpos >= qpos, 0.0, neg)
            bias_band = jnp.concatenate([bias_prev, bias_cur], axis=1)
            bias_first = jnp.concatenate([jnp.full((blk, blk), neg, F32), bias_cur], axis=1)

        def pieces(rd, n):
            return [pl.ds((a * d + rd) * blk + plen * n, plen) for a in range(na)]

        def gather(ref, ps):
            return jnp.concatenate([ref[p, :] for p in ps], axis=0)

        nk = 2 * blk if per_stream > 1 else blk
        heads = (lane_lo, jnp.logical_not(lane_lo))

        def keys(ref, rd, n):
            cur = gather(ref, pieces(rd, n))
            if per_stream == 1:
                return cur.astype(BF16)
            prev = gather(ref, pieces(rd, max(n - 1, 0)))
            return jnp.concatenate([prev, cur], axis=0).astype(BF16)

        def group(g):
            blocks = [((g * ATT_GROUP + b) % d, (g * ATT_GROUP + b) // d) for b in range(ATT_GROUP)]
            for b, (rd, n) in enumerate(blocks):
                q = gather(qp_ref, pieces(rd, n))
                kk = keys(kp_ref, rd, n)
                bias = bias_cur if per_stream == 1 else (bias_band if n > 0 else bias_first)
                for h, sel in enumerate(heads):
                    qh = jnp.where(sel, q, 0.0).astype(BF16)
                    s_ref[b, h, :, :nk] = _dot_nt(qh, kk) + bias
            for b in range(ATT_GROUP):
                for h in range(2):
                    m = jnp.max(s_ref[b, h, :, :nk], axis=-1, keepdims=True)
                    ms_ref[b, h] = jnp.broadcast_to(m, (blk, blk))
            for b, (rd, n) in enumerate(blocks):
                vv = jnp.concatenate([keys(vp_ref, rd, n), jnp.ones((nk, blk), BF16)], axis=1)
                res = []
                for h in range(2):
                    mh = ms_ref[b, h]
                    mh = jnp.concatenate([mh, mh], axis=1) if nk == 2 * blk else mh
                    res.append(_dot(jnp.exp(s_ref[b, h, :, :nk] - mh).astype(BF16), vv))
                u = jnp.where(lane_lo, res[0][:, :blk], res[1][:, :blk])
                l = jnp.where(lane_lo, res[0][:, blk:], res[1][:, blk:])
                m = jnp.where(lane_lo, ms_ref[b, 0], ms_ref[b, 1])
                for a, p in enumerate(pieces(rd, n)):
                    sl = slice(a * plen, (a + 1) * plen)
                    if first_pattern:
                        m_ref[p, :] = m[sl]
                        l_ref[p, :] = l[sl]
                        acc_ref[p, :] = u[sl]
                    else:
                        m_old = m_ref[p, :]
                        m_new = jnp.maximum(m_old, m[sl])
                        a_old = jnp.exp(m_old - m_new)
                        a_blk = jnp.exp(m[sl] - m_new)
                        m_ref[p, :] = m_new
                        l_ref[p, :] = a_old * l_ref[p, :] + a_blk * l[sl]
                        acc_ref[p, :] = a_old * acc_ref[p, :] + a_blk * u[sl]

        for g in range(d * per_stream // ATT_GROUP):
            group(g)

    for idx, (_, d) in enumerate(DIL_PATTERNS):
        run_pattern(d, idx == 0)

    for r4 in range(4):
        for a in range(4):
            rows = pl.ds((4 * a + r4) * blk, blk)
            qp_ref[pl.ds(r4 * (seq // 4) + a, blk, stride=4), :] = acc_ref[rows, :] / l_ref[rows, :]
    for r4 in range(4):
        o_ref[pl.ds(r4, seq // 4, stride=4), :] = qp_ref[pl.ds(r4 * (seq // 4), seq // 4), :]


def _attn_prompt(p, batch, seq):
    ns = DIL_PATTERNS[-1][1]
    assert seq == ns * ATT_BLOCK
    assert all(w // d == ATT_BLOCK and ns % d == 0 and ATT_BLOCK * d // ns >= 8 for w, d in DIL_PATTERNS)
    lanes = 2 * HD_ATT
    n_pairs = H_ATT // 2
    return pl.pallas_call(
        _attn_kernel,
        grid=(batch, n_pairs),
        in_specs=[
            pl.BlockSpec((seq, lanes), lambda b, h: (b, COL_QA // lanes + h)),
            pl.BlockSpec((seq, lanes), lambda b, h: (b, COL_KA // lanes + h)),
            pl.BlockSpec((seq, lanes), lambda b, h: (b, COL_VA // lanes + h)),
        ],
        out_specs=[
            pl.BlockSpec((seq, lanes), lambda b, h: (b, h)),
            pl.BlockSpec((1, 2, HD_ATT, seq), lambda b, h: (b, h, 0, 0)),
            pl.BlockSpec((1, 2, HD_ATT, seq), lambda b, h: (b, h, 0, 0)),
        ],
        out_shape=[
            jax.ShapeDtypeStruct((batch * seq, D_ATT), F32),
            jax.ShapeDtypeStruct((batch, H_ATT, HD_ATT, seq), F32),
            jax.ShapeDtypeStruct((batch, H_ATT, HD_ATT, seq), F32),
        ],
        scratch_shapes=[pltpu.VMEM((seq, lanes), F32)] * 6 + [
            pltpu.VMEM((ATT_GROUP, 2, ATT_BLOCK, 2 * ATT_BLOCK), F32),
            pltpu.VMEM((ATT_GROUP, 2, ATT_BLOCK, lanes), F32),
        ],
        compiler_params=_params("parallel", "parallel"),
        name="attn_prompt",
    )(p, p, p)


_GLA_LEVELS = (32, 16, 8, 4, 2, 1)


def _gla_exponent_matrix():
    c = GLA_CHUNK
    t = np.arange(c)[:, None]
    u = np.arange(c)[None, :]
    mats = [(u <= t), (u > t)]
    for h in _GLA_LEVELS:
        mid = (t // (2 * h)) * (2 * h) + h - 1
        upper = (t % (2 * h)) >= h
        mats.append(np.where(upper, (u > mid) & (u <= t), (u > t) & (u <= mid)))
    tmat = np.concatenate(mats, axis=0).astype(np.float32)
    return np.concatenate([tmat, tmat], axis=1)


def _gla_kernel(q_ref, k_ref, v_ref, g_ref, t_ref, o_ref, s_ref, st_ref):
    c = GLA_CHUNK
    scale = DK_GLA ** -0.5
    ti = lax.broadcasted_iota(jnp.int32, (c, c), 0)
    si = lax.broadcasted_iota(jnp.int32, (c, c), 1)
    txs = ti ^ si
    below = ti > si
    level_masks = [below & (txs >= h) & (txs < 2 * h) for h in _GLA_LEVELS]
    diag = ti == si

    st_ref[...] = jnp.zeros_like(st_ref)

    def chunk(ci, carry):
        r = pl.ds(pl.multiple_of(ci * c, c), c)
        g = g_ref[r, :]
        g_hi = g.astype(BF16)
        g_lo = (g - g_hi.astype(F32)).astype(BF16)
        f = jnp.exp(_dot(t_ref[...], jnp.concatenate([g_hi, g_lo], axis=0)))
        q = q_ref[r, :] * scale
        k = k_ref[r, :]
        v = v_ref[r, :].astype(BF16)
        a = jnp.where(diag, _dot_nt(q.astype(BF16), k.astype(BF16)), 0.0)
        for lvl in range(len(_GLA_LEVELS)):
            fl = f[2 * c + lvl * c: 3 * c + lvl * c]
            a = a + jnp.where(level_masks[lvl],
                              _dot_nt((q * fl).astype(BF16), (k * fl).astype(BF16)), 0.0)
        st = st_ref[...]
        qe = (q * f[0:c]).astype(BF16)
        o_ref[r, :] = _dot_nt(qe, st.astype(BF16)) + _dot(a.astype(BF16), v)
        kd = (k * f[c:2 * c]).astype(BF16)
        st_ref[...] = st * f[c - 1:c] + _dot_tn(v, kd)
        return carry

    lax.fori_loop(0, q_ref.shape[0] // c, chunk, 0, unroll=8)
    s_ref[0, 0] = st_ref[...].T


def _gla_prompt(p, la, tmat, batch, seq):
    return pl.pallas_call(
        _gla_kernel,
        grid=(batch, H_GLA),
        in_specs=[
            pl.BlockSpec((seq, DK_GLA), lambda b, h: (b, COL_QG // DK_GLA + h)),
            pl.BlockSpec((seq, DK_GLA), lambda b, h: (b, COL_KG // DK_GLA + h)),
            pl.BlockSpec((seq, DV_GLA), lambda b, h: (b, COL_VG // DV_GLA + h)),
            pl.BlockSpec((seq, DK_GLA), lambda b, h: (b, h)),
            pl.BlockSpec(tmat.shape, lambda b, h: (0, 0)),
        ],
        out_specs=[
            pl.BlockSpec((seq, DV_GLA), lambda b, h: (b, h)),
            pl.BlockSpec((1, 1, DK_GLA, DV_GLA), lambda b, h: (b, h, 0, 0)),
        ],
        out_shape=[
            jax.ShapeDtypeStruct((batch * seq, D_GLA_V), F32),
            jax.ShapeDtypeStruct((batch, H_GLA, DK_GLA, DV_GLA), F32),
        ],
        scratch_shapes=[pltpu.VMEM((DV_GLA, DK_GLA), F32)],
        compiler_params=_params("parallel", "parallel"),
        name="gla_prompt",
    )(p, p, p, la, tmat)


def _merge_kernel(x_ref, oa_ref, og_ref, rg_ref, ga_ref, gg_ref, w_ref, o_ref):
    a = (_rms(oa_ref[...]) * ga_ref[...]).astype(BF16)
    parts = []
    for h in range(H_GLA):
        cols = slice(h * DV_GLA, (h + 1) * DV_GLA)
        r = rg_ref[:, cols]
        parts.append((_rms(og_ref[:, cols]) * gg_ref[...] * (r * jax.nn.sigmoid(r))).astype(BF16))
    gg = jnp.concatenate(parts, axis=-1)
    o_ref[...] = x_ref[...] + _dot(a, w_ref[0:D_ATT, :]) + _dot(gg, w_ref[D_ATT:, :])


def _merge(x, o_att, o_gla, p, g_att, g_gla, w_out, *, tm):
    m = x.shape[0]
    return pl.pallas_call(
        _merge_kernel,
        grid=(m // tm,),
        in_specs=[
            pl.BlockSpec((tm, D_MODEL), lambda i: (i, 0)),
            pl.BlockSpec((tm, D_ATT), lambda i: (i, 0)),
            pl.BlockSpec((tm, D_GLA_V), lambda i: (i, 0)),
            pl.BlockSpec((tm, D_GLA_V), lambda i: (i, COL_RG // D_GLA_V)),
            pl.BlockSpec((1, D_ATT), lambda i: (0, 0)),
            pl.BlockSpec((1, DV_GLA), lambda i: (0, 0)),
            pl.BlockSpec((D_MODEL, D_MODEL), lambda i: (0, 0)),
        ],
        out_specs=pl.BlockSpec((tm, D_MODEL), lambda i: (i, 0)),
        out_shape=jax.ShapeDtypeStruct((m, D_MODEL), F32),
        compiler_params=_params("parallel"),
        name="merge",
    )(x, o_att, o_gla, p, g_att, g_gla, w_out)


SATT_HEADS_PER_STEP = 4


def _sattn_heads(q, kn, vn, k_ref, v_ref, s_ref):
    scale = HD_ATT ** -0.5
    wb = k_ref.shape[-1]
    t = lax.broadcasted_iota(jnp.int32, (1, wb), 1)
    cnt = jnp.zeros((1, wb), F32)
    for w, d in DIL_PATTERNS:
        cnt = cnt + jnp.where((t >= wb - w) & ((t & (d - 1)) == 0), 1.0, 0.0)
    bias = jnp.where(cnt > 0.0, 0.0, jnp.float32(-jnp.inf))
    n_pat = float(len(DIL_PATTERNS))
    n_heads = k_ref.shape[1]
    q = q * scale
    q16 = q.astype(BF16)
    s_new = jnp.sum(q * kn, axis=-1, keepdims=True)
    for h in range(n_heads):
        s_ref[h:h + 1, :] = _dot(q16, k_ref[0, h].astype(BF16))[h:h + 1, :]
    s = s_ref[...] + bias
    m = jnp.maximum(jnp.max(s, axis=-1, keepdims=True), s_new)
    e = cnt * jnp.exp(s - m)
    e_new = n_pat * jnp.exp(s_new - m)
    den = jnp.sum(e, axis=-1, keepdims=True) + e_new
    e16 = e.astype(BF16)
    head = lax.broadcasted_iota(jnp.int32, (n_heads, HD_ATT), 0)
    num = e_new * vn
    for h in range(n_heads):
        num = num + jnp.where(head == h, _dot_nt(e16, v_ref[0, h].astype(BF16)), 0.0)
    return num / den


SGLA_SEQS_PER_STEP = 4
_SPLIT = 3


def _bf16_pieces(x):
    pieces = []
    for _ in range(_SPLIT):
        p = x.astype(BF16)
        pieces.append(p)
        x = x - p.astype(F32)
    return pieces


def _sgla_kernel(q_ref, k_ref, g_ref, v_ref, s_ref, so_ref, o_ref):
    scale = DK_GLA ** -0.5
    n_vec = 3
    sel_rows = 16
    r = lax.broadcasted_iota(jnp.int32, (sel_rows, n_vec * DV_GLA), 0)
    c = lax.broadcasted_iota(jnp.int32, (sel_rows, n_vec * DV_GLA), 1)
    selector = jnp.where((r // _SPLIT == c // DV_GLA) & (r < n_vec * _SPLIT), 1.0, 0.0).astype(BF16)
    pad = jnp.zeros((sel_rows - n_vec * _SPLIT, DK_GLA), BF16)
    for s in range(q_ref.shape[0]):
        for h in range(H_GLA):
            hrow = slice(h, h + 1)
            vecs = (jnp.exp(g_ref[s, hrow, :]), k_ref[s, hrow, :], q_ref[s, hrow, :] * scale)
            lhs = jnp.concatenate([p for x in vecs for p in _bf16_pieces(x)] + [pad], axis=0)
            cols = _dot_tn(lhs, selector)
            decay, kcol, qcol = (cols[:, i * DV_GLA:(i + 1) * DV_GLA] for i in range(n_vec))
            s_new = decay * s_ref[s, h] + kcol * v_ref[s, hrow, :]
            so_ref[s, h] = s_new
            o_ref[s, hrow, :] = jnp.sum(qcol * s_new, axis=0, keepdims=True)


def _gla_sample(q, k, g, v, state):
    nb = state.shape[0]
    bs = SGLA_SEQS_PER_STEP
    krow = pl.BlockSpec((bs, H_GLA, DK_GLA), lambda b: (b, 0, 0))
    vrow = pl.BlockSpec((bs, H_GLA, DV_GLA), lambda b: (b, 0, 0))
    st = pl.BlockSpec((bs, H_GLA, DK_GLA, DV_GLA), lambda b: (b, 0, 0, 0))
    return pl.pallas_call(
        _sgla_kernel,
        grid=(nb // bs,),
        in_specs=[krow, krow, krow, vrow, st],
        out_specs=[st, vrow],
        out_shape=[jax.ShapeDtypeStruct((nb, H_GLA, DK_GLA, DV_GLA), F32),
                   jax.ShapeDtypeStruct((nb, H_GLA, DV_GLA), F32)],
        compiler_params=_params("parallel"),
        name="gla_sample",
    )(q, k, g, v, state)


def kernel(x_prompt, x_sample, cache_att_k, cache_att_v, state_gla, g_ffn1, w_ffn1_in, w_ffn1_out, g_mix, w_in, w_gate2, b_gate2, g_att_out, g_gla_out, w_out, g_ffn2, w_ffn2_in, w_ffn2_out, g_final):
    depth = w_in.shape[0]
    assert depth == 1
    batch, seq, _ = x_prompt.shape
    nb, dec_seq, _ = x_sample.shape
    assert dec_seq == 1
    xp = x_prompt.reshape(batch * seq, D_MODEL)
    xs = x_sample.reshape(nb, D_MODEL)
    row = lambda a: a.reshape(1, -1)
    l = 0

    w_in_t = w_in[l].T
    w_lr = w_in_t[D_PROJ_MAIN:].astype(BF16)
    w_g2 = w_gate2[l].astype(BF16)
    b_g2 = row(b_gate2[l])
    w_o = w_out[l].astype(BF16)
    gf = row(g_final)
    tmat = jnp.asarray(_gla_exponent_matrix(), dtype=BF16)

    xs, w1a, w1b, w1o = _ffn(xs, row(g_ffn1[l]), w_ffn1_in[l], w_ffn1_in[l], w_ffn1_out[l], gf,
                             tm=nb, tf=FFN_TILE, final_norm=False, emit_bf16=True)
    ps, las, w_main = _proj(xs, row(g_mix[l]), w_in_t, w_lr, w_g2, b_g2, tm=nb, tn=512, emit_bf16=True)
    heads = lambda c0: ps[:, c0:c0 + D_ATT].reshape(nb, H_ATT, HD_ATT)
    k_new, v_new = heads(COL_KA), heads(COL_VA)
    xp, o_att_s, w2i, w2o = _ffn(
        xp, row(g_ffn1[l]), w1a, w1b, w1o, gf, tm=FFN_ROWS_SIDE, tf=FFN_TILE_SIDE, final_norm=False,
        side=(heads(COL_QA), k_new, v_new,
              jnp.transpose(cache_att_k[l], (0, 2, 3, 1)), jnp.transpose(cache_att_v[l], (0, 2, 3, 1)),
              w_ffn2_in[l], w_ffn2_out[l]))
    krows = lambda a: a.reshape(nb, H_GLA, DK_GLA)
    s_new, o_gla_s = _gla_sample(
        krows(ps[:, COL_QG:COL_QG + D_GLA_K]), krows(ps[:, COL_KG:COL_KG + D_GLA_K]), krows(las),
        ps[:, COL_VG:COL_VG + D_GLA_V].reshape(nb, H_GLA, DV_GLA), state_gla[l])
    xs = _merge(xs, o_att_s.reshape(nb, D_ATT), o_gla_s.reshape(nb, D_GLA_V), ps,
                row(g_att_out[l]), row(g_gla_out[l]), w_o, tm=nb)
    ys = _ffn(xs, row(g_ffn2[l]), w2i, w2i, w2o, gf, tm=nb, tf=FFN_TILE, final_norm=True)
    nk_s = k_new.reshape(1, nb, 1, H_ATT, HD_ATT)
    nv_s = v_new.reshape(1, nb, 1, H_ATT, HD_ATT)

    pp, lap = _proj(xp, row(g_mix[l]), w_main, w_lr, w_g2, b_g2, tm=1024, tn=1024)
    o_att, kt_p, vt_p = _attn_prompt(pp, batch, seq)
    o_gla, s_fin = _gla_prompt(pp, lap, tmat, batch, seq)
    xp = _merge(xp, o_att, o_gla, pp, row(g_att_out[l]), row(g_gla_out[l]), w_o, tm=512)
    yp = _ffn(xp, row(g_ffn2[l]), w2i, w2i, w2o, gf, tm=FFN_ROWS, tf=FFN_TILE, final_norm=True)
    nk_p = jnp.transpose(kt_p, (0, 3, 1, 2))[None]
    nv_p = jnp.transpose(vt_p, (0, 3, 1, 2))[None]

    return (yp.reshape(batch, seq, D_MODEL), ys.reshape(nb, 1, D_MODEL), nk_p, nv_p,
            s_fin[None], nk_s, nv_s, s_new[None])
```

```python
import functools

import numpy as np
import jax
import jax.numpy as jnp
from jax import lax
from jax.experimental import pallas as pl
from jax.experimental.pallas import tpu as pltpu

F32 = jnp.float32
BF16 = jnp.bfloat16

D_MODEL = 2048
D_FF = 5632
D_ATT = 1024
HD_ATT = 64
H_ATT = 16
H_GLA = 4
DK_GLA = 128
DV_GLA = 256
D_GLA_K = H_GLA * DK_GLA
D_GLA_V = H_GLA * DV_GLA
GATE_RANK = 16
GATE_TAU = 16.0
NORM_EPS = 1e-6
DIL_PATTERNS = ((128, 1), (512, 4), (2048, 16))
ATT_BLOCK = 128
ATT_GROUP = 16
GLA_CHUNK = 64
GLA_GROUP = 8
D_PROJ_MAIN = 3 * D_ATT + 2 * D_GLA_K + 2 * D_GLA_V

COL_QA, COL_KA, COL_VA = 0, D_ATT, 2 * D_ATT
COL_QG = 3 * D_ATT
COL_KG = COL_QG + D_GLA_K
COL_VG = COL_KG + D_GLA_K
COL_RG = COL_VG + D_GLA_V

VMEM_LIMIT_BYTES = 56 * 1024 * 1024

FFN_ROWS, FFN_TILE_WIDE_ROWS = 1024, 256
FFN_ROWS_SIDE, FFN_TILE_SIDE = FFN_ROWS, FFN_TILE_WIDE_ROWS
FFN_TILE = 512


def _rms(x):
    return x * lax.rsqrt(jnp.mean(x * x, axis=-1, keepdims=True) + NORM_EPS)


def _dot(a, b):
    return jnp.dot(a, b, preferred_element_type=F32)


def _dot_nt(a, b):
    return lax.dot_general(a, b, (((1,), (1,)), ((), ())), preferred_element_type=F32)


def _dot_tn(a, b):
    return lax.dot_general(a, b, (((0,), (0,)), ((), ())), preferred_element_type=F32)


def _params(*sem):
    return pltpu.CompilerParams(dimension_semantics=sem, vmem_limit_bytes=VMEM_LIMIT_BYTES)


def _ffn_step(x_ref, g_ref, wa, wb, wo, gf_ref, o_ref, h_ref, final_norm, side_jobs=None):
    j = pl.program_id(1)

    @pl.when(j == 0)
    def _():
        h_ref[...] = (_rms(x_ref[...]) * g_ref[...]).astype(BF16)
        o_ref[...] = jnp.zeros_like(o_ref)

    h = h_ref[...]
    a = _dot(h, wa)
    b = _dot(h, wb)
    act = (a * jax.nn.sigmoid(a) * b).astype(BF16)
    o_ref[...] += _dot(act, wo)
    if side_jobs is not None:
        side_jobs()

    @pl.when(j == pl.num_programs(1) - 1)
    def _():
        y = x_ref[...] + 0.5 * o_ref[...]
        if final_norm:
            y = _rms(y) * gf_ref[...]
        o_ref[...] = y


def _ffn_kernel(x_ref, g_ref, wa_ref, wb_ref, wo_ref, gf_ref, o_ref, *rest, final_norm, emit_bf16):
    wa, wb, wo = wa_ref[...], wb_ref[...], wo_ref[...]
    if emit_bf16:
        wa, wb, wo = wa.astype(BF16), wb.astype(BF16), wo.astype(BF16)
        for dst_ref, w in zip(rest[:3], (wa, wb, wo)):
            dst_ref[...] = w
    _ffn_step(x_ref, g_ref, wa, wb, wo, gf_ref, o_ref, rest[-1], final_norm)


def _ffn_side_kernel(x_ref, g_ref, wa_ref, wb_ref, wo_ref, gf_ref,
                     q_ref, kn_ref, vn_ref, k_ref, v_ref, cwi_ref, cwo_ref,
                     o_ref, so_ref, cwi16_ref, cwo16_ref, h_ref, s_ref, *, final_norm, n_side_blocks):
    def side_jobs():
        hs = k_ref.shape[1]
        groups = H_ATT // hs
        step = pl.program_id(0) * pl.num_programs(1) + pl.program_id(1)
        blk = jnp.minimum(step, n_side_blocks - 1)
        heads = pl.ds(pl.multiple_of((blk % groups) * hs, hs), hs)
        so_ref[0, heads, :] = _sattn_heads(q_ref[0, heads, :], kn_ref[0, heads, :], vn_ref[0, heads, :],
                                           k_ref, v_ref, s_ref)
        cwi16_ref[...] = cwi_ref[...].astype(BF16)
        cwo16_ref[...] = cwo_ref[...].astype(BF16)

    _ffn_step(x_ref, g_ref, wa_ref[...], wb_ref[...], wo_ref[...], gf_ref, o_ref, h_ref, final_norm, side_jobs)


def _ffn(x, g, wa, wb, wo, g_final, *, tm, tf, final_norm, emit_bf16=False, side=None):
    m = x.shape[0]
    nf = D_FF // tf
    ni = m // tm
    b_off = nf if wb.shape[1] == 2 * D_FF else 0
    in_specs = [
        pl.BlockSpec((tm, D_MODEL), lambda i, j: (i, 0)),
        pl.BlockSpec((1, D_MODEL), lambda i, j: (0, 0)),
        pl.BlockSpec((D_MODEL, tf), lambda i, j: (0, j)),
        pl.BlockSpec((D_MODEL, tf), lambda i, j: (0, j + b_off)),
        pl.BlockSpec((tf, D_MODEL), lambda i, j: (j, 0)),
        pl.BlockSpec((1, D_MODEL), lambda i, j: (0, 0)),
    ]
    out_specs = [pl.BlockSpec((tm, D_MODEL), lambda i, j: (i, 0))]
    out_shape = [jax.ShapeDtypeStruct((m, D_MODEL), F32)]
    scratch = [pltpu.VMEM((tm, D_MODEL), BF16)]
    operands = (x, g, wa, wb, wo, g_final)
    if side is None:
        body = functools.partial(_ffn_kernel, final_norm=final_norm, emit_bf16=emit_bf16)
        sem = ("parallel", "arbitrary")
    if emit_bf16:
        assert m == tm and side is None
        out_specs += [pl.BlockSpec((D_MODEL, tf), lambda i, j: (0, j)),
                      pl.BlockSpec((D_MODEL, tf), lambda i, j: (0, j)),
                      pl.BlockSpec((tf, D_MODEL), lambda i, j: (j, 0))]
        out_shape += [jax.ShapeDtypeStruct((D_MODEL, D_FF), BF16),
                      jax.ShapeDtypeStruct((D_MODEL, D_FF), BF16),
                      jax.ShapeDtypeStruct((D_FF, D_MODEL), BF16)]
    if side is not None:
        q, kn, vn, cache_kt, cache_vt, cw_in, cw_out = side
        nb, _, _, wb_len = cache_kt.shape
        assert all(wb_len % d == 0 and w <= wb_len for w, d in DIL_PATTERNS)
        hs = SATT_HEADS_PER_STEP
        groups = H_ATT // hs
        n_blocks = nb * groups
        assert ni * nf >= n_blocks and D_MODEL % ni == 0 and (2 * D_FF) % nf == 0
        sblk = lambda i, j: jnp.minimum(i * nf + j, n_blocks - 1)
        seq_row = pl.BlockSpec((1, H_ATT, HD_ATT), lambda i, j: (sblk(i, j) // groups, 0, 0))
        cache = pl.BlockSpec((1, hs, HD_ATT, wb_len),
                             lambda i, j: (sblk(i, j) // groups, sblk(i, j) % groups, 0, 0))
        cwi = pl.BlockSpec((D_MODEL // ni, 2 * D_FF // nf), lambda i, j: (i, j))
        cwo = pl.BlockSpec((D_FF // nf, D_MODEL // ni), lambda i, j: (j, i))
        in_specs += [seq_row, seq_row, seq_row, cache, cache, cwi, cwo]
        out_specs += [seq_row, cwi, cwo]
        out_shape += [jax.ShapeDtypeStruct((nb, H_ATT, HD_ATT), F32),
                      jax.ShapeDtypeStruct(cw_in.shape, BF16),
                      jax.ShapeDtypeStruct(cw_out.shape, BF16)]
        scratch.append(pltpu.VMEM((hs, wb_len), F32))
        operands += (q, kn, vn, cache_kt, cache_vt, cw_in, cw_out)
        body = functools.partial(_ffn_side_kernel, final_norm=final_norm, n_side_blocks=n_blocks)
        sem = ("arbitrary", "arbitrary")
    outs = pl.pallas_call(
        body,
        grid=(ni, nf),
        in_specs=in_specs,
        out_specs=out_specs,
        out_shape=out_shape,
        scratch_shapes=scratch,
        compiler_params=_params(*sem),
        name="ffn",
    )(*operands)
    return outs if (emit_bf16 or side is not None) else outs[0]


def _proj_kernel(x_ref, g_ref, w_ref, wlr_ref, wg2_ref, bg2_ref, p_ref, la_ref, *rest, emit_bf16):
    h_ref = rest[-1]
    j = pl.program_id(1)

    @pl.when(j == 0)
    def _():
        h = (_rms(x_ref[...]) * g_ref[...]).astype(BF16)
        h_ref[...] = h
        lr = _dot_nt(h, wlr_ref[...])
        z = _dot(lr.astype(BF16), wg2_ref[...]) + bg2_ref[...]
        log_sig = jnp.minimum(z, 0.0) - jnp.log1p(jnp.exp(-jnp.abs(z)))
        la_ref[...] = log_sig * (1.0 / GATE_TAU)

    w = w_ref[...]
    if emit_bf16:
        w = w.astype(BF16)
        rest[0][...] = w
    p_ref[...] = _dot_nt(h_ref[...], w)


def _proj(x, g, w_t, w_lr_t, w_g2, b_g2, *, tm, tn, emit_bf16=False):
    m = x.shape[0]
    out_specs = [
        pl.BlockSpec((tm, tn), lambda i, j: (i, j)),
        pl.BlockSpec((tm, D_GLA_K), lambda i, j: (i, 0)),
    ]
    out_shape = [
        jax.ShapeDtypeStruct((m, D_PROJ_MAIN), F32),
        jax.ShapeDtypeStruct((m, D_GLA_K), F32),
    ]
    if emit_bf16:
        assert m == tm
        out_specs.append(pl.BlockSpec((tn, D_MODEL), lambda i, j: (j, 0)))
        out_shape.append(jax.ShapeDtypeStruct((D_PROJ_MAIN, D_MODEL), BF16))
    return pl.pallas_call(
        functools.partial(_proj_kernel, emit_bf16=emit_bf16),
        grid=(m // tm, D_PROJ_MAIN // tn),
        in_specs=[
            pl.BlockSpec((tm, D_MODEL), lambda i, j: (i, 0)),
            pl.BlockSpec((1, D_MODEL), lambda i, j: (0, 0)),
            pl.BlockSpec((tn, D_MODEL), lambda i, j: (j, 0)),
            pl.BlockSpec((GATE_RANK, D_MODEL), lambda i, j: (0, 0)),
            pl.BlockSpec((GATE_RANK, D_GLA_K), lambda i, j: (0, 0)),
            pl.BlockSpec((1, D_GLA_K), lambda i, j: (0, 0)),
        ],
        out_specs=out_specs,
        out_shape=out_shape,
        scratch_shapes=[pltpu.VMEM((tm, D_MODEL), BF16)],
        compiler_params=_params("parallel", "arbitrary"),
        name="proj",
    )(x, g, w_t, w_lr_t, w_g2, b_g2)


def _attn_kernel(q_ref, k_ref, v_ref, o_ref, kt_ref, vt_ref,
                 qp_ref, kp_ref, vp_ref, m_ref, l_ref, acc_ref, s_ref, ms_ref):
    blk = ATT_BLOCK
    for src_ref, dst_ref in ((k_ref, kt_ref), (v_ref, vt_ref)):
        t = src_ref[...].T
        dst_ref[0, 0] = t[:HD_ATT]
        dst_ref[0, 1] = t[HD_ATT:]

    scale = HD_ATT ** -0.5
    seq = q_ref.shape[0]
    ns = DIL_PATTERNS[-1][1]
    lane_lo = lax.broadcasted_iota(jnp.int32, (1, 2 * HD_ATT), 1) < HD_ATT
    neg = jnp.float32(-jnp.inf)

    assert ns == 16
    for src_ref, tmp_ref, dst_ref in ((q_ref, m_ref, qp_ref), (k_ref, l_ref, kp_ref), (v_ref, acc_ref, vp_ref)):
        for r4 in range(4):
            x = src_ref[pl.ds(r4, seq // 4, stride=4), :]
            tmp_ref[pl.ds(r4 * (seq // 4), seq // 4), :] = x * scale if src_ref is q_ref else x
        for r4 in range(4):
            for a in range(4):
                dst_ref[pl.ds((4 * a + r4) * blk, blk), :] = tmp_ref[pl.ds(r4 * (seq // 4) + a, blk, stride=4), :]

    def run_pattern(d, first_pattern):
        na = ns // d
        plen = blk // na
        per_stream = seq // (d * blk)

        def offset(idx):
            return na * (idx & (plen - 1)) + idx // plen

        qpos = offset(lax.broadcasted_iota(jnp.int32, (blk, blk), 0))
        kpos = offset(lax.broadcasted_iota(jnp.int32, (blk, blk), 1))
        bias_cur = jnp.where(kpos <= qpos, 0.0, neg)
        if per_stream > 1:
            bias_prev = jnp.where(kpos >= qpos, 0.0, neg)
            bias_band = jnp.concatenate([bias_prev, bias_cur], axis=1)
            bias_first = jnp.concatenate([jnp.full((blk, blk), neg, F32), bias_cur], axis=1)

        def pieces(rd, n):
            return [pl.ds((a * d + rd) * blk + plen * n, plen) for a in range(na)]

        def gather(ref, ps):
            return jnp.concatenate([ref[p, :] for p in ps], axis=0)

        nk = 2 * blk if per_stream > 1 else blk
        heads = (lane_lo, jnp.logical_not(lane_lo))

        def keys(ref, rd, n):
            cur = gather(ref, pieces(rd, n))
            if per_stream == 1:
                return cur.astype(BF16)
            prev = gather(ref, pieces(rd, max(n - 1, 0)))
            return jnp.concatenate([prev, cur], axis=0).astype(BF16)

        def group(g):
            blocks = [((g * ATT_GROUP + b) % d, (g * ATT_GROUP + b) // d) for b in range(ATT_GROUP)]
            for b, (rd, n) in enumerate(blocks):
                q = gather(qp_ref, pieces(rd, n))
                kk = keys(kp_ref, rd, n)
                bias = bias_cur if per_stream == 1 else (bias_band if n > 0 else bias_first)
                for h, sel in enumerate(heads):
                    qh = jnp.where(sel, q, 0.0).astype(BF16)
                    s_ref[b, h, :, :nk] = _dot_nt(qh, kk) + bias
            for b in range(ATT_GROUP):
                for h in range(2):
                    m = jnp.max(s_ref[b, h, :, :nk], axis=-1, keepdims=True)
                    ms_ref[b, h] = jnp.broadcast_to(m, (blk, blk))
            for b, (rd, n) in enumerate(blocks):
                vv = jnp.concatenate([keys(vp_ref, rd, n), jnp.ones((nk, blk), BF16)], axis=1)
                res = []
                for h in range(2):
                    mh = ms_ref[b, h]
                    mh = jnp.concatenate([mh, mh], axis=1) if nk == 2 * blk else mh
                    res.append(_dot(jnp.exp(s_ref[b, h, :, :nk] - mh).astype(BF16), vv))
                u = jnp.where(lane_lo, res[0][:, :blk], res[1][:, :blk])
                l = jnp.where(lane_lo, res[0][:, blk:], res[1][:, blk:])
                m = jnp.where(lane_lo, ms_ref[b, 0], ms_ref[b, 1])
                for a, p in enumerate(pieces(rd, n)):
                    sl = slice(a * plen, (a + 1) * plen)
                    if first_pattern:
                        m_ref[p, :] = m[sl]
                        l_ref[p, :] = l[sl]
                        acc_ref[p, :] = u[sl]
                    else:
                        m_old = m_ref[p, :]
                        m_new = jnp.maximum(m_old, m[sl])
                        a_old = jnp.exp(m_old - m_new)
                        a_blk = jnp.exp(m[sl] - m_new)
                        m_ref[p, :] = m_new
                        l_ref[p, :] = a_old * l_ref[p, :] + a_blk * l[sl]
                        acc_ref[p, :] = a_old * acc_ref[p, :] + a_blk * u[sl]

        for g in range(d * per_stream // ATT_GROUP):
            group(g)

    for idx, (_, d) in enumerate(DIL_PATTERNS):
        run_pattern(d, idx == 0)

    for r4 in range(4):
        for a in range(4):
            rows = pl.ds((4 * a + r4) * blk, blk)
            qp_ref[pl.ds(r4 * (seq // 4) + a, blk, stride=4), :] = acc_ref[rows, :] / l_ref[rows, :]
    for r4 in range(4):
        o_ref[pl.ds(r4, seq // 4, stride=4), :] = qp_ref[pl.ds(r4 * (seq // 4), seq // 4), :]


def _attn_prompt(p, batch, seq):
    ns = DIL_PATTERNS[-1][1]
    assert seq == ns * ATT_BLOCK
    assert all(w // d == ATT_BLOCK and ns % d == 0 and ATT_BLOCK * d // ns >= 8 for w, d in DIL_PATTERNS)
    lanes = 2 * HD_ATT
    n_pairs = H_ATT // 2
    return pl.pallas_call(
        _attn_kernel,
        grid=(batch, n_pairs),
        in_specs=[
            pl.BlockSpec((seq, lanes), lambda b, h: (b, COL_QA // lanes + h)),
            pl.BlockSpec((seq, lanes), lambda b, h: (b, COL_KA // lanes + h)),
            pl.BlockSpec((seq, lanes), lambda b, h: (b, COL_VA // lanes + h)),
        ],
        out_specs=[
            pl.BlockSpec((seq, lanes), lambda b, h: (b, h)),
            pl.BlockSpec((1, 2, HD_ATT, seq), lambda b, h: (b, h, 0, 0)),
            pl.BlockSpec((1, 2, HD_ATT, seq), lambda b, h: (b, h, 0, 0)),
        ],
        out_shape=[
            jax.ShapeDtypeStruct((batch * seq, D_ATT), F32),
            jax.ShapeDtypeStruct((batch, H_ATT, HD_ATT, seq), F32),
            jax.ShapeDtypeStruct((batch, H_ATT, HD_ATT, seq), F32),
        ],
        scratch_shapes=[pltpu.VMEM((seq, lanes), F32)] * 6 + [
            pltpu.VMEM((ATT_GROUP, 2, ATT_BLOCK, 2 * ATT_BLOCK), F32),
            pltpu.VMEM((ATT_GROUP, 2, ATT_BLOCK, lanes), F32),
        ],
        compiler_params=_params("parallel", "parallel"),
        name="attn_prompt",
    )(p, p, p)


_GLA_LEVELS = (32, 16, 8, 4, 2, 1)


def _gla_exponent_matrix():
    c = GLA_CHUNK
    t = np.arange(c)[:, None]
    u = np.arange(c)[None, :]
    mats = [(u <= t), (u > t)]
    for h in _GLA_LEVELS:
        mid = (t // (2 * h)) * (2 * h) + h - 1
        upper = (t % (2 * h)) >= h
        mats.append(np.where(upper, (u > mid) & (u <= t), (u > t) & (u <= mid)))
    tmat = np.concatenate(mats, axis=0).astype(np.float32)
    return np.concatenate([tmat, tmat], axis=1)


def _gla_kernel(q_ref, k_ref, v_ref, g_ref, t_ref, o_ref, s_ref, st_ref, f_ref, a_ref, kv_ref):
    c = GLA_CHUNK
    scale = DK_GLA ** -0.5
    ti = lax.broadcasted_iota(jnp.int32, (c, c), 0)
    si = lax.broadcasted_iota(jnp.int32, (c, c), 1)
    txs = ti ^ si
    below = ti > si
    level_masks = [below & (txs >= h) & (txs < 2 * h) for h in _GLA_LEVELS]
    diag = ti == si

    st_ref[...] = jnp.zeros_like(st_ref)

    def group(gi, carry):
        rows = [pl.ds(pl.multiple_of((gi * GLA_GROUP + i) * c, c), c) for i in range(GLA_GROUP)]
        for i, r in enumerate(rows):
            g = g_ref[r, :]
            g_hi = g.astype(BF16)
            g_lo = (g - g_hi.astype(F32)).astype(BF16)
            f_ref[i] = jnp.exp(_dot(t_ref[...], jnp.concatenate([g_hi, g_lo], axis=0)))
        for i, r in enumerate(rows):
            q = q_ref[r, :] * scale
            k = k_ref[r, :]
            a = jnp.where(diag, _dot_nt(q.astype(BF16), k.astype(BF16)), 0.0)
            for lvl in range(len(_GLA_LEVELS)):
                fl = f_ref[i, 2 * c + lvl * c: 3 * c + lvl * c, :]
                a = a + jnp.where(level_masks[lvl],
                                  _dot_nt((q * fl).astype(BF16), (k * fl).astype(BF16)), 0.0)
            a_ref[i] = a.astype(BF16)
        for i, r in enumerate(rows):
            v = v_ref[r, :].astype(BF16)
            kd = (k_ref[r, :] * f_ref[i, c:2 * c, :]).astype(BF16)
            kv_ref[i] = _dot_tn(v, kd)
            o_ref[r, :] = _dot(a_ref[i], v)
        for i, r in enumerate(rows):
            st = st_ref[...]
            qe = (q_ref[r, :] * scale * f_ref[i, 0:c, :]).astype(BF16)
            o_ref[r, :] += _dot_nt(qe, st.astype(BF16))
            st_ref[...] = st * f_ref[i, c - 1:c, :] + kv_ref[i]
        return carry

    lax.fori_loop(0, q_ref.shape[0] // (c * GLA_GROUP), group, 0)
    s_ref[0, 0] = st_ref[...].T


def _gla_prompt(p, la, tmat, batch, seq):
    return pl.pallas_call(
        _gla_kernel,
        grid=(batch, H_GLA),
        in_specs=[
            pl.BlockSpec((seq, DK_GLA), lambda b, h: (b, COL_QG // DK_GLA + h)),
            pl.BlockSpec((seq, DK_GLA), lambda b, h: (b, COL_KG // DK_GLA + h)),
            pl.BlockSpec((seq, DV_GLA), lambda b, h: (b, COL_VG // DV_GLA + h)),
            pl.BlockSpec((seq, DK_GLA), lambda b, h: (b, h)),
            pl.BlockSpec(tmat.shape, lambda b, h: (0, 0)),
        ],
        out_specs=[
            pl.BlockSpec((seq, DV_GLA), lambda b, h: (b, h)),
            pl.BlockSpec((1, 1, DK_GLA, DV_GLA), lambda b, h: (b, h, 0, 0)),
        ],
        out_shape=[
            jax.ShapeDtypeStruct((batch * seq, D_GLA_V), F32),
            jax.ShapeDtypeStruct((batch, H_GLA, DK_GLA, DV_GLA), F32),
        ],
        scratch_shapes=[pltpu.VMEM((DV_GLA, DK_GLA), F32),
                        pltpu.VMEM((GLA_GROUP,) + (tmat.shape[0], DK_GLA), F32),
                        pltpu.VMEM((GLA_GROUP, GLA_CHUNK, GLA_CHUNK), BF16),
                        pltpu.VMEM((GLA_GROUP, DV_GLA, DK_GLA), F32)],
        compiler_params=_params("parallel", "parallel"),
        name="gla_prompt",
    )(p, p, p, la, tmat)


def _merge_kernel(x_ref, oa_ref, og_ref, rg_ref, ga_ref, gg_ref, w_ref, o_ref):
    a = (_rms(oa_ref[...]) * ga_ref[...]).astype(BF16)
    parts = []
    for h in range(H_GLA):
        cols = slice(h * DV_GLA, (h + 1) * DV_GLA)
        r = rg_ref[:, cols]
        parts.append((_rms(og_ref[:, cols]) * gg_ref[...] * (r * jax.nn.sigmoid(r))).astype(BF16))
    gg = jnp.concatenate(parts, axis=-1)
    o_ref[...] = x_ref[...] + _dot(a, w_ref[0:D_ATT, :]) + _dot(gg, w_ref[D_ATT:, :])


def _merge(x, o_att, o_gla, p, g_att, g_gla, w_out, *, tm):
    m = x.shape[0]
    return pl.pallas_call(
        _merge_kernel,
        grid=(m // tm,),
        in_specs=[
            pl.BlockSpec((tm, D_MODEL), lambda i: (i, 0)),
            pl.BlockSpec((tm, D_ATT), lambda i: (i, 0)),
            pl.BlockSpec((tm, D_GLA_V), lambda i: (i, 0)),
            pl.BlockSpec((tm, D_GLA_V), lambda i: (i, COL_RG // D_GLA_V)),
            pl.BlockSpec((1, D_ATT), lambda i: (0, 0)),
            pl.BlockSpec((1, DV_GLA), lambda i: (0, 0)),
            pl.BlockSpec((D_MODEL, D_MODEL), lambda i: (0, 0)),
        ],
        out_specs=pl.BlockSpec((tm, D_MODEL), lambda i: (i, 0)),
        out_shape=jax.ShapeDtypeStruct((m, D_MODEL), F32),
        compiler_params=_params("parallel"),
        name="merge",
    )(x, o_att, o_gla, p, g_att, g_gla, w_out)


SATT_HEADS_PER_STEP = 4


def _sattn_heads(q, kn, vn, k_ref, v_ref, s_ref):
    scale = HD_ATT ** -0.5
    wb = k_ref.shape[-1]
    t = lax.broadcasted_iota(jnp.int32, (1, wb), 1)
    cnt = jnp.zeros((1, wb), F32)
    for w, d in DIL_PATTERNS:
        cnt = cnt + jnp.where((t >= wb - w) & ((t & (d - 1)) == 0), 1.0, 0.0)
    bias = jnp.where(cnt > 0.0, 0.0, jnp.float32(-jnp.inf))
    n_pat = float(len(DIL_PATTERNS))
    n_heads = k_ref.shape[1]
    q = q * scale
    q16 = q.astype(BF16)
    s_new = jnp.sum(q * kn, axis=-1, keepdims=True)
    for h in range(n_heads):
        s_ref[h:h + 1, :] = _dot(q16, k_ref[0, h].astype(BF16))[h:h + 1, :]
    s = s_ref[...] + bias
    m = jnp.maximum(jnp.max(s, axis=-1, keepdims=True), s_new)
    e = cnt * jnp.exp(s - m)
    e_new = n_pat * jnp.exp(s_new - m)
    den = jnp.sum(e, axis=-1, keepdims=True) + e_new
    e16 = e.astype(BF16)
    head = lax.broadcasted_iota(jnp.int32, (n_heads, HD_ATT), 0)
    num = e_new * vn
    for h in range(n_heads):
        num = num + jnp.where(head == h, _dot_nt(e16, v_ref[0, h].astype(BF16)), 0.0)
    return num / den


SGLA_SEQS_PER_STEP = 4
_SPLIT = 3


def _bf16_pieces(x):
    pieces = []
    for _ in range(_SPLIT):
        p = x.astype(BF16)
        pieces.append(p)
        x = x - p.astype(F32)
    return pieces


def _sgla_kernel(q_ref, k_ref, g_ref, v_ref, s_ref, so_ref, o_ref):
    scale = DK_GLA ** -0.5
    n_vec = 3
    sel_rows = 16
    r = lax.broadcasted_iota(jnp.int32, (sel_rows, n_vec * DV_GLA), 0)
    c = lax.broadcasted_iota(jnp.int32, (sel_rows, n_vec * DV_GLA), 1)
    selector = jnp.where((r // _SPLIT == c // DV_GLA) & (r < n_vec * _SPLIT), 1.0, 0.0).astype(BF16)
    pad = jnp.zeros((sel_rows - n_vec * _SPLIT, DK_GLA), BF16)
    for s in range(q_ref.shape[0]):
        for h in range(H_GLA):
            hrow = slice(h, h + 1)
            vecs = (jnp.exp(g_ref[s, hrow, :]), k_ref[s, hrow, :], q_ref[s, hrow, :] * scale)
            lhs = jnp.concatenate([p for x in vecs for p in _bf16_pieces(x)] + [pad], axis=0)
            cols = _dot_tn(lhs, selector)
            decay, kcol, qcol = (cols[:, i * DV_GLA:(i + 1) * DV_GLA] for i in range(n_vec))
            s_new = decay * s_ref[s, h] + kcol * v_ref[s, hrow, :]
            so_ref[s, h] = s_new
            o_ref[s, hrow, :] = jnp.sum(qcol * s_new, axis=0, keepdims=True)


def _gla_sample(q, k, g, v, state):
    nb = state.shape[0]
    bs = SGLA_SEQS_PER_STEP
    krow = pl.BlockSpec((bs, H_GLA, DK_GLA), lambda b: (b, 0, 0))
    vrow = pl.BlockSpec((bs, H_GLA, DV_GLA), lambda b: (b, 0, 0))
    st = pl.BlockSpec((bs, H_GLA, DK_GLA, DV_GLA), lambda b: (b, 0, 0, 0))
    return pl.pallas_call(
        _sgla_kernel,
        grid=(nb // bs,),
        in_specs=[krow, krow, krow, vrow, st],
        out_specs=[st, vrow],
        out_shape=[jax.ShapeDtypeStruct((nb, H_GLA, DK_GLA, DV_GLA), F32),
                   jax.ShapeDtypeStruct((nb, H_GLA, DV_GLA), F32)],
        compiler_params=_params("parallel"),
        name="gla_sample",
    )(q, k, g, v, state)


def kernel(x_prompt, x_sample, cache_att_k, cache_att_v, state_gla, g_ffn1, w_ffn1_in, w_ffn1_out, g_mix, w_in, w_gate2, b_gate2, g_att_out, g_gla_out, w_out, g_ffn2, w_ffn2_in, w_ffn2_out, g_final):
    depth = w_in.shape[0]
    assert depth == 1
    batch, seq, _ = x_prompt.shape
    nb, dec_seq, _ = x_sample.shape
    assert dec_seq == 1
    xp = x_prompt.reshape(batch * seq, D_MODEL)
    xs = x_sample.reshape(nb, D_MODEL)
    row = lambda a: a.reshape(1, -1)
    l = 0

    w_in_t = w_in[l].T
    w_lr = w_in_t[D_PROJ_MAIN:].astype(BF16)
    w_g2 = w_gate2[l].astype(BF16)
    b_g2 = row(b_gate2[l])
    w_o = w_out[l].astype(BF16)
    gf = row(g_final)
    tmat = jnp.asarray(_gla_exponent_matrix(), dtype=BF16)

    xs, w1a, w1b, w1o = _ffn(xs, row(g_ffn1[l]), w_ffn1_in[l], w_ffn1_in[l], w_ffn1_out[l], gf,
                             tm=nb, tf=FFN_TILE, final_norm=False, emit_bf16=True)
    ps, las, w_main = _proj(xs, row(g_mix[l]), w_in_t, w_lr, w_g2, b_g2, tm=nb, tn=512, emit_bf16=True)
    heads = lambda c0: ps[:, c0:c0 + D_ATT].reshape(nb, H_ATT, HD_ATT)
    k_new, v_new = heads(COL_KA), heads(COL_VA)
    xp, o_att_s, w2i, w2o = _ffn(
        xp, row(g_ffn1[l]), w1a, w1b, w1o, gf, tm=FFN_ROWS_SIDE, tf=FFN_TILE_SIDE, final_norm=False,
        side=(heads(COL_QA), k_new, v_new,
              jnp.transpose(cache_att_k[l], (0, 2, 3, 1)), jnp.transpose(cache_att_v[l], (0, 2, 3, 1)),
              w_ffn2_in[l], w_ffn2_out[l]))
    krows = lambda a: a.reshape(nb, H_GLA, DK_GLA)
    s_new, o_gla_s = _gla_sample(
        krows(ps[:, COL_QG:COL_QG + D_GLA_K]), krows(ps[:, COL_KG:COL_KG + D_GLA_K]), krows(las),
        ps[:, COL_VG:COL_VG + D_GLA_V].reshape(nb, H_GLA, DV_GLA), state_gla[l])
    xs = _merge(xs, o_att_s.reshape(nb, D_ATT), o_gla_s.reshape(nb, D_GLA_V), ps,
                row(g_att_out[l]), row(g_gla_out[l]), w_o, tm=nb)
    ys = _ffn(xs, row(g_ffn2[l]), w2i, w2i, w2o, gf, tm=nb, tf=FFN_TILE, final_norm=True)
    nk_s = k_new.reshape(1, nb, 1, H_ATT, HD_ATT)
    nv_s = v_new.reshape(1, nb, 1, H_ATT, HD_ATT)

    pp, lap = _proj(xp, row(g_mix[l]), w_main, w_lr, w_g2, b_g2, tm=1024, tn=1024)
    o_att, kt_p, vt_p = _attn_prompt(pp, batch, seq)
    o_gla, s_fin = _gla_prompt(pp, lap, tmat, batch, seq)
    xp = _merge(xp, o_att, o_gla, pp, row(g_att_out[l]), row(g_gla_out[l]), w_o, tm=512)
    yp = _ffn(xp, row(g_ffn2[l]), w2i, w2i, w2o, gf, tm=FFN_ROWS, tf=FFN_TILE_WIDE_ROWS, final_norm=True)
    nk_p = jnp.transpose(kt_p, (0, 3, 1, 2))[None]
    nv_p = jnp.transpose(vt_p, (0, 3, 1, 2))[None]

    return (yp.reshape(batch, seq, D_MODEL), ys.reshape(nb, 1, D_MODEL), nk_p, nv_p,
            s_fin[None], nk_s, nv_s, s_new[None])
```

```python
import functools

import numpy as np
import jax
import jax.numpy as jnp
from jax import lax
from jax.experimental import pallas as pl
from jax.experimental.pallas import tpu as pltpu

F32 = jnp.float32
BF16 = jnp.bfloat16

D_MODEL = 2048
D_FF = 5632
D_ATT = 1024
HD_ATT = 64
H_ATT = 16
H_GLA = 4
DK_GLA = 128
DV_GLA = 256
D_GLA_K = H_GLA * DK_GLA
D_GLA_V = H_GLA * DV_GLA
GATE_RANK = 16
GATE_TAU = 16.0
NORM_EPS = 1e-6
DIL_PATTERNS = ((128, 1), (512, 4), (2048, 16))
ATT_BLOCK = 128
ATT_GROUP = 16
GLA_CHUNK = 64
GLA_GROUP = 8
D_PROJ_MAIN = 3 * D_ATT + 2 * D_GLA_K + 2 * D_GLA_V

COL_QA, COL_KA, COL_VA = 0, D_ATT, 2 * D_ATT
COL_QG = 3 * D_ATT
COL_KG = COL_QG + D_GLA_K
COL_VG = COL_KG + D_GLA_K
COL_RG = COL_VG + D_GLA_V

VMEM_LIMIT_BYTES = 56 * 1024 * 1024

FFN_ROWS, FFN_TILE_WIDE_ROWS = 1024, 256
FFN_ROWS_SIDE, FFN_TILE_SIDE = FFN_ROWS, FFN_TILE_WIDE_ROWS
FFN_TILE = 512


def _rms(x):
    return x * lax.rsqrt(jnp.mean(x * x, axis=-1, keepdims=True) + NORM_EPS)


def _dot(a, b):
    return jnp.dot(a, b, preferred_element_type=F32)


def _dot_nt(a, b):
    return lax.dot_general(a, b, (((1,), (1,)), ((), ())), preferred_element_type=F32)


def _dot_tn(a, b):
    return lax.dot_general(a, b, (((0,), (0,)), ((), ())), preferred_element_type=F32)


def _params(*sem):
    return pltpu.CompilerParams(dimension_semantics=sem, vmem_limit_bytes=VMEM_LIMIT_BYTES)


def _ffn_step(x_ref, g_ref, wa, wb, wo, gf_ref, o_ref, h_ref, final_norm, side_jobs=(None, None)):
    j = pl.program_id(1)

    @pl.when(j == 0)
    def _():
        h_ref[...] = (_rms(x_ref[...]) * g_ref[...]).astype(BF16)
        o_ref[...] = jnp.zeros_like(o_ref)

    h = h_ref[...]
    a = _dot(h, wa)
    if side_jobs[0] is not None:
        side_jobs[0]()
    b = _dot(h, wb)
    if side_jobs[1] is not None:
        side_jobs[1]()
    act = (a * jax.nn.sigmoid(a) * b).astype(BF16)
    o_ref[...] += _dot(act, wo)

    @pl.when(j == pl.num_programs(1) - 1)
    def _():
        y = x_ref[...] + 0.5 * o_ref[...]
        if final_norm:
            y = _rms(y) * gf_ref[...]
        o_ref[...] = y


def _ffn_kernel(x_ref, g_ref, wa_ref, wb_ref, wo_ref, gf_ref, o_ref, *rest, final_norm, emit_bf16):
    wa, wb, wo = wa_ref[...], wb_ref[...], wo_ref[...]
    if emit_bf16:
        wa, wb, wo = wa.astype(BF16), wb.astype(BF16), wo.astype(BF16)
        for dst_ref, w in zip(rest[:3], (wa, wb, wo)):
            dst_ref[...] = w
    _ffn_step(x_ref, g_ref, wa, wb, wo, gf_ref, o_ref, rest[-1], final_norm)


def _ffn_side_kernel(x_ref, g_ref, wa_ref, wb_ref, wo_ref, gf_ref,
                     q_ref, kn_ref, vn_ref, k_ref, v_ref, cwi_ref, cwo_ref,
                     o_ref, so_ref, cwi16_ref, cwo16_ref, h_ref, s_ref, *, final_norm, n_side_blocks):
    hs = k_ref.shape[1]
    groups = H_ATT // hs
    step = pl.program_id(0) * pl.num_programs(1) + pl.program_id(1)
    blk = jnp.minimum(step, n_side_blocks - 1)
    heads = pl.ds(pl.multiple_of((blk % groups) * hs, hs), hs)

    def scores():
        _sattn_scores(q_ref[0, heads, :], k_ref, s_ref)
        cwi16_ref[...] = cwi_ref[...].astype(BF16)
        cwo16_ref[...] = cwo_ref[...].astype(BF16)

    def finish():
        so_ref[0, heads, :] = _sattn_finish(q_ref[0, heads, :], kn_ref[0, heads, :], vn_ref[0, heads, :],
                                            v_ref, s_ref)

    _ffn_step(x_ref, g_ref, wa_ref[...], wb_ref[...], wo_ref[...], gf_ref, o_ref, h_ref, final_norm,
              (scores, finish))


def _ffn(x, g, wa, wb, wo, g_final, *, tm, tf, final_norm, emit_bf16=False, side=None):
    m = x.shape[0]
    nf = D_FF // tf
    ni = m // tm
    b_off = nf if wb.shape[1] == 2 * D_FF else 0
    in_specs = [
        pl.BlockSpec((tm, D_MODEL), lambda i, j: (i, 0)),
        pl.BlockSpec((1, D_MODEL), lambda i, j: (0, 0)),
        pl.BlockSpec((D_MODEL, tf), lambda i, j: (0, j)),
        pl.BlockSpec((D_MODEL, tf), lambda i, j: (0, j + b_off)),
        pl.BlockSpec((tf, D_MODEL), lambda i, j: (j, 0)),
        pl.BlockSpec((1, D_MODEL), lambda i, j: (0, 0)),
    ]
    out_specs = [pl.BlockSpec((tm, D_MODEL), lambda i, j: (i, 0))]
    out_shape = [jax.ShapeDtypeStruct((m, D_MODEL), F32)]
    scratch = [pltpu.VMEM((tm, D_MODEL), BF16)]
    operands = (x, g, wa, wb, wo, g_final)
    if side is None:
        body = functools.partial(_ffn_kernel, final_norm=final_norm, emit_bf16=emit_bf16)
        sem = ("parallel", "arbitrary")
    if emit_bf16:
        assert m == tm and side is None
        out_specs += [pl.BlockSpec((D_MODEL, tf), lambda i, j: (0, j)),
                      pl.BlockSpec((D_MODEL, tf), lambda i, j: (0, j)),
                      pl.BlockSpec((tf, D_MODEL), lambda i, j: (j, 0))]
        out_shape += [jax.ShapeDtypeStruct((D_MODEL, D_FF), BF16),
                      jax.ShapeDtypeStruct((D_MODEL, D_FF), BF16),
                      jax.ShapeDtypeStruct((D_FF, D_MODEL), BF16)]
    if side is not None:
        q, kn, vn, cache_kt, cache_vt, cw_in, cw_out = side
        nb, _, _, wb_len = cache_kt.shape
        assert all(wb_len % d == 0 and w <= wb_len for w, d in DIL_PATTERNS)
        hs = SATT_HEADS_PER_STEP
        groups = H_ATT // hs
        n_blocks = nb * groups
        assert ni * nf >= n_blocks and D_MODEL % ni == 0 and (2 * D_FF) % nf == 0
        sblk = lambda i, j: jnp.minimum(i * nf + j, n_blocks - 1)
        seq_row = pl.BlockSpec((1, H_ATT, HD_ATT), lambda i, j: (sblk(i, j) // groups, 0, 0))
        cache = pl.BlockSpec((1, hs, HD_ATT, wb_len),
                             lambda i, j: (sblk(i, j) // groups, sblk(i, j) % groups, 0, 0))
        cwi = pl.BlockSpec((D_MODEL // ni, 2 * D_FF // nf), lambda i, j: (i, j))
        cwo = pl.BlockSpec((D_FF // nf, D_MODEL // ni), lambda i, j: (j, i))
        in_specs += [seq_row, seq_row, seq_row, cache, cache, cwi, cwo]
        out_specs += [seq_row, cwi, cwo]
        out_shape += [jax.ShapeDtypeStruct((nb, H_ATT, HD_ATT), F32),
                      jax.ShapeDtypeStruct(cw_in.shape, BF16),
                      jax.ShapeDtypeStruct(cw_out.shape, BF16)]
        scratch.append(pltpu.VMEM((hs, wb_len), F32))
        operands += (q, kn, vn, cache_kt, cache_vt, cw_in, cw_out)
        body = functools.partial(_ffn_side_kernel, final_norm=final_norm, n_side_blocks=n_blocks)
        sem = ("arbitrary", "arbitrary")
    outs = pl.pallas_call(
        body,
        grid=(ni, nf),
        in_specs=in_specs,
        out_specs=out_specs,
        out_shape=out_shape,
        scratch_shapes=scratch,
        compiler_params=_params(*sem),
        name="ffn",
    )(*operands)
    return outs if (emit_bf16 or side is not None) else outs[0]


def _proj_kernel(x_ref, g_ref, w_ref, wlr_ref, wg2_ref, bg2_ref, p_ref, la_ref, *rest, emit_bf16):
    h_ref = rest[-1]
    j = pl.program_id(1)

    @pl.when(j == 0)
    def _():
        h = (_rms(x_ref[...]) * g_ref[...]).astype(BF16)
        h_ref[...] = h
        lr = _dot_nt(h, wlr_ref[...])
        z = _dot(lr.astype(BF16), wg2_ref[...]) + bg2_ref[...]
        log_sig = jnp.minimum(z, 0.0) - jnp.log1p(jnp.exp(-jnp.abs(z)))
        la_ref[...] = log_sig * (1.0 / GATE_TAU)

    w = w_ref[...]
    if emit_bf16:
        w = w.astype(BF16)
        rest[0][...] = w
    p_ref[...] = _dot_nt(h_ref[...], w)


def _proj(x, g, w_t, w_lr_t, w_g2, b_g2, *, tm, tn, emit_bf16=False):
    m = x.shape[0]
    out_specs = [
        pl.BlockSpec((tm, tn), lambda i, j: (i, j)),
        pl.BlockSpec((tm, D_GLA_K), lambda i, j: (i, 0)),
    ]
    out_shape = [
        jax.ShapeDtypeStruct((m, D_PROJ_MAIN), F32),
        jax.ShapeDtypeStruct((m, D_GLA_K), F32),
    ]
    if emit_bf16:
        assert m == tm
        out_specs.append(pl.BlockSpec((tn, D_MODEL), lambda i, j: (j, 0)))
        out_shape.append(jax.ShapeDtypeStruct((D_PROJ_MAIN, D_MODEL), BF16))
    return pl.pallas_call(
        functools.partial(_proj_kernel, emit_bf16=emit_bf16),
        grid=(m // tm, D_PROJ_MAIN // tn),
        in_specs=[
            pl.BlockSpec((tm, D_MODEL), lambda i, j: (i, 0)),
            pl.BlockSpec((1, D_MODEL), lambda i, j: (0, 0)),
            pl.BlockSpec((tn, D_MODEL), lambda i, j: (j, 0)),
            pl.BlockSpec((GATE_RANK, D_MODEL), lambda i, j: (0, 0)),
            pl.BlockSpec((GATE_RANK, D_GLA_K), lambda i, j: (0, 0)),
            pl.BlockSpec((1, D_GLA_K), lambda i, j: (0, 0)),
        ],
        out_specs=out_specs,
        out_shape=out_shape,
        scratch_shapes=[pltpu.VMEM((tm, D_MODEL), BF16)],
        compiler_params=_params("parallel", "arbitrary"),
        name="proj",
    )(x, g, w_t, w_lr_t, w_g2, b_g2)


def _attn_kernel(q_ref, k_ref, v_ref, o_ref, kt_ref, vt_ref,
                 qp_ref, kp_ref, vp_ref, m_ref, l_ref, acc_ref, s_ref, ms_ref):
    blk = ATT_BLOCK
    for src_ref, dst_ref in ((k_ref, kt_ref), (v_ref, vt_ref)):
        t = src_ref[...].T
        dst_ref[0, 0] = t[:HD_ATT]
        dst_ref[0, 1] = t[HD_ATT:]

    scale = HD_ATT ** -0.5
    seq = q_ref.shape[0]
    ns = DIL_PATTERNS[-1][1]
    lane_lo = lax.broadcasted_iota(jnp.int32, (1, 2 * HD_ATT), 1) < HD_ATT
    neg = jnp.float32(-jnp.inf)

    assert ns == 16
    for src_ref, tmp_ref, dst_ref in ((q_ref, m_ref, qp_ref), (k_ref, l_ref, kp_ref), (v_ref, acc_ref, vp_ref)):
        for r4 in range(4):
            x = src_ref[pl.ds(r4, seq // 4, stride=4), :]
            tmp_ref[pl.ds(r4 * (seq // 4), seq // 4), :] = x * scale if src_ref is q_ref else x
        for r4 in range(4):
            for a in range(4):
                dst_ref[pl.ds((4 * a + r4) * blk, blk), :] = tmp_ref[pl.ds(r4 * (seq // 4) + a, blk, stride=4), :]

    def run_pattern(d, first_pattern):
        na = ns // d
        plen = blk // na
        per_stream = seq // (d * blk)

        def offset(idx):
            return na * (idx & (plen - 1)) + idx // plen

        qpos = offset(lax.broadcasted_iota(jnp.int32, (blk, blk), 0))
        kpos = offset(lax.broadcasted_iota(jnp.int32, (blk, blk), 1))
        bias_cur = jnp.where(kpos <= qpos, 0.0, neg)
        if per_stream > 1:
            bias_prev = jnp.where(kpos >= qpos, 0.0, neg)
            bias_band = jnp.concatenate([bias_prev, bias_cur], axis=1)
            bias_first = jnp.concatenate([jnp.full((blk, blk), neg, F32), bias_cur], axis=1)

        def pieces(rd, n):
            return [pl.ds((a * d + rd) * blk + plen * n, plen) for a in range(na)]

        def gather(ref, ps):
            return jnp.concatenate([ref[p, :] for p in ps], axis=0)

        nk = 2 * blk if per_stream > 1 else blk
        heads = (lane_lo, jnp.logical_not(lane_lo))

        def keys(ref, rd, n):
            cur = gather(ref, pieces(rd, n))
            if per_stream == 1:
                return cur.astype(BF16)
            prev = gather(ref, pieces(rd, max(n - 1, 0)))
            return jnp.concatenate([prev, cur], axis=0).astype(BF16)

        def group(g):
            blocks = [((g * ATT_GROUP + b) % d, (g * ATT_GROUP + b) // d) for b in range(ATT_GROUP)]
            for b, (rd, n) in enumerate(blocks):
                q = gather(qp_ref, pieces(rd, n))
                kk = keys(kp_ref, rd, n)
                bias = bias_cur if per_stream == 1 else (bias_band if n > 0 else bias_first)
                for h, sel in enumerate(heads):
                    qh = jnp.where(sel, q, 0.0).astype(BF16)
                    s_ref[b, h, :, :nk] = _dot_nt(qh, kk) + bias
            for b in range(ATT_GROUP):
                for h in range(2):
                    m = jnp.max(s_ref[b, h, :, :nk], axis=-1, keepdims=True)
                    ms_ref[b, h] = jnp.broadcast_to(m, (blk, blk))
            for b, (rd, n) in enumerate(blocks):
                vv = jnp.concatenate([keys(vp_ref, rd, n), jnp.ones((nk, blk), BF16)], axis=1)
                res = []
                for h in range(2):
                    mh = ms_ref[b, h]
                    mh = jnp.concatenate([mh, mh], axis=1) if nk == 2 * blk else mh
                    res.append(_dot(jnp.exp(s_ref[b, h, :, :nk] - mh).astype(BF16), vv))
                u = jnp.where(lane_lo, res[0][:, :blk], res[1][:, :blk])
                l = jnp.where(lane_lo, res[0][:, blk:], res[1][:, blk:])
                m = jnp.where(lane_lo, ms_ref[b, 0], ms_ref[b, 1])
                for a, p in enumerate(pieces(rd, n)):
                    sl = slice(a * plen, (a + 1) * plen)
                    if first_pattern:
                        m_ref[p, :] = m[sl]
                        l_ref[p, :] = l[sl]
                        acc_ref[p, :] = u[sl]
                    else:
                        m_old = m_ref[p, :]
                        m_new = jnp.maximum(m_old, m[sl])
                        a_old = jnp.exp(m_old - m_new)
                        a_blk = jnp.exp(m[sl] - m_new)
                        m_ref[p, :] = m_new
                        l_ref[p, :] = a_old * l_ref[p, :] + a_blk * l[sl]
                        acc_ref[p, :] = a_old * acc_ref[p, :] + a_blk * u[sl]

        for g in range(d * per_stream // ATT_GROUP):
            group(g)

    for idx, (_, d) in enumerate(DIL_PATTERNS):
        run_pattern(d, idx == 0)

    for r4 in range(4):
        for a in range(4):
            rows = pl.ds((4 * a + r4) * blk, blk)
            qp_ref[pl.ds(r4 * (seq // 4) + a, blk, stride=4), :] = acc_ref[rows, :] / l_ref[rows, :]
    for r4 in range(4):
        o_ref[pl.ds(r4, seq // 4, stride=4), :] = qp_ref[pl.ds(r4 * (seq // 4), seq // 4), :]


def _attn_prompt(p, batch, seq):
    ns = DIL_PATTERNS[-1][1]
    assert seq == ns * ATT_BLOCK
    assert all(w // d == ATT_BLOCK and ns % d == 0 and ATT_BLOCK * d // ns >= 8 for w, d in DIL_PATTERNS)
    lanes = 2 * HD_ATT
    n_pairs = H_ATT // 2
    return pl.pallas_call(
        _attn_kernel,
        grid=(batch, n_pairs),
        in_specs=[
            pl.BlockSpec((seq, lanes), lambda b, h: (b, COL_QA // lanes + h)),
            pl.BlockSpec((seq, lanes), lambda b, h: (b, COL_KA // lanes + h)),
            pl.BlockSpec((seq, lanes), lambda b, h: (b, COL_VA // lanes + h)),
        ],
        out_specs=[
            pl.BlockSpec((seq, lanes), lambda b, h: (b, h)),
            pl.BlockSpec((1, 2, HD_ATT, seq), lambda b, h: (b, h, 0, 0)),
            pl.BlockSpec((1, 2, HD_ATT, seq), lambda b, h: (b, h, 0, 0)),
        ],
        out_shape=[
            jax.ShapeDtypeStruct((batch * seq, D_ATT), F32),
            jax.ShapeDtypeStruct((batch, H_ATT, HD_ATT, seq), F32),
            jax.ShapeDtypeStruct((batch, H_ATT, HD_ATT, seq), F32),
        ],
        scratch_shapes=[pltpu.VMEM((seq, lanes), F32)] * 6 + [
            pltpu.VMEM((ATT_GROUP, 2, ATT_BLOCK, 2 * ATT_BLOCK), F32),
            pltpu.VMEM((ATT_GROUP, 2, ATT_BLOCK, lanes), F32),
        ],
        compiler_params=_params("parallel", "parallel"),
        name="attn_prompt",
    )(p, p, p)


_GLA_LEVELS = (32, 16, 8, 4, 2, 1)


def _gla_exponent_matrix():
    c = GLA_CHUNK
    t = np.arange(c)[:, None]
    u = np.arange(c)[None, :]
    mats = [(u <= t), (u > t)]
    for h in _GLA_LEVELS:
        mid = (t // (2 * h)) * (2 * h) + h - 1
        upper = (t % (2 * h)) >= h
        mats.append(np.where(upper, (u > mid) & (u <= t), (u > t) & (u <= mid)))
    tmat = np.concatenate(mats, axis=0).astype(np.float32)
    return np.concatenate([tmat, tmat], axis=1)


def _gla_kernel(q_ref, k_ref, v_ref, g_ref, t_ref, o_ref, s_ref, st_ref, f_ref, a_ref, kv_ref):
    c = GLA_CHUNK
    scale = DK_GLA ** -0.5
    ti = lax.broadcasted_iota(jnp.int32, (c, c), 0)
    si = lax.broadcasted_iota(jnp.int32, (c, c), 1)
    txs = ti ^ si
    below = ti > si
    level_masks = [below & (txs >= h) & (txs < 2 * h) for h in _GLA_LEVELS]
    diag = ti == si

    st_ref[...] = jnp.zeros_like(st_ref)

    def group(gi, carry):
        rows = [pl.ds(pl.multiple_of((gi * GLA_GROUP + i) * c, c), c) for i in range(GLA_GROUP)]
        for i, r in enumerate(rows):
            g = g_ref[r, :]
            g_hi = g.astype(BF16)
            g_lo = (g - g_hi.astype(F32)).astype(BF16)
            f_ref[i] = jnp.exp(_dot(t_ref[...], jnp.concatenate([g_hi, g_lo], axis=0)))
        for i, r in enumerate(rows):
            q = q_ref[r, :] * scale
            k = k_ref[r, :]
            a = jnp.where(diag, _dot_nt(q.astype(BF16), k.astype(BF16)), 0.0)
            for lvl in range(len(_GLA_LEVELS)):
                fl = f_ref[i, 2 * c + lvl * c: 3 * c + lvl * c, :]
                a = a + jnp.where(level_masks[lvl],
                                  _dot_nt((q * fl).astype(BF16), (k * fl).astype(BF16)), 0.0)
            a_ref[i] = a.astype(BF16)
        for i, r in enumerate(rows):
            v = v_ref[r, :].astype(BF16)
            kd = (k_ref[r, :] * f_ref[i, c:2 * c, :]).astype(BF16)
            kv_ref[i] = _dot_tn(v, kd)
            o_ref[r, :] = _dot(a_ref[i], v)
        for i, r in enumerate(rows):
            st = st_ref[...]
            qe = (q_ref[r, :] * scale * f_ref[i, 0:c, :]).astype(BF16)
            o_ref[r, :] += _dot_nt(qe, st.astype(BF16))
            st_ref[...] = st * f_ref[i, c - 1:c, :] + kv_ref[i]
        return carry

    lax.fori_loop(0, q_ref.shape[0] // (c * GLA_GROUP), group, 0)
    s_ref[0, 0] = st_ref[...].T


def _gla_prompt(p, la, tmat, batch, seq):
    return pl.pallas_call(
        _gla_kernel,
        grid=(batch, H_GLA),
        in_specs=[
            pl.BlockSpec((seq, DK_GLA), lambda b, h: (b, COL_QG // DK_GLA + h)),
            pl.BlockSpec((seq, DK_GLA), lambda b, h: (b, COL_KG // DK_GLA + h)),
            pl.BlockSpec((seq, DV_GLA), lambda b, h: (b, COL_VG // DV_GLA + h)),
            pl.BlockSpec((seq, DK_GLA), lambda b, h: (b, h)),
            pl.BlockSpec(tmat.shape, lambda b, h: (0, 0)),
        ],
        out_specs=[
            pl.BlockSpec((seq, DV_GLA), lambda b, h: (b, h)),
            pl.BlockSpec((1, 1, DK_GLA, DV_GLA), lambda b, h: (b, h, 0, 0)),
        ],
        out_shape=[
            jax.ShapeDtypeStruct((batch * seq, D_GLA_V), F32),
            jax.ShapeDtypeStruct((batch, H_GLA, DK_GLA, DV_GLA), F32),
        ],
        scratch_shapes=[pltpu.VMEM((DV_GLA, DK_GLA), F32),
                        pltpu.VMEM((GLA_GROUP,) + (tmat.shape[0], DK_GLA), F32),
                        pltpu.VMEM((GLA_GROUP, GLA_CHUNK, GLA_CHUNK), BF16),
                        pltpu.VMEM((GLA_GROUP, DV_GLA, DK_GLA), F32)],
        compiler_params=_params("parallel", "parallel"),
        name="gla_prompt",
    )(p, p, p, la, tmat)


def _merge_kernel(x_ref, oa_ref, og_ref, rg_ref, ga_ref, gg_ref, w_ref, o_ref):
    a = (_rms(oa_ref[...]) * ga_ref[...]).astype(BF16)
    parts = []
    for h in range(H_GLA):
        cols = slice(h * DV_GLA, (h + 1) * DV_GLA)
        r = rg_ref[:, cols]
        parts.append((_rms(og_ref[:, cols]) * gg_ref[...] * (r * jax.nn.sigmoid(r))).astype(BF16))
    gg = jnp.concatenate(parts, axis=-1)
    o_ref[...] = x_ref[...] + _dot(a, w_ref[0:D_ATT, :]) + _dot(gg, w_ref[D_ATT:, :])


def _merge(x, o_att, o_gla, p, g_att, g_gla, w_out, *, tm):
    m = x.shape[0]
    return pl.pallas_call(
        _merge_kernel,
        grid=(m // tm,),
        in_specs=[
            pl.BlockSpec((tm, D_MODEL), lambda i: (i, 0)),
            pl.BlockSpec((tm, D_ATT), lambda i: (i, 0)),
            pl.BlockSpec((tm, D_GLA_V), lambda i: (i, 0)),
            pl.BlockSpec((tm, D_GLA_V), lambda i: (i, COL_RG // D_GLA_V)),
            pl.BlockSpec((1, D_ATT), lambda i: (0, 0)),
            pl.BlockSpec((1, DV_GLA), lambda i: (0, 0)),
            pl.BlockSpec((D_MODEL, D_MODEL), lambda i: (0, 0)),
        ],
        out_specs=pl.BlockSpec((tm, D_MODEL), lambda i: (i, 0)),
        out_shape=jax.ShapeDtypeStruct((m, D_MODEL), F32),
        compiler_params=_params("parallel"),
        name="merge",
    )(x, o_att, o_gla, p, g_att, g_gla, w_out)


SATT_HEADS_PER_STEP = 4


def _sattn_scores(q, k_ref, s_ref):
    q16 = (q * HD_ATT ** -0.5).astype(BF16)
    for h in range(k_ref.shape[1]):
        s_ref[h:h + 1, :] = _dot(q16, k_ref[0, h].astype(BF16))[h:h + 1, :]


def _sattn_finish(q, kn, vn, v_ref, s_ref):
    scale = HD_ATT ** -0.5
    wb = v_ref.shape[-1]
    t = lax.broadcasted_iota(jnp.int32, (1, wb), 1)
    cnt = jnp.zeros((1, wb), F32)
    for w, d in DIL_PATTERNS:
        cnt = cnt + jnp.where((t >= wb - w) & ((t & (d - 1)) == 0), 1.0, 0.0)
    bias = jnp.where(cnt > 0.0, 0.0, jnp.float32(-jnp.inf))
    n_pat = float(len(DIL_PATTERNS))
    n_heads = v_ref.shape[1]
    s_new = jnp.sum(q * scale * kn, axis=-1, keepdims=True)
    s = s_ref[...] + bias
    m = jnp.maximum(jnp.max(s, axis=-1, keepdims=True), s_new)
    e = cnt * jnp.exp(s - m)
    e_new = n_pat * jnp.exp(s_new - m)
    den = jnp.sum(e, axis=-1, keepdims=True) + e_new
    e16 = e.astype(BF16)
    head = lax.broadcasted_iota(jnp.int32, (n_heads, HD_ATT), 0)
    num = e_new * vn
    for h in range(n_heads):
        num = num + jnp.where(head == h, _dot_nt(e16, v_ref[0, h].astype(BF16)), 0.0)
    return num / den


SGLA_SEQS_PER_STEP = 4
_SPLIT = 3


def _bf16_pieces(x):
    pieces = []
    for _ in range(_SPLIT):
        p = x.astype(BF16)
        pieces.append(p)
        x = x - p.astype(F32)
    return pieces


def _sgla_kernel(q_ref, k_ref, g_ref, v_ref, s_ref, so_ref, o_ref):
    scale = DK_GLA ** -0.5
    n_vec = 3
    sel_rows = 16
    r = lax.broadcasted_iota(jnp.int32, (sel_rows, n_vec * DV_GLA), 0)
    c = lax.broadcasted_iota(jnp.int32, (sel_rows, n_vec * DV_GLA), 1)
    selector = jnp.where((r // _SPLIT == c // DV_GLA) & (r < n_vec * _SPLIT), 1.0, 0.0).astype(BF16)
    pad = jnp.zeros((sel_rows - n_vec * _SPLIT, DK_GLA), BF16)
    for s in range(q_ref.shape[0]):
        for h in range(H_GLA):
            hrow = slice(h, h + 1)
            vecs = (jnp.exp(g_ref[s, hrow, :]), k_ref[s, hrow, :], q_ref[s, hrow, :] * scale)
            lhs = jnp.concatenate([p for x in vecs for p in _bf16_pieces(x)] + [pad], axis=0)
            cols = _dot_tn(lhs, selector)
            decay, kcol, qcol = (cols[:, i * DV_GLA:(i + 1) * DV_GLA] for i in range(n_vec))
            s_new = decay * s_ref[s, h] + kcol * v_ref[s, hrow, :]
            so_ref[s, h] = s_new
            o_ref[s, hrow, :] = jnp.sum(qcol * s_new, axis=0, keepdims=True)


def _gla_sample(q, k, g, v, state):
    nb = state.shape[0]
    bs = SGLA_SEQS_PER_STEP
    krow = pl.BlockSpec((bs, H_GLA, DK_GLA), lambda b: (b, 0, 0))
    vrow = pl.BlockSpec((bs, H_GLA, DV_GLA), lambda b: (b, 0, 0))
    st = pl.BlockSpec((bs, H_GLA, DK_GLA, DV_GLA), lambda b: (b, 0, 0, 0))
    return pl.pallas_call(
        _sgla_kernel,
        grid=(nb // bs,),
        in_specs=[krow, krow, krow, vrow, st],
        out_specs=[st, vrow],
        out_shape=[jax.ShapeDtypeStruct((nb, H_GLA, DK_GLA, DV_GLA), F32),
                   jax.ShapeDtypeStruct((nb, H_GLA, DV_GLA), F32)],
        compiler_params=_params("parallel"),
        name="gla_sample",
    )(q, k, g, v, state)


def kernel(x_prompt, x_sample, cache_att_k, cache_att_v, state_gla, g_ffn1, w_ffn1_in, w_ffn1_out, g_mix, w_in, w_gate2, b_gate2, g_att_out, g_gla_out, w_out, g_ffn2, w_ffn2_in, w_ffn2_out, g_final):
    depth = w_in.shape[0]
    assert depth == 1
    batch, seq, _ = x_prompt.shape
    nb, dec_seq, _ = x_sample.shape
    assert dec_seq == 1
    xp = x_prompt.reshape(batch * seq, D_MODEL)
    xs = x_sample.reshape(nb, D_MODEL)
    row = lambda a: a.reshape(1, -1)
    l = 0

    w_in_t = w_in[l].T
    w_lr = w_in_t[D_PROJ_MAIN:].astype(BF16)
    w_g2 = w_gate2[l].astype(BF16)
    b_g2 = row(b_gate2[l])
    w_o = w_out[l].astype(BF16)
    gf = row(g_final)
    tmat = jnp.asarray(_gla_exponent_matrix(), dtype=BF16)

    xs, w1a, w1b, w1o = _ffn(xs, row(g_ffn1[l]), w_ffn1_in[l], w_ffn1_in[l], w_ffn1_out[l], gf,
                             tm=nb, tf=FFN_TILE, final_norm=False, emit_bf16=True)
    ps, las, w_main = _proj(xs, row(g_mix[l]), w_in_t, w_lr, w_g2, b_g2, tm=nb, tn=512, emit_bf16=True)
    heads = lambda c0: ps[:, c0:c0 + D_ATT].reshape(nb, H_ATT, HD_ATT)
    k_new, v_new = heads(COL_KA), heads(COL_VA)
    xp, o_att_s, w2i, w2o = _ffn(
        xp, row(g_ffn1[l]), w1a, w1b, w1o, gf, tm=FFN_ROWS_SIDE, tf=FFN_TILE_SIDE, final_norm=False,
        side=(heads(COL_QA), k_new, v_new,
              jnp.transpose(cache_att_k[l], (0, 2, 3, 1)), jnp.transpose(cache_att_v[l], (0, 2, 3, 1)),
              w_ffn2_in[l], w_ffn2_out[l]))
    krows = lambda a: a.reshape(nb, H_GLA, DK_GLA)
    s_new, o_gla_s = _gla_sample(
        krows(ps[:, COL_QG:COL_QG + D_GLA_K]), krows(ps[:, COL_KG:COL_KG + D_GLA_K]), krows(las),
        ps[:, COL_VG:COL_VG + D_GLA_V].reshape(nb, H_GLA, DV_GLA), state_gla[l])
    xs = _merge(xs, o_att_s.reshape(nb, D_ATT), o_gla_s.reshape(nb, D_GLA_V), ps,
                row(g_att_out[l]), row(g_gla_out[l]), w_o, tm=nb)
    ys = _ffn(xs, row(g_ffn2[l]), w2i, w2i, w2o, gf, tm=nb, tf=FFN_TILE, final_norm=True)
    nk_s = k_new.reshape(1, nb, 1, H_ATT, HD_ATT)
    nv_s = v_new.reshape(1, nb, 1, H_ATT, HD_ATT)

    pp, lap = _proj(xp, row(g_mix[l]), w_main, w_lr, w_g2, b_g2, tm=1024, tn=1024)
    o_att, kt_p, vt_p = _attn_prompt(pp, batch, seq)
    o_gla, s_fin = _gla_prompt(pp, lap, tmat, batch, seq)
    xp = _merge(xp, o_att, o_gla, pp, row(g_att_out[l]), row(g_gla_out[l]), w_o, tm=512)
    yp = _ffn(xp, row(g_ffn2[l]), w2i, w2i, w2o, gf, tm=FFN_ROWS, tf=FFN_TILE_WIDE_ROWS, final_norm=True)
    nk_p = jnp.transpose(kt_p, (0, 3, 1, 2))[None]
    nv_p = jnp.transpose(vt_p, (0, 3, 1, 2))[None]

    return (yp.reshape(batch, seq, D_MODEL), ys.reshape(nb, 1, D_MODEL), nk_p, nv_p,
            s_fin[None], nk_s, nv_s, s_new[None])
```

```python
import functools

import numpy as np
import jax
import jax.numpy as jnp
from jax import lax
from jax.experimental import pallas as pl
from jax.experimental.pallas import tpu as pltpu

F32 = jnp.float32
BF16 = jnp.bfloat16

D_MODEL = 2048
D_FF = 5632
D_ATT = 1024
HD_ATT = 64
H_ATT = 16
H_GLA = 4
DK_GLA = 128
DV_GLA = 256
D_GLA_K = H_GLA * DK_GLA
D_GLA_V = H_GLA * DV_GLA
GATE_RANK = 16
GATE_TAU = 16.0
NORM_EPS = 1e-6
DIL_PATTERNS = ((128, 1), (512, 4), (2048, 16))
ATT_BLOCK = 128
ATT_GROUP = 16
GLA_CHUNK = 64
GLA_GROUP = 8
D_PROJ_MAIN = 3 * D_ATT + 2 * D_GLA_K + 2 * D_GLA_V

COL_QA, COL_KA, COL_VA = 0, D_ATT, 2 * D_ATT
COL_QG = 3 * D_ATT
COL_KG = COL_QG + D_GLA_K
COL_VG = COL_KG + D_GLA_K
COL_RG = COL_VG + D_GLA_V

VMEM_LIMIT_BYTES = 56 * 1024 * 1024

FFN_ROWS, FFN_TILE_WIDE_ROWS = 1024, 256
FFN_ROWS_SIDE, FFN_TILE_SIDE = FFN_ROWS, FFN_TILE_WIDE_ROWS
FFN_TILE = 512


def _rms(x):
    return x * lax.rsqrt(jnp.mean(x * x, axis=-1, keepdims=True) + NORM_EPS)


def _dot(a, b):
    return jnp.dot(a, b, preferred_element_type=F32)


def _dot_nt(a, b):
    return lax.dot_general(a, b, (((1,), (1,)), ((), ())), preferred_element_type=F32)


def _dot_tn(a, b):
    return lax.dot_general(a, b, (((0,), (0,)), ((), ())), preferred_element_type=F32)


def _params(*sem):
    return pltpu.CompilerParams(dimension_semantics=sem, vmem_limit_bytes=VMEM_LIMIT_BYTES)


def _ffn_step(x_ref, g_ref, wa, wb, wo, gf_ref, o_ref, h_ref, final_norm, side_jobs=(None, None)):
    j = pl.program_id(1)

    @pl.when(j == 0)
    def _():
        h_ref[...] = (_rms(x_ref[...]) * g_ref[...]).astype(BF16)
        o_ref[...] = jnp.zeros_like(o_ref)

    h = h_ref[...]
    a = _dot(h, wa)
    if side_jobs[0] is not None:
        side_jobs[0]()
    b = _dot(h, wb)
    if side_jobs[1] is not None:
        side_jobs[1]()
    act = (a * jax.nn.sigmoid(a) * b).astype(BF16)
    o_ref[...] += _dot(act, wo)

    @pl.when(j == pl.num_programs(1) - 1)
    def _():
        y = x_ref[...] + 0.5 * o_ref[...]
        if final_norm:
            y = _rms(y) * gf_ref[...]
        o_ref[...] = y


def _ffn_kernel(x_ref, g_ref, wa_ref, wb_ref, wo_ref, gf_ref, o_ref, *rest, final_norm, emit_bf16):
    wa, wb, wo = wa_ref[...], wb_ref[...], wo_ref[...]
    if emit_bf16:
        wa, wb, wo = wa.astype(BF16), wb.astype(BF16), wo.astype(BF16)
        for dst_ref, w in zip(rest[:3], (wa, wb, wo)):
            dst_ref[...] = w
    _ffn_step(x_ref, g_ref, wa, wb, wo, gf_ref, o_ref, rest[-1], final_norm)


def _ffn_side_kernel(x_ref, g_ref, wa_ref, wb_ref, wo_ref, gf_ref,
                     q_ref, kn_ref, vn_ref, k_ref, v_ref, cwi_ref, cwo_ref,
                     o_ref, so_ref, cwi16_ref, cwo16_ref, h_ref, s_ref, *, final_norm, n_side_blocks):
    hs = k_ref.shape[1]
    groups = H_ATT // hs
    step = pl.program_id(0) * pl.num_programs(1) + pl.program_id(1)
    blk = jnp.minimum(step, n_side_blocks - 1)
    heads = pl.ds(pl.multiple_of((blk % groups) * hs, hs), hs)

    def scores():
        _sattn_scores(q_ref[0, heads, :], k_ref, s_ref)
        cwi16_ref[...] = cwi_ref[...].astype(BF16)
        cwo16_ref[...] = cwo_ref[...].astype(BF16)

    def finish():
        so_ref[0, heads, :] = _sattn_finish(q_ref[0, heads, :], kn_ref[0, heads, :], vn_ref[0, heads, :],
                                            v_ref, s_ref)

    _ffn_step(x_ref, g_ref, wa_ref[...], wb_ref[...], wo_ref[...], gf_ref, o_ref, h_ref, final_norm,
              (scores, finish))


def _ffn(x, g, wa, wb, wo, g_final, *, tm, tf, final_norm, emit_bf16=False, side=None):
    m = x.shape[0]
    nf = D_FF // tf
    ni = m // tm
    b_off = nf if wb.shape[1] == 2 * D_FF else 0
    in_specs = [
        pl.BlockSpec((tm, D_MODEL), lambda i, j: (i, 0)),
        pl.BlockSpec((1, D_MODEL), lambda i, j: (0, 0)),
        pl.BlockSpec((D_MODEL, tf), lambda i, j: (0, j)),
        pl.BlockSpec((D_MODEL, tf), lambda i, j: (0, j + b_off)),
        pl.BlockSpec((tf, D_MODEL), lambda i, j: (j, 0)),
        pl.BlockSpec((1, D_MODEL), lambda i, j: (0, 0)),
    ]
    out_specs = [pl.BlockSpec((tm, D_MODEL), lambda i, j: (i, 0))]
    out_shape = [jax.ShapeDtypeStruct((m, D_MODEL), F32)]
    scratch = [pltpu.VMEM((tm, D_MODEL), BF16)]
    operands = (x, g, wa, wb, wo, g_final)
    if side is None:
        body = functools.partial(_ffn_kernel, final_norm=final_norm, emit_bf16=emit_bf16)
        sem = ("parallel", "arbitrary")
    if emit_bf16:
        assert m == tm and side is None
        out_specs += [pl.BlockSpec((D_MODEL, tf), lambda i, j: (0, j)),
                      pl.BlockSpec((D_MODEL, tf), lambda i, j: (0, j)),
                      pl.BlockSpec((tf, D_MODEL), lambda i, j: (j, 0))]
        out_shape += [jax.ShapeDtypeStruct((D_MODEL, D_FF), BF16),
                      jax.ShapeDtypeStruct((D_MODEL, D_FF), BF16),
                      jax.ShapeDtypeStruct((D_FF, D_MODEL), BF16)]
    if side is not None:
        q, kn, vn, cache_kt, cache_vt, cw_in, cw_out = side
        nb, _, _, wb_len = cache_kt.shape
        assert all(wb_len % d == 0 and w <= wb_len for w, d in DIL_PATTERNS)
        hs = SATT_HEADS_PER_STEP
        groups = H_ATT // hs
        n_blocks = nb * groups
        assert ni * nf >= n_blocks and D_MODEL % ni == 0 and (2 * D_FF) % nf == 0
        sblk = lambda i, j: jnp.minimum(i * nf + j, n_blocks - 1)
        seq_row = pl.BlockSpec((1, H_ATT, HD_ATT), lambda i, j: (sblk(i, j) // groups, 0, 0))
        cache = pl.BlockSpec((1, hs, HD_ATT, wb_len),
                             lambda i, j: (sblk(i, j) // groups, sblk(i, j) % groups, 0, 0))
        cwi = pl.BlockSpec((D_MODEL // ni, 2 * D_FF // nf), lambda i, j: (i, j))
        cwo = pl.BlockSpec((D_FF // nf, D_MODEL // ni), lambda i, j: (j, i))
        in_specs += [seq_row, seq_row, seq_row, cache, cache, cwi, cwo]
        out_specs += [seq_row, cwi, cwo]
        out_shape += [jax.ShapeDtypeStruct((nb, H_ATT, HD_ATT), F32),
                      jax.ShapeDtypeStruct(cw_in.shape, BF16),
                      jax.ShapeDtypeStruct(cw_out.shape, BF16)]
        scratch.append(pltpu.VMEM((hs, wb_len), F32))
        operands += (q, kn, vn, cache_kt, cache_vt, cw_in, cw_out)
        body = functools.partial(_ffn_side_kernel, final_norm=final_norm, n_side_blocks=n_blocks)
        sem = ("arbitrary", "arbitrary")
    outs = pl.pallas_call(
        body,
        grid=(ni, nf),
        in_specs=in_specs,
        out_specs=out_specs,
        out_shape=out_shape,
        scratch_shapes=scratch,
        compiler_params=_params(*sem),
        name="ffn",
    )(*operands)
    return outs if (emit_bf16 or side is not None) else outs[0]


def _proj_kernel(x_ref, g_ref, w_ref, wlr_ref, wg2_ref, bg2_ref, p_ref, la_ref, *rest, emit_bf16):
    h_ref = rest[-1]
    j = pl.program_id(1)

    @pl.when(j == 0)
    def _():
        h = (_rms(x_ref[...]) * g_ref[...]).astype(BF16)
        h_ref[...] = h
        lr = _dot_nt(h, wlr_ref[...])
        z = _dot(lr.astype(BF16), wg2_ref[...]) + bg2_ref[...]
        log_sig = jnp.minimum(z, 0.0) - jnp.log1p(jnp.exp(-jnp.abs(z)))
        la_ref[...] = log_sig * (1.0 / GATE_TAU)

    if emit_bf16:
        w = w_ref[...]
        rest[0][...] = w.T.astype(BF16)
        p_ref[...] = _dot_nt(h_ref[...], w.astype(BF16))
    else:
        p_ref[...] = _dot(h_ref[...], w_ref[...])


def _proj(x, g, w, w_lr_t, w_g2, b_g2, *, tm, tn, emit_bf16=False):
    m = x.shape[0]
    if emit_bf16:
        w_spec = pl.BlockSpec((tn, D_MODEL), lambda i, j: (j, 0))
    else:
        w_spec = pl.BlockSpec((D_MODEL, tn), lambda i, j: (0, j))
    out_specs = [
        pl.BlockSpec((tm, tn), lambda i, j: (i, j)),
        pl.BlockSpec((tm, D_GLA_K), lambda i, j: (i, 0)),
    ]
    out_shape = [
        jax.ShapeDtypeStruct((m, D_PROJ_MAIN), F32),
        jax.ShapeDtypeStruct((m, D_GLA_K), F32),
    ]
    if emit_bf16:
        assert m == tm
        out_specs.append(pl.BlockSpec((D_MODEL, tn), lambda i, j: (0, j)))
        out_shape.append(jax.ShapeDtypeStruct((D_MODEL, D_PROJ_MAIN), BF16))
    return pl.pallas_call(
        functools.partial(_proj_kernel, emit_bf16=emit_bf16),
        grid=(m // tm, D_PROJ_MAIN // tn),
        in_specs=[
            pl.BlockSpec((tm, D_MODEL), lambda i, j: (i, 0)),
            pl.BlockSpec((1, D_MODEL), lambda i, j: (0, 0)),
            w_spec,
            pl.BlockSpec((GATE_RANK, D_MODEL), lambda i, j: (0, 0)),
            pl.BlockSpec((GATE_RANK, D_GLA_K), lambda i, j: (0, 0)),
            pl.BlockSpec((1, D_GLA_K), lambda i, j: (0, 0)),
        ],
        out_specs=out_specs,
        out_shape=out_shape,
        scratch_shapes=[pltpu.VMEM((tm, D_MODEL), BF16)],
        compiler_params=_params("parallel", "arbitrary"),
        name="proj",
    )(x, g, w, w_lr_t, w_g2, b_g2)


def _attn_kernel(q_ref, k_ref, v_ref, o_ref, kt_ref, vt_ref,
                 qp_ref, kp_ref, vp_ref, m_ref, l_ref, acc_ref, s_ref, ms_ref):
    blk = ATT_BLOCK
    for src_ref, dst_ref in ((k_ref, kt_ref), (v_ref, vt_ref)):
        t = src_ref[...].T
        dst_ref[0, 0] = t[:HD_ATT]
        dst_ref[0, 1] = t[HD_ATT:]

    scale = HD_ATT ** -0.5
    seq = q_ref.shape[0]
    ns = DIL_PATTERNS[-1][1]
    lane_lo = lax.broadcasted_iota(jnp.int32, (1, 2 * HD_ATT), 1) < HD_ATT
    neg = jnp.float32(-jnp.inf)

    assert ns == 16
    for src_ref, tmp_ref, dst_ref in ((q_ref, m_ref, qp_ref), (k_ref, l_ref, kp_ref), (v_ref, acc_ref, vp_ref)):
        for r4 in range(4):
            x = src_ref[pl.ds(r4, seq // 4, stride=4), :]
            tmp_ref[pl.ds(r4 * (seq // 4), seq // 4), :] = x * scale if src_ref is q_ref else x
        for r4 in range(4):
            for a in range(4):
                dst_ref[pl.ds((4 * a + r4) * blk, blk), :] = tmp_ref[pl.ds(r4 * (seq // 4) + a, blk, stride=4), :]

    def run_pattern(d, first_pattern):
        na = ns // d
        plen = blk // na
        per_stream = seq // (d * blk)

        def offset(idx):
            return na * (idx & (plen - 1)) + idx // plen

        qpos = offset(lax.broadcasted_iota(jnp.int32, (blk, blk), 0))
        kpos = offset(lax.broadcasted_iota(jnp.int32, (blk, blk), 1))
        bias_cur = jnp.where(kpos <= qpos, 0.0, neg)
        if per_stream > 1:
            bias_prev = jnp.where(kpos >= qpos, 0.0, neg)
            bias_band = jnp.concatenate([bias_prev, bias_cur], axis=1)
            bias_first = jnp.concatenate([jnp.full((blk, blk), neg, F32), bias_cur], axis=1)

        def pieces(rd, n):
            return [pl.ds((a * d + rd) * blk + plen * n, plen) for a in range(na)]

        def gather(ref, ps):
            return jnp.concatenate([ref[p, :] for p in ps], axis=0)

        nk = 2 * blk if per_stream > 1 else blk
        heads = (lane_lo, jnp.logical_not(lane_lo))

        def keys(ref, rd, n):
            cur = gather(ref, pieces(rd, n))
            if per_stream == 1:
                return cur.astype(BF16)
            prev = gather(ref, pieces(rd, max(n - 1, 0)))
            return jnp.concatenate([prev, cur], axis=0).astype(BF16)

        def group(g):
            blocks = [((g * ATT_GROUP + b) % d, (g * ATT_GROUP + b) // d) for b in range(ATT_GROUP)]
            for b, (rd, n) in enumerate(blocks):
                q = gather(qp_ref, pieces(rd, n))
                kk = keys(kp_ref, rd, n)
                bias = bias_cur if per_stream == 1 else (bias_band if n > 0 else bias_first)
                for h, sel in enumerate(heads):
                    qh = jnp.where(sel, q, 0.0).astype(BF16)
                    s_ref[b, h, :, :nk] = _dot_nt(qh, kk) + bias
            for b in range(ATT_GROUP):
                for h in range(2):
                    m = jnp.max(s_ref[b, h, :, :nk], axis=-1, keepdims=True)
                    ms_ref[b, h] = jnp.broadcast_to(m, (blk, blk))
            for b, (rd, n) in enumerate(blocks):
                vv = jnp.concatenate([keys(vp_ref, rd, n), jnp.ones((nk, blk), BF16)], axis=1)
                res = []
                for h in range(2):
                    mh = ms_ref[b, h]
                    mh = jnp.concatenate([mh, mh], axis=1) if nk == 2 * blk else mh
                    res.append(_dot(jnp.exp(s_ref[b, h, :, :nk] - mh).astype(BF16), vv))
                u = jnp.where(lane_lo, res[0][:, :blk], res[1][:, :blk])
                l = jnp.where(lane_lo, res[0][:, blk:], res[1][:, blk:])
                m = jnp.where(lane_lo, ms_ref[b, 0], ms_ref[b, 1])
                for a, p in enumerate(pieces(rd, n)):
                    sl = slice(a * plen, (a + 1) * plen)
                    if first_pattern:
                        m_ref[p, :] = m[sl]
                        l_ref[p, :] = l[sl]
                        acc_ref[p, :] = u[sl]
                    else:
                        m_old = m_ref[p, :]
                        m_new = jnp.maximum(m_old, m[sl])
                        a_old = jnp.exp(m_old - m_new)
                        a_blk = jnp.exp(m[sl] - m_new)
                        m_ref[p, :] = m_new
                        l_ref[p, :] = a_old * l_ref[p, :] + a_blk * l[sl]
                        acc_ref[p, :] = a_old * acc_ref[p, :] + a_blk * u[sl]

        for g in range(d * per_stream // ATT_GROUP):
            group(g)

    for idx, (_, d) in enumerate(DIL_PATTERNS):
        run_pattern(d, idx == 0)

    for r4 in range(4):
        for a in range(4):
            rows = pl.ds((4 * a + r4) * blk, blk)
            qp_ref[pl.ds(r4 * (seq // 4) + a, blk, stride=4), :] = acc_ref[rows, :] / l_ref[rows, :]
    for r4 in range(4):
        o_ref[pl.ds(r4, seq // 4, stride=4), :] = qp_ref[pl.ds(r4 * (seq // 4), seq // 4), :]


def _attn_prompt(p, batch, seq):
    ns = DIL_PATTERNS[-1][1]
    assert seq == ns * ATT_BLOCK
    assert all(w // d == ATT_BLOCK and ns % d == 0 and ATT_BLOCK * d // ns >= 8 for w, d in DIL_PATTERNS)
    lanes = 2 * HD_ATT
    n_pairs = H_ATT // 2
    return pl.pallas_call(
        _attn_kernel,
        grid=(batch, n_pairs),
        in_specs=[
            pl.BlockSpec((seq, lanes), lambda b, h: (b, COL_QA // lanes + h)),
            pl.BlockSpec((seq, lanes), lambda b, h: (b, COL_KA // lanes + h)),
            pl.BlockSpec((seq, lanes), lambda b, h: (b, COL_VA // lanes + h)),
        ],
        out_specs=[
            pl.BlockSpec((seq, lanes), lambda b, h: (b, h)),
            pl.BlockSpec((1, 2, HD_ATT, seq), lambda b, h: (b, h, 0, 0)),
            pl.BlockSpec((1, 2, HD_ATT, seq), lambda b, h: (b, h, 0, 0)),
        ],
        out_shape=[
            jax.ShapeDtypeStruct((batch * seq, D_ATT), F32),
            jax.ShapeDtypeStruct((batch, H_ATT, HD_ATT, seq), F32),
            jax.ShapeDtypeStruct((batch, H_ATT, HD_ATT, seq), F32),
        ],
        scratch_shapes=[pltpu.VMEM((seq, lanes), F32)] * 6 + [
            pltpu.VMEM((ATT_GROUP, 2, ATT_BLOCK, 2 * ATT_BLOCK), F32),
            pltpu.VMEM((ATT_GROUP, 2, ATT_BLOCK, lanes), F32),
        ],
        compiler_params=_params("parallel", "parallel"),
        name="attn_prompt",
    )(p, p, p)


_GLA_LEVELS = (32, 16, 8, 4, 2, 1)


def _gla_exponent_matrix():
    c = GLA_CHUNK
    t = np.arange(c)[:, None]
    u = np.arange(c)[None, :]
    mats = [(u <= t), (u > t)]
    for h in _GLA_LEVELS:
        mid = (t // (2 * h)) * (2 * h) + h - 1
        upper = (t % (2 * h)) >= h
        mats.append(np.where(upper, (u > mid) & (u <= t), (u > t) & (u <= mid)))
    tmat = np.concatenate(mats, axis=0).astype(np.float32)
    return np.concatenate([tmat, tmat], axis=1)


def _gla_kernel(q_ref, k_ref, v_ref, g_ref, t_ref, o_ref, s_ref, st_ref, f_ref, a_ref, kv_ref):
    c = GLA_CHUNK
    scale = DK_GLA ** -0.5
    ti = lax.broadcasted_iota(jnp.int32, (c, c), 0)
    si = lax.broadcasted_iota(jnp.int32, (c, c), 1)
    txs = ti ^ si
    below = ti > si
    level_masks = [below & (txs >= h) & (txs < 2 * h) for h in _GLA_LEVELS]
    diag = ti == si

    st_ref[...] = jnp.zeros_like(st_ref)

    def group(gi, carry):
        rows = [pl.ds(pl.multiple_of((gi * GLA_GROUP + i) * c, c), c) for i in range(GLA_GROUP)]
        for i, r in enumerate(rows):
            g = g_ref[r, :]
            g_hi = g.astype(BF16)
            g_lo = (g - g_hi.astype(F32)).astype(BF16)
            f_ref[i] = jnp.exp(_dot(t_ref[...], jnp.concatenate([g_hi, g_lo], axis=0)))
        for i, r in enumerate(rows):
            q = q_ref[r, :] * scale
            k = k_ref[r, :]
            a = jnp.where(diag, _dot_nt(q.astype(BF16), k.astype(BF16)), 0.0)
            for lvl in range(len(_GLA_LEVELS)):
                fl = f_ref[i, 2 * c + lvl * c: 3 * c + lvl * c, :]
                a = a + jnp.where(level_masks[lvl],
                                  _dot_nt((q * fl).astype(BF16), (k * fl).astype(BF16)), 0.0)
            a_ref[i] = a.astype(BF16)
        for i, r in enumerate(rows):
            v = v_ref[r, :].astype(BF16)
            kd = (k_ref[r, :] * f_ref[i, c:2 * c, :]).astype(BF16)
            kv_ref[i] = _dot_tn(v, kd)
            o_ref[r, :] = _dot(a_ref[i], v)
        for i, r in enumerate(rows):
            st = st_ref[...]
            qe = (q_ref[r, :] * scale * f_ref[i, 0:c, :]).astype(BF16)
            o_ref[r, :] += _dot_nt(qe, st.astype(BF16))
            st_ref[...] = st * f_ref[i, c - 1:c, :] + kv_ref[i]
        return carry

    lax.fori_loop(0, q_ref.shape[0] // (c * GLA_GROUP), group, 0)
    s_ref[0, 0] = st_ref[...].T


def _gla_prompt(p, la, tmat, batch, seq):
    return pl.pallas_call(
        _gla_kernel,
        grid=(batch, H_GLA),
        in_specs=[
            pl.BlockSpec((seq, DK_GLA), lambda b, h: (b, COL_QG // DK_GLA + h)),
            pl.BlockSpec((seq, DK_GLA), lambda b, h: (b, COL_KG // DK_GLA + h)),
            pl.BlockSpec((seq, DV_GLA), lambda b, h: (b, COL_VG // DV_GLA + h)),
            pl.BlockSpec((seq, DK_GLA), lambda b, h: (b, h)),
            pl.BlockSpec(tmat.shape, lambda b, h: (0, 0)),
        ],
        out_specs=[
            pl.BlockSpec((seq, DV_GLA), lambda b, h: (b, h)),
            pl.BlockSpec((1, 1, DK_GLA, DV_GLA), lambda b, h: (b, h, 0, 0)),
        ],
        out_shape=[
            jax.ShapeDtypeStruct((batch * seq, D_GLA_V), F32),
            jax.ShapeDtypeStruct((batch, H_GLA, DK_GLA, DV_GLA), F32),
        ],
        scratch_shapes=[pltpu.VMEM((DV_GLA, DK_GLA), F32),
                        pltpu.VMEM((GLA_GROUP,) + (tmat.shape[0], DK_GLA), F32),
                        pltpu.VMEM((GLA_GROUP, GLA_CHUNK, GLA_CHUNK), BF16),
                        pltpu.VMEM((GLA_GROUP, DV_GLA, DK_GLA), F32)],
        compiler_params=_params("parallel", "parallel"),
        name="gla_prompt",
    )(p, p, p, la, tmat)


def _merge_kernel(x_ref, oa_ref, og_ref, rg_ref, ga_ref, gg_ref, w_ref, o_ref):
    a = (_rms(oa_ref[...]) * ga_ref[...]).astype(BF16)
    parts = []
    for h in range(H_GLA):
        cols = slice(h * DV_GLA, (h + 1) * DV_GLA)
        r = rg_ref[:, cols]
        parts.append((_rms(og_ref[:, cols]) * gg_ref[...] * (r * jax.nn.sigmoid(r))).astype(BF16))
    gg = jnp.concatenate(parts, axis=-1)
    o_ref[...] = x_ref[...] + _dot(a, w_ref[0:D_ATT, :]) + _dot(gg, w_ref[D_ATT:, :])


def _merge(x, o_att, o_gla, p, g_att, g_gla, w_out, *, tm):
    m = x.shape[0]
    return pl.pallas_call(
        _merge_kernel,
        grid=(m // tm,),
        in_specs=[
            pl.BlockSpec((tm, D_MODEL), lambda i: (i, 0)),
            pl.BlockSpec((tm, D_ATT), lambda i: (i, 0)),
            pl.BlockSpec((tm, D_GLA_V), lambda i: (i, 0)),
            pl.BlockSpec((tm, D_GLA_V), lambda i: (i, COL_RG // D_GLA_V)),
            pl.BlockSpec((1, D_ATT), lambda i: (0, 0)),
            pl.BlockSpec((1, DV_GLA), lambda i: (0, 0)),
            pl.BlockSpec((D_MODEL, D_MODEL), lambda i: (0, 0)),
        ],
        out_specs=pl.BlockSpec((tm, D_MODEL), lambda i: (i, 0)),
        out_shape=jax.ShapeDtypeStruct((m, D_MODEL), F32),
        compiler_params=_params("parallel"),
        name="merge",
    )(x, o_att, o_gla, p, g_att, g_gla, w_out)


SATT_HEADS_PER_STEP = 4


def _sattn_scores(q, k_ref, s_ref):
    q16 = (q * HD_ATT ** -0.5).astype(BF16)
    for h in range(k_ref.shape[1]):
        s_ref[h:h + 1, :] = _dot(q16, k_ref[0, h].astype(BF16))[h:h + 1, :]


def _sattn_finish(q, kn, vn, v_ref, s_ref):
    scale = HD_ATT ** -0.5
    wb = v_ref.shape[-1]
    t = lax.broadcasted_iota(jnp.int32, (1, wb), 1)
    cnt = jnp.zeros((1, wb), F32)
    for w, d in DIL_PATTERNS:
        cnt = cnt + jnp.where((t >= wb - w) & ((t & (d - 1)) == 0), 1.0, 0.0)
    bias = jnp.where(cnt > 0.0, 0.0, jnp.float32(-jnp.inf))
    n_pat = float(len(DIL_PATTERNS))
    n_heads = v_ref.shape[1]
    s_new = jnp.sum(q * scale * kn, axis=-1, keepdims=True)
    s = s_ref[...] + bias
    m = jnp.maximum(jnp.max(s, axis=-1, keepdims=True), s_new)
    e = cnt * jnp.exp(s - m)
    e_new = n_pat * jnp.exp(s_new - m)
    den = jnp.sum(e, axis=-1, keepdims=True) + e_new
    e16 = e.astype(BF16)
    head = lax.broadcasted_iota(jnp.int32, (n_heads, HD_ATT), 0)
    num = e_new * vn
    for h in range(n_heads):
        num = num + jnp.where(head == h, _dot_nt(e16, v_ref[0, h].astype(BF16)), 0.0)
    return num / den


SGLA_SEQS_PER_STEP = 4
_SPLIT = 3


def _bf16_pieces(x):
    pieces = []
    for _ in range(_SPLIT):
        p = x.astype(BF16)
        pieces.append(p)
        x = x - p.astype(F32)
    return pieces


def _sgla_kernel(q_ref, k_ref, g_ref, v_ref, s_ref, so_ref, o_ref):
    scale = DK_GLA ** -0.5
    n_vec = 3
    sel_rows = 16
    r = lax.broadcasted_iota(jnp.int32, (sel_rows, n_vec * DV_GLA), 0)
    c = lax.broadcasted_iota(jnp.int32, (sel_rows, n_vec * DV_GLA), 1)
    selector = jnp.where((r // _SPLIT == c // DV_GLA) & (r < n_vec * _SPLIT), 1.0, 0.0).astype(BF16)
    pad = jnp.zeros((sel_rows - n_vec * _SPLIT, DK_GLA), BF16)
    for s in range(q_ref.shape[0]):
        for h in range(H_GLA):
            hrow = slice(h, h + 1)
            vecs = (jnp.exp(g_ref[s, hrow, :]), k_ref[s, hrow, :], q_ref[s, hrow, :] * scale)
            lhs = jnp.concatenate([p for x in vecs for p in _bf16_pieces(x)] + [pad], axis=0)
            cols = _dot_tn(lhs, selector)
            decay, kcol, qcol = (cols[:, i * DV_GLA:(i + 1) * DV_GLA] for i in range(n_vec))
            s_new = decay * s_ref[s, h] + kcol * v_ref[s, hrow, :]
            so_ref[s, h] = s_new
            o_ref[s, hrow, :] = jnp.sum(qcol * s_new, axis=0, keepdims=True)


def _gla_sample(q, k, g, v, state):
    nb = state.shape[0]
    bs = SGLA_SEQS_PER_STEP
    krow = pl.BlockSpec((bs, H_GLA, DK_GLA), lambda b: (b, 0, 0))
    vrow = pl.BlockSpec((bs, H_GLA, DV_GLA), lambda b: (b, 0, 0))
    st = pl.BlockSpec((bs, H_GLA, DK_GLA, DV_GLA), lambda b: (b, 0, 0, 0))
    return pl.pallas_call(
        _sgla_kernel,
        grid=(nb // bs,),
        in_specs=[krow, krow, krow, vrow, st],
        out_specs=[st, vrow],
        out_shape=[jax.ShapeDtypeStruct((nb, H_GLA, DK_GLA, DV_GLA), F32),
                   jax.ShapeDtypeStruct((nb, H_GLA, DV_GLA), F32)],
        compiler_params=_params("parallel"),
        name="gla_sample",
    )(q, k, g, v, state)


def kernel(x_prompt, x_sample, cache_att_k, cache_att_v, state_gla, g_ffn1, w_ffn1_in, w_ffn1_out, g_mix, w_in, w_gate2, b_gate2, g_att_out, g_gla_out, w_out, g_ffn2, w_ffn2_in, w_ffn2_out, g_final):
    depth = w_in.shape[0]
    assert depth == 1
    batch, seq, _ = x_prompt.shape
    nb, dec_seq, _ = x_sample.shape
    assert dec_seq == 1
    xp = x_prompt.reshape(batch * seq, D_MODEL)
    xs = x_sample.reshape(nb, D_MODEL)
    row = lambda a: a.reshape(1, -1)
    l = 0

    w_in_t = w_in[l].T
    w_lr = w_in_t[D_PROJ_MAIN:].astype(BF16)
    w_g2 = w_gate2[l].astype(BF16)
    b_g2 = row(b_gate2[l])
    w_o = w_out[l].astype(BF16)
    gf = row(g_final)
    tmat = jnp.asarray(_gla_exponent_matrix(), dtype=BF16)

    xs, w1a, w1b, w1o = _ffn(xs, row(g_ffn1[l]), w_ffn1_in[l], w_ffn1_in[l], w_ffn1_out[l], gf,
                             tm=nb, tf=FFN_TILE, final_norm=False, emit_bf16=True)
    ps, las, w_main = _proj(xs, row(g_mix[l]), w_in_t, w_lr, w_g2, b_g2, tm=nb, tn=512, emit_bf16=True)
    heads = lambda c0: ps[:, c0:c0 + D_ATT].reshape(nb, H_ATT, HD_ATT)
    k_new, v_new = heads(COL_KA), heads(COL_VA)
    xp, o_att_s, w2i, w2o = _ffn(
        xp, row(g_ffn1[l]), w1a, w1b, w1o, gf, tm=FFN_ROWS_SIDE, tf=FFN_TILE_SIDE, final_norm=False,
        side=(heads(COL_QA), k_new, v_new,
              jnp.transpose(cache_att_k[l], (0, 2, 3, 1)), jnp.transpose(cache_att_v[l], (0, 2, 3, 1)),
              w_ffn2_in[l], w_ffn2_out[l]))
    krows = lambda a: a.reshape(nb, H_GLA, DK_GLA)
    s_new, o_gla_s = _gla_sample(
        krows(ps[:, COL_QG:COL_QG + D_GLA_K]), krows(ps[:, COL_KG:COL_KG + D_GLA_K]), krows(las),
        ps[:, COL_VG:COL_VG + D_GLA_V].reshape(nb, H_GLA, DV_GLA), state_gla[l])
    xs = _merge(xs, o_att_s.reshape(nb, D_ATT), o_gla_s.reshape(nb, D_GLA_V), ps,
                row(g_att_out[l]), row(g_gla_out[l]), w_o, tm=nb)
    ys = _ffn(xs, row(g_ffn2[l]), w2i, w2i, w2o, gf, tm=nb, tf=FFN_TILE, final_norm=True)
    nk_s = k_new.reshape(1, nb, 1, H_ATT, HD_ATT)
    nv_s = v_new.reshape(1, nb, 1, H_ATT, HD_ATT)

    pp, lap = _proj(xp, row(g_mix[l]), w_main, w_lr, w_g2, b_g2, tm=1024, tn=1024)
    o_att, kt_p, vt_p = _attn_prompt(pp, batch, seq)
    o_gla, s_fin = _gla_prompt(pp, lap, tmat, batch, seq)
    xp = _merge(xp, o_att, o_gla, pp, row(g_att_out[l]), row(g_gla_out[l]), w_o, tm=512)
    yp = _ffn(xp, row(g_ffn2[l]), w2i, w2i, w2o, gf, tm=FFN_ROWS, tf=FFN_TILE_WIDE_ROWS, final_norm=True)
    nk_p = jnp.transpose(kt_p, (0, 3, 1, 2))[None]
    nv_p = jnp.transpose(vt_p, (0, 3, 1, 2))[None]

    return (yp.reshape(batch, seq, D_MODEL), ys.reshape(nb, 1, D_MODEL), nk_p, nv_p,
            s_fin[None], nk_s, nv_s, s_new[None])
```

```python
import functools

import numpy as np
import jax
import jax.numpy as jnp
from jax import lax
from jax.experimental import pallas as pl
from jax.experimental.pallas import tpu as pltpu

F32 = jnp.float32
BF16 = jnp.bfloat16

D_MODEL = 2048
D_FF = 5632
D_ATT = 1024
HD_ATT = 64
H_ATT = 16
H_GLA = 4
DK_GLA = 128
DV_GLA = 256
D_GLA_K = H_GLA * DK_GLA
D_GLA_V = H_GLA * DV_GLA
GATE_RANK = 16
GATE_TAU = 16.0
NORM_EPS = 1e-6
DIL_PATTERNS = ((128, 1), (512, 4), (2048, 16))
ATT_BLOCK = 128
ATT_GROUP = 16
GLA_CHUNK = 64
GLA_GROUP = 8
D_PROJ_MAIN = 3 * D_ATT + 2 * D_GLA_K + 2 * D_GLA_V

COL_QA, COL_KA, COL_VA = 0, D_ATT, 2 * D_ATT
COL_QG = 3 * D_ATT
COL_KG = COL_QG + D_GLA_K
COL_VG = COL_KG + D_GLA_K
COL_RG = COL_VG + D_GLA_V

VMEM_LIMIT_BYTES = 56 * 1024 * 1024

FFN_ROWS, FFN_TILE_WIDE_ROWS = 1024, 256
FFN_ROWS_SIDE, FFN_TILE_SIDE = FFN_ROWS, FFN_TILE_WIDE_ROWS
FFN_TILE = 512


def _rms(x):
    return x * lax.rsqrt(jnp.mean(x * x, axis=-1, keepdims=True) + NORM_EPS)


def _dot(a, b):
    return jnp.dot(a, b, preferred_element_type=F32)


def _dot_nt(a, b):
    return lax.dot_general(a, b, (((1,), (1,)), ((), ())), preferred_element_type=F32)


def _dot_tn(a, b):
    return lax.dot_general(a, b, (((0,), (0,)), ((), ())), preferred_element_type=F32)


def _params(*sem):
    return pltpu.CompilerParams(dimension_semantics=sem, vmem_limit_bytes=VMEM_LIMIT_BYTES)


def _ffn_step(x_ref, g_ref, wa, wb, wo, gf_ref, o_ref, h_ref, final_norm, side_jobs=(None, None)):
    j = pl.program_id(1)

    @pl.when(j == 0)
    def _():
        h_ref[...] = (_rms(x_ref[...]) * g_ref[...]).astype(BF16)
        o_ref[...] = jnp.zeros_like(o_ref)

    h = h_ref[...]
    a = _dot(h, wa)
    if side_jobs[0] is not None:
        side_jobs[0]()
    b = _dot(h, wb)
    if side_jobs[1] is not None:
        side_jobs[1]()
    act = (a * jax.nn.sigmoid(a) * b).astype(BF16)
    o_ref[...] += _dot(act, wo)

    @pl.when(j == pl.num_programs(1) - 1)
    def _():
        y = x_ref[...] + 0.5 * o_ref[...]
        if final_norm:
            y = _rms(y) * gf_ref[...]
        o_ref[...] = y


def _ffn_kernel(x_ref, g_ref, wa_ref, wb_ref, wo_ref, gf_ref, o_ref, *rest, final_norm, emit_bf16):
    wa, wb, wo = wa_ref[...], wb_ref[...], wo_ref[...]
    if emit_bf16:
        wa, wb, wo = wa.astype(BF16), wb.astype(BF16), wo.astype(BF16)
        for dst_ref, w in zip(rest[:3], (wa, wb, wo)):
            dst_ref[...] = w
    _ffn_step(x_ref, g_ref, wa, wb, wo, gf_ref, o_ref, rest[-1], final_norm)


def _ffn_side_kernel(x_ref, g_ref, wa_ref, wb_ref, wo_ref, gf_ref,
                     q_ref, kn_ref, vn_ref, k_ref, v_ref, cwi_ref, cwo_ref,
                     o_ref, so_ref, cwi16_ref, cwo16_ref, h_ref, s_ref, *, final_norm, n_side_blocks):
    hs = k_ref.shape[1]
    groups = H_ATT // hs
    step = pl.program_id(0) * pl.num_programs(1) + pl.program_id(1)
    blk = jnp.minimum(step, n_side_blocks - 1)
    heads = pl.ds(pl.multiple_of((blk % groups) * hs, hs), hs)

    def scores():
        _sattn_scores(q_ref[0, heads, :], k_ref, s_ref)
        cwi16_ref[...] = cwi_ref[...].astype(BF16)
        cwo16_ref[...] = cwo_ref[...].astype(BF16)

    def finish():
        so_ref[0, heads, :] = _sattn_finish(q_ref[0, heads, :], kn_ref[0, heads, :], vn_ref[0, heads, :],
                                            v_ref, s_ref)

    _ffn_step(x_ref, g_ref, wa_ref[...], wb_ref[...], wo_ref[...], gf_ref, o_ref, h_ref, final_norm,
              (scores, finish))


def _ffn(x, g, wa, wb, wo, g_final, *, tm, tf, final_norm, emit_bf16=False, side=None):
    m = x.shape[0]
    nf = D_FF // tf
    ni = m // tm
    b_off = nf if wb.shape[1] == 2 * D_FF else 0
    in_specs = [
        pl.BlockSpec((tm, D_MODEL), lambda i, j: (i, 0)),
        pl.BlockSpec((1, D_MODEL), lambda i, j: (0, 0)),
        pl.BlockSpec((D_MODEL, tf), lambda i, j: (0, j)),
        pl.BlockSpec((D_MODEL, tf), lambda i, j: (0, j + b_off)),
        pl.BlockSpec((tf, D_MODEL), lambda i, j: (j, 0)),
        pl.BlockSpec((1, D_MODEL), lambda i, j: (0, 0)),
    ]
    out_specs = [pl.BlockSpec((tm, D_MODEL), lambda i, j: (i, 0))]
    out_shape = [jax.ShapeDtypeStruct((m, D_MODEL), F32)]
    scratch = [pltpu.VMEM((tm, D_MODEL), BF16)]
    operands = (x, g, wa, wb, wo, g_final)
    if side is None:
        body = functools.partial(_ffn_kernel, final_norm=final_norm, emit_bf16=emit_bf16)
        sem = ("parallel", "arbitrary")
    if emit_bf16:
        assert m == tm and side is None
        out_specs += [pl.BlockSpec((D_MODEL, tf), lambda i, j: (0, j)),
                      pl.BlockSpec((D_MODEL, tf), lambda i, j: (0, j)),
                      pl.BlockSpec((tf, D_MODEL), lambda i, j: (j, 0))]
        out_shape += [jax.ShapeDtypeStruct((D_MODEL, D_FF), BF16),
                      jax.ShapeDtypeStruct((D_MODEL, D_FF), BF16),
                      jax.ShapeDtypeStruct((D_FF, D_MODEL), BF16)]
    if side is not None:
        q, kn, vn, cache_kt, cache_vt, cw_in, cw_out = side
        nb, _, _, wb_len = cache_kt.shape
        assert all(wb_len % d == 0 and w <= wb_len for w, d in DIL_PATTERNS)
        hs = SATT_HEADS_PER_STEP
        groups = H_ATT // hs
        n_blocks = nb * groups
        assert ni * nf >= n_blocks and D_MODEL % ni == 0 and (2 * D_FF) % nf == 0
        sblk = lambda i, j: jnp.minimum(i * nf + j, n_blocks - 1)
        seq_row = pl.BlockSpec((1, H_ATT, HD_ATT), lambda i, j: (sblk(i, j) // groups, 0, 0))
        cache = pl.BlockSpec((1, hs, HD_ATT, wb_len),
                             lambda i, j: (sblk(i, j) // groups, sblk(i, j) % groups, 0, 0))
        cwi = pl.BlockSpec((D_MODEL // ni, 2 * D_FF // nf), lambda i, j: (i, j))
        cwo = pl.BlockSpec((D_FF // nf, D_MODEL // ni), lambda i, j: (j, i))
        in_specs += [seq_row, seq_row, seq_row, cache, cache, cwi, cwo]
        out_specs += [seq_row, cwi, cwo]
        out_shape += [jax.ShapeDtypeStruct((nb, H_ATT, HD_ATT), F32),
                      jax.ShapeDtypeStruct(cw_in.shape, BF16),
                      jax.ShapeDtypeStruct(cw_out.shape, BF16)]
        scratch.append(pltpu.VMEM((hs, wb_len), F32))
        operands += (q, kn, vn, cache_kt, cache_vt, cw_in, cw_out)
        body = functools.partial(_ffn_side_kernel, final_norm=final_norm, n_side_blocks=n_blocks)
        sem = ("arbitrary", "arbitrary")
    outs = pl.pallas_call(
        body,
        grid=(ni, nf),
        in_specs=in_specs,
        out_specs=out_specs,
        out_shape=out_shape,
        scratch_shapes=scratch,
        compiler_params=_params(*sem),
        name="ffn",
    )(*operands)
    return outs if (emit_bf16 or side is not None) else outs[0]


def _ffn_two_phase_kernel(x_ref, g_ref, wa_ref, wb_ref, wo_ref, gf_ref, o_ref, h_ref, act_ref,
                          *, nf, n_out, final_norm):
    s = pl.program_id(1)
    tn = wo_ref.shape[1]

    @pl.when(s == 0)
    def _():
        h_ref[...] = (_rms(x_ref[...]) * g_ref[...]).astype(BF16)

    @pl.when(s < nf)
    def _():
        h = h_ref[...]
        a = _dot(h, wa_ref[...])
        b = _dot(h, wb_ref[...])
        act_ref[s] = (a * jax.nn.sigmoid(a) * b).astype(BF16)

    for n in range(n_out):
        @pl.when(s == nf + n)
        def _():
            cols = slice(n * tn, (n + 1) * tn)
            act = jnp.concatenate([act_ref[t] for t in range(nf)], axis=1)
            o_ref[:, cols] = x_ref[:, cols] + 0.5 * _dot(act, wo_ref[...])

    if final_norm:
        @pl.when(s == nf + n_out - 1)
        def _():
            o_ref[...] = _rms(o_ref[...]) * gf_ref[...]


def _ffn_two_phase(x, g, w_in16, wo, g_final, *, tm, tf, tn, final_norm):
    m = x.shape[0]
    nf = D_FF // tf
    n_out = D_MODEL // tn
    fcol = lambda s: jnp.minimum(s, nf - 1)
    ocol = lambda s: jnp.clip(s - nf, 0, n_out - 1)
    return pl.pallas_call(
        functools.partial(_ffn_two_phase_kernel, nf=nf, n_out=n_out, final_norm=final_norm),
        grid=(m // tm, nf + n_out),
        in_specs=[
            pl.BlockSpec((tm, D_MODEL), lambda i, s: (i, 0)),
            pl.BlockSpec((1, D_MODEL), lambda i, s: (0, 0)),
            pl.BlockSpec((D_MODEL, tf), lambda i, s: (0, fcol(s))),
            pl.BlockSpec((D_MODEL, tf), lambda i, s: (0, fcol(s) + nf)),
            pl.BlockSpec((D_FF, tn), lambda i, s: (0, ocol(s))),
            pl.BlockSpec((1, D_MODEL), lambda i, s: (0, 0)),
        ],
        out_specs=pl.BlockSpec((tm, D_MODEL), lambda i, s: (i, 0)),
        out_shape=jax.ShapeDtypeStruct((m, D_MODEL), F32),
        scratch_shapes=[pltpu.VMEM((tm, D_MODEL), BF16), pltpu.VMEM((nf, tm, tf), BF16)],
        compiler_params=_params("parallel", "arbitrary"),
        name="ffn_two_phase",
    )(x, g, w_in16, w_in16, wo, g_final)


def _proj_kernel(x_ref, g_ref, w_ref, wlr_ref, wg2_ref, bg2_ref, p_ref, la_ref, *rest, emit_bf16):
    h_ref = rest[-1]
    j = pl.program_id(1)

    @pl.when(j == 0)
    def _():
        h = (_rms(x_ref[...]) * g_ref[...]).astype(BF16)
        h_ref[...] = h
        lr = _dot_nt(h, wlr_ref[...])
        z = _dot(lr.astype(BF16), wg2_ref[...]) + bg2_ref[...]
        log_sig = jnp.minimum(z, 0.0) - jnp.log1p(jnp.exp(-jnp.abs(z)))
        la_ref[...] = log_sig * (1.0 / GATE_TAU)

    w = w_ref[...]
    if emit_bf16:
        w = w.astype(BF16)
        rest[0][...] = w
    p_ref[...] = _dot_nt(h_ref[...], w)


def _proj(x, g, w_t, w_lr_t, w_g2, b_g2, *, tm, tn, emit_bf16=False):
    m = x.shape[0]
    out_specs = [
        pl.BlockSpec((tm, tn), lambda i, j: (i, j)),
        pl.BlockSpec((tm, D_GLA_K), lambda i, j: (i, 0)),
    ]
    out_shape = [
        jax.ShapeDtypeStruct((m, D_PROJ_MAIN), F32),
        jax.ShapeDtypeStruct((m, D_GLA_K), F32),
    ]
    if emit_bf16:
        assert m == tm
        out_specs.append(pl.BlockSpec((tn, D_MODEL), lambda i, j: (j, 0)))
        out_shape.append(jax.ShapeDtypeStruct((D_PROJ_MAIN, D_MODEL), BF16))
    return pl.pallas_call(
        functools.partial(_proj_kernel, emit_bf16=emit_bf16),
        grid=(m // tm, D_PROJ_MAIN // tn),
        in_specs=[
            pl.BlockSpec((tm, D_MODEL), lambda i, j: (i, 0)),
            pl.BlockSpec((1, D_MODEL), lambda i, j: (0, 0)),
            pl.BlockSpec((tn, D_MODEL), lambda i, j: (j, 0)),
            pl.BlockSpec((GATE_RANK, D_MODEL), lambda i, j: (0, 0)),
            pl.BlockSpec((GATE_RANK, D_GLA_K), lambda i, j: (0, 0)),
            pl.BlockSpec((1, D_GLA_K), lambda i, j: (0, 0)),
        ],
        out_specs=out_specs,
        out_shape=out_shape,
        scratch_shapes=[pltpu.VMEM((tm, D_MODEL), BF16)],
        compiler_params=_params("parallel", "arbitrary"),
        name="proj",
    )(x, g, w_t, w_lr_t, w_g2, b_g2)


def _attn_kernel(q_ref, k_ref, v_ref, o_ref, kt_ref, vt_ref,
                 qp_ref, kp_ref, vp_ref, m_ref, l_ref, acc_ref, s_ref, ms_ref):
    blk = ATT_BLOCK
    for src_ref, dst_ref in ((k_ref, kt_ref), (v_ref, vt_ref)):
        t = src_ref[...].T
        dst_ref[0, 0] = t[:HD_ATT]
        dst_ref[0, 1] = t[HD_ATT:]

    scale = HD_ATT ** -0.5
    seq = q_ref.shape[0]
    ns = DIL_PATTERNS[-1][1]
    lane_lo = lax.broadcasted_iota(jnp.int32, (1, 2 * HD_ATT), 1) < HD_ATT
    neg = jnp.float32(-jnp.inf)

    assert ns == 16
    for src_ref, tmp_ref, dst_ref in ((q_ref, m_ref, qp_ref), (k_ref, l_ref, kp_ref), (v_ref, acc_ref, vp_ref)):
        for r4 in range(4):
            x = src_ref[pl.ds(r4, seq // 4, stride=4), :]
            tmp_ref[pl.ds(r4 * (seq // 4), seq // 4), :] = x * scale if src_ref is q_ref else x
        for r4 in range(4):
            for a in range(4):
                dst_ref[pl.ds((4 * a + r4) * blk, blk), :] = tmp_ref[pl.ds(r4 * (seq // 4) + a, blk, stride=4), :]

    def run_pattern(d, first_pattern):
        na = ns // d
        plen = blk // na
        per_stream = seq // (d * blk)

        def offset(idx):
            return na * (idx & (plen - 1)) + idx // plen

        qpos = offset(lax.broadcasted_iota(jnp.int32, (blk, blk), 0))
        kpos = offset(lax.broadcasted_iota(jnp.int32, (blk, blk), 1))
        bias_cur = jnp.where(kpos <= qpos, 0.0, neg)
        if per_stream > 1:
            bias_prev = jnp.where(kpos >= qpos, 0.0, neg)
            bias_band = jnp.concatenate([bias_prev, bias_cur], axis=1)
            bias_first = jnp.concatenate([jnp.full((blk, blk), neg, F32), bias_cur], axis=1)

        def pieces(rd, n):
            return [pl.ds((a * d + rd) * blk + plen * n, plen) for a in range(na)]

        def gather(ref, ps):
            return jnp.concatenate([ref[p, :] for p in ps], axis=0)

        nk = 2 * blk if per_stream > 1 else blk
        heads = (lane_lo, jnp.logical_not(lane_lo))

        def keys(ref, rd, n):
            cur = gather(ref, pieces(rd, n))
            if per_stream == 1:
                return cur.astype(BF16)
            prev = gather(ref, pieces(rd, max(n - 1, 0)))
            return jnp.concatenate([prev, cur], axis=0).astype(BF16)

        def group(g):
            blocks = [((g * ATT_GROUP + b) % d, (g * ATT_GROUP + b) // d) for b in range(ATT_GROUP)]
            for b, (rd, n) in enumerate(blocks):
                q = gather(qp_ref, pieces(rd, n))
                kk = keys(kp_ref, rd, n)
                bias = bias_cur if per_stream == 1 else (bias_band if n > 0 else bias_first)
                for h, sel in enumerate(heads):
                    qh = jnp.where(sel, q, 0.0).astype(BF16)
                    s_ref[b, h, :, :nk] = _dot_nt(qh, kk) + bias
            for b in range(ATT_GROUP):
                for h in range(2):
                    m = jnp.max(s_ref[b, h, :, :nk], axis=-1, keepdims=True)
                    ms_ref[b, h] = jnp.broadcast_to(m, (blk, blk))
            for b, (rd, n) in enumerate(blocks):
                vv = jnp.concatenate([keys(vp_ref, rd, n), jnp.ones((nk, blk), BF16)], axis=1)
                res = []
                for h in range(2):
                    mh = ms_ref[b, h]
                    mh = jnp.concatenate([mh, mh], axis=1) if nk == 2 * blk else mh
                    res.append(_dot(jnp.exp(s_ref[b, h, :, :nk] - mh).astype(BF16), vv))
                u = jnp.where(lane_lo, res[0][:, :blk], res[1][:, :blk])
                l = jnp.where(lane_lo, res[0][:, blk:], res[1][:, blk:])
                m = jnp.where(lane_lo, ms_ref[b, 0], ms_ref[b, 1])
                for a, p in enumerate(pieces(rd, n)):
                    sl = slice(a * plen, (a + 1) * plen)
                    if first_pattern:
                        m_ref[p, :] = m[sl]
                        l_ref[p, :] = l[sl]
                        acc_ref[p, :] = u[sl]
                    else:
                        m_old = m_ref[p, :]
                        m_new = jnp.maximum(m_old, m[sl])
                        a_old = jnp.exp(m_old - m_new)
                        a_blk = jnp.exp(m[sl] - m_new)
                        m_ref[p, :] = m_new
                        l_ref[p, :] = a_old * l_ref[p, :] + a_blk * l[sl]
                        acc_ref[p, :] = a_old * acc_ref[p, :] + a_blk * u[sl]

        for g in range(d * per_stream // ATT_GROUP):
            group(g)

    for idx, (_, d) in enumerate(DIL_PATTERNS):
        run_pattern(d, idx == 0)

    for r4 in range(4):
        for a in range(4):
            rows = pl.ds((4 * a + r4) * blk, blk)
            qp_ref[pl.ds(r4 * (seq // 4) + a, blk, stride=4), :] = acc_ref[rows, :] / l_ref[rows, :]
    for r4 in range(4):
        o_ref[pl.ds(r4, seq // 4, stride=4), :] = qp_ref[pl.ds(r4 * (seq // 4), seq // 4), :]


def _attn_prompt(p, batch, seq):
    ns = DIL_PATTERNS[-1][1]
    assert seq == ns * ATT_BLOCK
    assert all(w // d == ATT_BLOCK and ns % d == 0 and ATT_BLOCK * d // ns >= 8 for w, d in DIL_PATTERNS)
    lanes = 2 * HD_ATT
    n_pairs = H_ATT // 2
    return pl.pallas_call(
        _attn_kernel,
        grid=(batch, n_pairs),
        in_specs=[
            pl.BlockSpec((seq, lanes), lambda b, h: (b, COL_QA // lanes + h)),
            pl.BlockSpec((seq, lanes), lambda b, h: (b, COL_KA // lanes + h)),
            pl.BlockSpec((seq, lanes), lambda b, h: (b, COL_VA // lanes + h)),
        ],
        out_specs=[
            pl.BlockSpec((seq, lanes), lambda b, h: (b, h)),
            pl.BlockSpec((1, 2, HD_ATT, seq), lambda b, h: (b, h, 0, 0)),
            pl.BlockSpec((1, 2, HD_ATT, seq), lambda b, h: (b, h, 0, 0)),
        ],
        out_shape=[
            jax.ShapeDtypeStruct((batch * seq, D_ATT), F32),
            jax.ShapeDtypeStruct((batch, H_ATT, HD_ATT, seq), F32),
            jax.ShapeDtypeStruct((batch, H_ATT, HD_ATT, seq), F32),
        ],
        scratch_shapes=[pltpu.VMEM((seq, lanes), F32)] * 6 + [
            pltpu.VMEM((ATT_GROUP, 2, ATT_BLOCK, 2 * ATT_BLOCK), F32),
            pltpu.VMEM((ATT_GROUP, 2, ATT_BLOCK, lanes), F32),
        ],
        compiler_params=_params("parallel", "parallel"),
        name="attn_prompt",
    )(p, p, p)


_GLA_LEVELS = (32, 16, 8, 4, 2, 1)


def _gla_exponent_matrix():
    c = GLA_CHUNK
    t = np.arange(c)[:, None]
    u = np.arange(c)[None, :]
    mats = [(u <= t), (u > t)]
    for h in _GLA_LEVELS:
        mid = (t // (2 * h)) * (2 * h) + h - 1
        upper = (t % (2 * h)) >= h
        mats.append(np.where(upper, (u > mid) & (u <= t), (u > t) & (u <= mid)))
    tmat = np.concatenate(mats, axis=0).astype(np.float32)
    return np.concatenate([tmat, tmat], axis=1)


def _gla_kernel(q_ref, k_ref, v_ref, g_ref, t_ref, o_ref, s_ref, st_ref, f_ref, a_ref, kv_ref):
    c = GLA_CHUNK
    scale = DK_GLA ** -0.5
    ti = lax.broadcasted_iota(jnp.int32, (c, c), 0)
    si = lax.broadcasted_iota(jnp.int32, (c, c), 1)
    txs = ti ^ si
    below = ti > si
    level_masks = [below & (txs >= h) & (txs < 2 * h) for h in _GLA_LEVELS]
    diag = ti == si

    st_ref[...] = jnp.zeros_like(st_ref)

    def group(gi, carry):
        rows = [pl.ds(pl.multiple_of((gi * GLA_GROUP + i) * c, c), c) for i in range(GLA_GROUP)]
        for i, r in enumerate(rows):
            g = g_ref[r, :]
            g_hi = g.astype(BF16)
            g_lo = (g - g_hi.astype(F32)).astype(BF16)
            f_ref[i] = jnp.exp(_dot(t_ref[...], jnp.concatenate([g_hi, g_lo], axis=0)))
        for i, r in enumerate(rows):
            q = q_ref[r, :] * scale
            k = k_ref[r, :]
            a = jnp.where(diag, _dot_nt(q.astype(BF16), k.astype(BF16)), 0.0)
            for lvl in range(len(_GLA_LEVELS)):
                fl = f_ref[i, 2 * c + lvl * c: 3 * c + lvl * c, :]
                a = a + jnp.where(level_masks[lvl],
                                  _dot_nt((q * fl).astype(BF16), (k * fl).astype(BF16)), 0.0)
            a_ref[i] = a.astype(BF16)
        for i, r in enumerate(rows):
            v = v_ref[r, :].astype(BF16)
            kd = (k_ref[r, :] * f_ref[i, c:2 * c, :]).astype(BF16)
            kv_ref[i] = _dot_tn(v, kd)
            o_ref[r, :] = _dot(a_ref[i], v)
        for i, r in enumerate(rows):
            st = st_ref[...]
            qe = (q_ref[r, :] * scale * f_ref[i, 0:c, :]).astype(BF16)
            o_ref[r, :] += _dot_nt(qe, st.astype(BF16))
            st_ref[...] = st * f_ref[i, c - 1:c, :] + kv_ref[i]
        return carry

    lax.fori_loop(0, q_ref.shape[0] // (c * GLA_GROUP), group, 0)
    s_ref[0, 0] = st_ref[...].T


def _gla_prompt(p, la, tmat, batch, seq):
    return pl.pallas_call(
        _gla_kernel,
        grid=(batch, H_GLA),
        in_specs=[
            pl.BlockSpec((seq, DK_GLA), lambda b, h: (b, COL_QG // DK_GLA + h)),
            pl.BlockSpec((seq, DK_GLA), lambda b, h: (b, COL_KG // DK_GLA + h)),
            pl.BlockSpec((seq, DV_GLA), lambda b, h: (b, COL_VG // DV_GLA + h)),
            pl.BlockSpec((seq, DK_GLA), lambda b, h: (b, h)),
            pl.BlockSpec(tmat.shape, lambda b, h: (0, 0)),
        ],
        out_specs=[
            pl.BlockSpec((seq, DV_GLA), lambda b, h: (b, h)),
            pl.BlockSpec((1, 1, DK_GLA, DV_GLA), lambda b, h: (b, h, 0, 0)),
        ],
        out_shape=[
            jax.ShapeDtypeStruct((batch * seq, D_GLA_V), F32),
            jax.ShapeDtypeStruct((batch, H_GLA, DK_GLA, DV_GLA), F32),
        ],
        scratch_shapes=[pltpu.VMEM((DV_GLA, DK_GLA), F32),
                        pltpu.VMEM((GLA_GROUP,) + (tmat.shape[0], DK_GLA), F32),
                        pltpu.VMEM((GLA_GROUP, GLA_CHUNK, GLA_CHUNK), BF16),
                        pltpu.VMEM((GLA_GROUP, DV_GLA, DK_GLA), F32)],
        compiler_params=_params("parallel", "parallel"),
        name="gla_prompt",
    )(p, p, p, la, tmat)


def _merge_kernel(x_ref, oa_ref, og_ref, rg_ref, ga_ref, gg_ref, w_ref, o_ref):
    a = (_rms(oa_ref[...]) * ga_ref[...]).astype(BF16)
    parts = []
    for h in range(H_GLA):
        cols = slice(h * DV_GLA, (h + 1) * DV_GLA)
        r = rg_ref[:, cols]
        parts.append((_rms(og_ref[:, cols]) * gg_ref[...] * (r * jax.nn.sigmoid(r))).astype(BF16))
    gg = jnp.concatenate(parts, axis=-1)
    o_ref[...] = x_ref[...] + _dot(a, w_ref[0:D_ATT, :]) + _dot(gg, w_ref[D_ATT:, :])


def _merge(x, o_att, o_gla, p, g_att, g_gla, w_out, *, tm):
    m = x.shape[0]
    return pl.pallas_call(
        _merge_kernel,
        grid=(m // tm,),
        in_specs=[
            pl.BlockSpec((tm, D_MODEL), lambda i: (i, 0)),
            pl.BlockSpec((tm, D_ATT), lambda i: (i, 0)),
            pl.BlockSpec((tm, D_GLA_V), lambda i: (i, 0)),
            pl.BlockSpec((tm, D_GLA_V), lambda i: (i, COL_RG // D_GLA_V)),
            pl.BlockSpec((1, D_ATT), lambda i: (0, 0)),
            pl.BlockSpec((1, DV_GLA), lambda i: (0, 0)),
            pl.BlockSpec((D_MODEL, D_MODEL), lambda i: (0, 0)),
        ],
        out_specs=pl.BlockSpec((tm, D_MODEL), lambda i: (i, 0)),
        out_shape=jax.ShapeDtypeStruct((m, D_MODEL), F32),
        compiler_params=_params("parallel"),
        name="merge",
    )(x, o_att, o_gla, p, g_att, g_gla, w_out)


SATT_HEADS_PER_STEP = 4


def _sattn_scores(q, k_ref, s_ref):
    q16 = (q * HD_ATT ** -0.5).astype(BF16)
    for h in range(k_ref.shape[1]):
        s_ref[h:h + 1, :] = _dot(q16, k_ref[0, h].astype(BF16))[h:h + 1, :]


def _sattn_finish(q, kn, vn, v_ref, s_ref):
    scale = HD_ATT ** -0.5
    wb = v_ref.shape[-1]
    t = lax.broadcasted_iota(jnp.int32, (1, wb), 1)
    cnt = jnp.zeros((1, wb), F32)
    for w, d in DIL_PATTERNS:
        cnt = cnt + jnp.where((t >= wb - w) & ((t & (d - 1)) == 0), 1.0, 0.0)
    bias = jnp.where(cnt > 0.0, 0.0, jnp.float32(-jnp.inf))
    n_pat = float(len(DIL_PATTERNS))
    n_heads = v_ref.shape[1]
    s_new = jnp.sum(q * scale * kn, axis=-1, keepdims=True)
    s = s_ref[...] + bias
    m = jnp.maximum(jnp.max(s, axis=-1, keepdims=True), s_new)
    e = cnt * jnp.exp(s - m)
    e_new = n_pat * jnp.exp(s_new - m)
    den = jnp.sum(e, axis=-1, keepdims=True) + e_new
    e16 = e.astype(BF16)
    head = lax.broadcasted_iota(jnp.int32, (n_heads, HD_ATT), 0)
    num = e_new * vn
    for h in range(n_heads):
        num = num + jnp.where(head == h, _dot_nt(e16, v_ref[0, h].astype(BF16)), 0.0)
    return num / den


SGLA_SEQS_PER_STEP = 4
_SPLIT = 3


def _bf16_pieces(x):
    pieces = []
    for _ in range(_SPLIT):
        p = x.astype(BF16)
        pieces.append(p)
        x = x - p.astype(F32)
    return pieces


def _sgla_kernel(q_ref, k_ref, g_ref, v_ref, s_ref, so_ref, o_ref):
    scale = DK_GLA ** -0.5
    n_vec = 3
    sel_rows = 16
    r = lax.broadcasted_iota(jnp.int32, (sel_rows, n_vec * DV_GLA), 0)
    c = lax.broadcasted_iota(jnp.int32, (sel_rows, n_vec * DV_GLA), 1)
    selector = jnp.where((r // _SPLIT == c // DV_GLA) & (r < n_vec * _SPLIT), 1.0, 0.0).astype(BF16)
    pad = jnp.zeros((sel_rows - n_vec * _SPLIT, DK_GLA), BF16)
    for s in range(q_ref.shape[0]):
        for h in range(H_GLA):
            hrow = slice(h, h + 1)
            vecs = (jnp.exp(g_ref[s, hrow, :]), k_ref[s, hrow, :], q_ref[s, hrow, :] * scale)
            lhs = jnp.concatenate([p for x in vecs for p in _bf16_pieces(x)] + [pad], axis=0)
            cols = _dot_tn(lhs, selector)
            decay, kcol, qcol = (cols[:, i * DV_GLA:(i + 1) * DV_GLA] for i in range(n_vec))
            s_new = decay * s_ref[s, h] + kcol * v_ref[s, hrow, :]
            so_ref[s, h] = s_new
            o_ref[s, hrow, :] = jnp.sum(qcol * s_new, axis=0, keepdims=True)


def _gla_sample(q, k, g, v, state):
    nb = state.shape[0]
    bs = SGLA_SEQS_PER_STEP
    krow = pl.BlockSpec((bs, H_GLA, DK_GLA), lambda b: (b, 0, 0))
    vrow = pl.BlockSpec((bs, H_GLA, DV_GLA), lambda b: (b, 0, 0))
    st = pl.BlockSpec((bs, H_GLA, DK_GLA, DV_GLA), lambda b: (b, 0, 0, 0))
    return pl.pallas_call(
        _sgla_kernel,
        grid=(nb // bs,),
        in_specs=[krow, krow, krow, vrow, st],
        out_specs=[st, vrow],
        out_shape=[jax.ShapeDtypeStruct((nb, H_GLA, DK_GLA, DV_GLA), F32),
                   jax.ShapeDtypeStruct((nb, H_GLA, DV_GLA), F32)],
        compiler_params=_params("parallel"),
        name="gla_sample",
    )(q, k, g, v, state)


def kernel(x_prompt, x_sample, cache_att_k, cache_att_v, state_gla, g_ffn1, w_ffn1_in, w_ffn1_out, g_mix, w_in, w_gate2, b_gate2, g_att_out, g_gla_out, w_out, g_ffn2, w_ffn2_in, w_ffn2_out, g_final):
    depth = w_in.shape[0]
    assert depth == 1
    batch, seq, _ = x_prompt.shape
    nb, dec_seq, _ = x_sample.shape
    assert dec_seq == 1
    xp = x_prompt.reshape(batch * seq, D_MODEL)
    xs = x_sample.reshape(nb, D_MODEL)
    row = lambda a: a.reshape(1, -1)
    l = 0

    w_in_t = w_in[l].T
    w_lr = w_in_t[D_PROJ_MAIN:].astype(BF16)
    w_g2 = w_gate2[l].astype(BF16)
    b_g2 = row(b_gate2[l])
    w_o = w_out[l].astype(BF16)
    gf = row(g_final)
    tmat = jnp.asarray(_gla_exponent_matrix(), dtype=BF16)

    xs, w1a, w1b, w1o = _ffn(xs, row(g_ffn1[l]), w_ffn1_in[l], w_ffn1_in[l], w_ffn1_out[l], gf,
                             tm=nb, tf=FFN_TILE, final_norm=False, emit_bf16=True)
    ps, las, w_main = _proj(xs, row(g_mix[l]), w_in_t, w_lr, w_g2, b_g2, tm=nb, tn=512, emit_bf16=True)
    heads = lambda c0: ps[:, c0:c0 + D_ATT].reshape(nb, H_ATT, HD_ATT)
    k_new, v_new = heads(COL_KA), heads(COL_VA)
    xp, o_att_s, w2i, w2o = _ffn(
        xp, row(g_ffn1[l]), w1a, w1b, w1o, gf, tm=FFN_ROWS_SIDE, tf=FFN_TILE_SIDE, final_norm=False,
        side=(heads(COL_QA), k_new, v_new,
              jnp.transpose(cache_att_k[l], (0, 2, 3, 1)), jnp.transpose(cache_att_v[l], (0, 2, 3, 1)),
              w_ffn2_in[l], w_ffn2_out[l]))
    krows = lambda a: a.reshape(nb, H_GLA, DK_GLA)
    s_new, o_gla_s = _gla_sample(
        krows(ps[:, COL_QG:COL_QG + D_GLA_K]), krows(ps[:, COL_KG:COL_KG + D_GLA_K]), krows(las),
        ps[:, COL_VG:COL_VG + D_GLA_V].reshape(nb, H_GLA, DV_GLA), state_gla[l])
    xs = _merge(xs, o_att_s.reshape(nb, D_ATT), o_gla_s.reshape(nb, D_GLA_V), ps,
                row(g_att_out[l]), row(g_gla_out[l]), w_o, tm=nb)
    ys = _ffn(xs, row(g_ffn2[l]), w2i, w2i, w2o, gf, tm=nb, tf=FFN_TILE, final_norm=True)
    nk_s = k_new.reshape(1, nb, 1, H_ATT, HD_ATT)
    nv_s = v_new.reshape(1, nb, 1, H_ATT, HD_ATT)

    pp, lap = _proj(xp, row(g_mix[l]), w_main, w_lr, w_g2, b_g2, tm=1024, tn=1024)
    o_att, kt_p, vt_p = _attn_prompt(pp, batch, seq)
    o_gla, s_fin = _gla_prompt(pp, lap, tmat, batch, seq)
    xp = _merge(xp, o_att, o_gla, pp, row(g_att_out[l]), row(g_gla_out[l]), w_o, tm=512)
    yp = _ffn_two_phase(xp, row(g_ffn2[l]), w2i, w2o, gf, tm=512, tf=512, tn=512, final_norm=True)
    nk_p = jnp.transpose(kt_p, (0, 3, 1, 2))[None]
    nv_p = jnp.transpose(vt_p, (0, 3, 1, 2))[None]

    return (yp.reshape(batch, seq, D_MODEL), ys.reshape(nb, 1, D_MODEL), nk_p, nv_p,
            s_fin[None], nk_s, nv_s, s_new[None])
```

```python
import functools

import numpy as np
import jax
import jax.numpy as jnp
from jax import lax
from jax.experimental import pallas as pl
from jax.experimental.pallas import tpu as pltpu

F32 = jnp.float32
BF16 = jnp.bfloat16

D_MODEL = 2048
D_FF = 5632
D_ATT = 1024
HD_ATT = 64
H_ATT = 16
H_GLA = 4
DK_GLA = 128
DV_GLA = 256
D_GLA_K = H_GLA * DK_GLA
D_GLA_V = H_GLA * DV_GLA
GATE_RANK = 16
GATE_TAU = 16.0
NORM_EPS = 1e-6
DIL_PATTERNS = ((128, 1), (512, 4), (2048, 16))
ATT_BLOCK = 128
ATT_GROUP = 16
GLA_CHUNK = 64
GLA_GROUP = 16
D_PROJ_MAIN = 3 * D_ATT + 2 * D_GLA_K + 2 * D_GLA_V

COL_QA, COL_KA, COL_VA = 0, D_ATT, 2 * D_ATT
COL_QG = 3 * D_ATT
COL_KG = COL_QG + D_GLA_K
COL_VG = COL_KG + D_GLA_K
COL_RG = COL_VG + D_GLA_V

VMEM_LIMIT_BYTES = 56 * 1024 * 1024

FFN_ROWS, FFN_TILE_WIDE_ROWS = 1024, 256
FFN_ROWS_SIDE, FFN_TILE_SIDE = FFN_ROWS, FFN_TILE_WIDE_ROWS
FFN_TILE = 512


def _rms(x):
    return x * lax.rsqrt(jnp.mean(x * x, axis=-1, keepdims=True) + NORM_EPS)


def _dot(a, b):
    return jnp.dot(a, b, preferred_element_type=F32)


def _dot_nt(a, b):
    return lax.dot_general(a, b, (((1,), (1,)), ((), ())), preferred_element_type=F32)


def _dot_tn(a, b):
    return lax.dot_general(a, b, (((0,), (0,)), ((), ())), preferred_element_type=F32)


def _params(*sem):
    return pltpu.CompilerParams(dimension_semantics=sem, vmem_limit_bytes=VMEM_LIMIT_BYTES)


def _ffn_step(x_ref, g_ref, wa, wb, wo, gf_ref, o_ref, h_ref, final_norm, side_jobs=(None, None)):
    j = pl.program_id(1)

    @pl.when(j == 0)
    def _():
        h_ref[...] = (_rms(x_ref[...]) * g_ref[...]).astype(BF16)
        o_ref[...] = jnp.zeros_like(o_ref)

    h = h_ref[...]
    a = _dot(h, wa)
    if side_jobs[0] is not None:
        side_jobs[0]()
    b = _dot(h, wb)
    if side_jobs[1] is not None:
        side_jobs[1]()
    act = (a * jax.nn.sigmoid(a) * b).astype(BF16)
    o_ref[...] += _dot(act, wo)

    @pl.when(j == pl.num_programs(1) - 1)
    def _():
        y = x_ref[...] + 0.5 * o_ref[...]
        if final_norm:
            y = _rms(y) * gf_ref[...]
        o_ref[...] = y


def _ffn_kernel(x_ref, g_ref, wa_ref, wb_ref, wo_ref, gf_ref, o_ref, *rest, final_norm, emit_bf16):
    wa, wb, wo = wa_ref[...], wb_ref[...], wo_ref[...]
    if emit_bf16:
        wa, wb, wo = wa.astype(BF16), wb.astype(BF16), wo.astype(BF16)
        for dst_ref, w in zip(rest[:3], (wa, wb, wo)):
            dst_ref[...] = w
    _ffn_step(x_ref, g_ref, wa, wb, wo, gf_ref, o_ref, rest[-1], final_norm)


def _ffn_side_kernel(x_ref, g_ref, wa_ref, wb_ref, wo_ref, gf_ref,
                     q_ref, kn_ref, vn_ref, k_ref, v_ref, cwi_ref, cwo_ref,
                     o_ref, so_ref, cwi16_ref, cwo16_ref, h_ref, s_ref, *, final_norm, n_side_blocks):
    hs = k_ref.shape[1]
    groups = H_ATT // hs
    step = pl.program_id(0) * pl.num_programs(1) + pl.program_id(1)
    blk = jnp.minimum(step, n_side_blocks - 1)
    heads = pl.ds(pl.multiple_of((blk % groups) * hs, hs), hs)

    def scores():
        _sattn_scores(q_ref[0, heads, :], k_ref, s_ref)
        cwi16_ref[...] = cwi_ref[...].astype(BF16)
        cwo16_ref[...] = cwo_ref[...].astype(BF16)

    def finish():
        so_ref[0, heads, :] = _sattn_finish(q_ref[0, heads, :], kn_ref[0, heads, :], vn_ref[0, heads, :],
                                            v_ref, s_ref)

    _ffn_step(x_ref, g_ref, wa_ref[...], wb_ref[...], wo_ref[...], gf_ref, o_ref, h_ref, final_norm,
              (scores, finish))


def _ffn(x, g, wa, wb, wo, g_final, *, tm, tf, final_norm, emit_bf16=False, side=None):
    m = x.shape[0]
    nf = D_FF // tf
    ni = m // tm
    b_off = nf if wb.shape[1] == 2 * D_FF else 0
    in_specs = [
        pl.BlockSpec((tm, D_MODEL), lambda i, j: (i, 0)),
        pl.BlockSpec((1, D_MODEL), lambda i, j: (0, 0)),
        pl.BlockSpec((D_MODEL, tf), lambda i, j: (0, j)),
        pl.BlockSpec((D_MODEL, tf), lambda i, j: (0, j + b_off)),
        pl.BlockSpec((tf, D_MODEL), lambda i, j: (j, 0)),
        pl.BlockSpec((1, D_MODEL), lambda i, j: (0, 0)),
    ]
    out_specs = [pl.BlockSpec((tm, D_MODEL), lambda i, j: (i, 0))]
    out_shape = [jax.ShapeDtypeStruct((m, D_MODEL), F32)]
    scratch = [pltpu.VMEM((tm, D_MODEL), BF16)]
    operands = (x, g, wa, wb, wo, g_final)
    if side is None:
        body = functools.partial(_ffn_kernel, final_norm=final_norm, emit_bf16=emit_bf16)
        sem = ("parallel", "arbitrary")
    if emit_bf16:
        assert m == tm and side is None
        out_specs += [pl.BlockSpec((D_MODEL, tf), lambda i, j: (0, j)),
                      pl.BlockSpec((D_MODEL, tf), lambda i, j: (0, j)),
                      pl.BlockSpec((tf, D_MODEL), lambda i, j: (j, 0))]
        out_shape += [jax.ShapeDtypeStruct((D_MODEL, D_FF), BF16),
                      jax.ShapeDtypeStruct((D_MODEL, D_FF), BF16),
                      jax.ShapeDtypeStruct((D_FF, D_MODEL), BF16)]
    if side is not None:
        q, kn, vn, cache_kt, cache_vt, cw_in, cw_out = side
        nb, _, _, wb_len = cache_kt.shape
        assert all(wb_len % d == 0 and w <= wb_len for w, d in DIL_PATTERNS)
        hs = SATT_HEADS_PER_STEP
        groups = H_ATT // hs
        n_blocks = nb * groups
        assert ni * nf >= n_blocks and D_MODEL % ni == 0 and (2 * D_FF) % nf == 0
        sblk = lambda i, j: jnp.minimum(i * nf + j, n_blocks - 1)
        seq_row = pl.BlockSpec((1, H_ATT, HD_ATT), lambda i, j: (sblk(i, j) // groups, 0, 0))
        cache = pl.BlockSpec((1, hs, HD_ATT, wb_len),
                             lambda i, j: (sblk(i, j) // groups, sblk(i, j) % groups, 0, 0))
        cwi = pl.BlockSpec((D_MODEL // ni, 2 * D_FF // nf), lambda i, j: (i, j))
        cwo = pl.BlockSpec((D_FF // nf, D_MODEL // ni), lambda i, j: (j, i))
        in_specs += [seq_row, seq_row, seq_row, cache, cache, cwi, cwo]
        out_specs += [seq_row, cwi, cwo]
        out_shape += [jax.ShapeDtypeStruct((nb, H_ATT, HD_ATT), F32),
                      jax.ShapeDtypeStruct(cw_in.shape, BF16),
                      jax.ShapeDtypeStruct(cw_out.shape, BF16)]
        scratch.append(pltpu.VMEM((hs, wb_len), F32))
        operands += (q, kn, vn, cache_kt, cache_vt, cw_in, cw_out)
        body = functools.partial(_ffn_side_kernel, final_norm=final_norm, n_side_blocks=n_blocks)
        sem = ("arbitrary", "arbitrary")
    outs = pl.pallas_call(
        body,
        grid=(ni, nf),
        in_specs=in_specs,
        out_specs=out_specs,
        out_shape=out_shape,
        scratch_shapes=scratch,
        compiler_params=_params(*sem),
        name="ffn",
    )(*operands)
    return outs if (emit_bf16 or side is not None) else outs[0]


def _proj_kernel(x_ref, g_ref, w_ref, wlr_ref, wg2_ref, bg2_ref, p_ref, la_ref, *rest, emit_bf16):
    h_ref = rest[-1]
    j = pl.program_id(1)

    @pl.when(j == 0)
    def _():
        h = (_rms(x_ref[...]) * g_ref[...]).astype(BF16)
        h_ref[...] = h
        lr = _dot_nt(h, wlr_ref[...])
        z = _dot(lr.astype(BF16), wg2_ref[...]) + bg2_ref[...]
        log_sig = jnp.minimum(z, 0.0) - jnp.log1p(jnp.exp(-jnp.abs(z)))
        la_ref[...] = log_sig * (1.0 / GATE_TAU)

    w = w_ref[...]
    if emit_bf16:
        w = w.astype(BF16)
        rest[0][...] = w
    p_ref[...] = _dot_nt(h_ref[...], w)


def _proj(x, g, w_t, w_lr_t, w_g2, b_g2, *, tm, tn, emit_bf16=False):
    m = x.shape[0]
    out_specs = [
        pl.BlockSpec((tm, tn), lambda i, j: (i, j)),
        pl.BlockSpec((tm, D_GLA_K), lambda i, j: (i, 0)),
    ]
    out_shape = [
        jax.ShapeDtypeStruct((m, D_PROJ_MAIN), F32),
        jax.ShapeDtypeStruct((m, D_GLA_K), F32),
    ]
    if emit_bf16:
        assert m == tm
        out_specs.append(pl.BlockSpec((tn, D_MODEL), lambda i, j: (j, 0)))
        out_shape.append(jax.ShapeDtypeStruct((D_PROJ_MAIN, D_MODEL), BF16))
    return pl.pallas_call(
        functools.partial(_proj_kernel, emit_bf16=emit_bf16),
        grid=(m // tm, D_PROJ_MAIN // tn),
        in_specs=[
            pl.BlockSpec((tm, D_MODEL), lambda i, j: (i, 0)),
            pl.BlockSpec((1, D_MODEL), lambda i, j: (0, 0)),
            pl.BlockSpec((tn, D_MODEL), lambda i, j: (j, 0)),
            pl.BlockSpec((GATE_RANK, D_MODEL), lambda i, j: (0, 0)),
            pl.BlockSpec((GATE_RANK, D_GLA_K), lambda i, j: (0, 0)),
            pl.BlockSpec((1, D_GLA_K), lambda i, j: (0, 0)),
        ],
        out_specs=out_specs,
        out_shape=out_shape,
        scratch_shapes=[pltpu.VMEM((tm, D_MODEL), BF16)],
        compiler_params=_params("parallel", "arbitrary"),
        name="proj",
    )(x, g, w_t, w_lr_t, w_g2, b_g2)


def _attn_kernel(q_ref, k_ref, v_ref, o_ref, kt_ref, vt_ref,
                 qp_ref, kp_ref, vp_ref, m_ref, l_ref, acc_ref, s_ref, ms_ref):
    blk = ATT_BLOCK
    for src_ref, dst_ref in ((k_ref, kt_ref), (v_ref, vt_ref)):
        t = src_ref[...].T
        dst_ref[0, 0] = t[:HD_ATT]
        dst_ref[0, 1] = t[HD_ATT:]

    scale = HD_ATT ** -0.5
    seq = q_ref.shape[0]
    ns = DIL_PATTERNS[-1][1]
    lane_lo = lax.broadcasted_iota(jnp.int32, (1, 2 * HD_ATT), 1) < HD_ATT
    neg = jnp.float32(-jnp.inf)

    assert ns == 16
    for src_ref, tmp_ref, dst_ref in ((q_ref, m_ref, qp_ref), (k_ref, l_ref, kp_ref), (v_ref, acc_ref, vp_ref)):
        for r4 in range(4):
            x = src_ref[pl.ds(r4, seq // 4, stride=4), :]
            tmp_ref[pl.ds(r4 * (seq // 4), seq // 4), :] = x * scale if src_ref is q_ref else x
        for r4 in range(4):
            for a in range(4):
                dst_ref[pl.ds((4 * a + r4) * blk, blk), :] = tmp_ref[pl.ds(r4 * (seq // 4) + a, blk, stride=4), :]

    def run_pattern(d, first_pattern):
        na = ns // d
        plen = blk // na
        per_stream = seq // (d * blk)

        def offset(idx):
            return na * (idx & (plen - 1)) + idx // plen

        qpos = offset(lax.broadcasted_iota(jnp.int32, (blk, blk), 0))
        kpos = offset(lax.broadcasted_iota(jnp.int32, (blk, blk), 1))
        bias_cur = jnp.where(kpos <= qpos, 0.0, neg)
        if per_stream > 1:
            bias_prev = jnp.where(kpos >= qpos, 0.0, neg)
            bias_band = jnp.concatenate([bias_prev, bias_cur], axis=1)
            bias_first = jnp.concatenate([jnp.full((blk, blk), neg, F32), bias_cur], axis=1)

        def pieces(rd, n):
            return [pl.ds((a * d + rd) * blk + plen * n, plen) for a in range(na)]

        def gather(ref, ps):
            return jnp.concatenate([ref[p, :] for p in ps], axis=0)

        nk = 2 * blk if per_stream > 1 else blk
        heads = (lane_lo, jnp.logical_not(lane_lo))

        def keys(ref, rd, n):
            cur = gather(ref, pieces(rd, n))
            if per_stream == 1:
                return cur.astype(BF16)
            prev = gather(ref, pieces(rd, max(n - 1, 0)))
            return jnp.concatenate([prev, cur], axis=0).astype(BF16)

        def group(g):
            blocks = [((g * ATT_GROUP + b) % d, (g * ATT_GROUP + b) // d) for b in range(ATT_GROUP)]
            for b, (rd, n) in enumerate(blocks):
                q = gather(qp_ref, pieces(rd, n))
                kk = keys(kp_ref, rd, n)
                bias = bias_cur if per_stream == 1 else (bias_band if n > 0 else bias_first)
                for h, sel in enumerate(heads):
                    qh = jnp.where(sel, q, 0.0).astype(BF16)
                    s_ref[b, h, :, :nk] = _dot_nt(qh, kk) + bias
            for b in range(ATT_GROUP):
                for h in range(2):
                    m = jnp.max(s_ref[b, h, :, :nk], axis=-1, keepdims=True)
                    ms_ref[b, h] = jnp.broadcast_to(m, (blk, blk))
            for b, (rd, n) in enumerate(blocks):
                vv = jnp.concatenate([keys(vp_ref, rd, n), jnp.ones((nk, blk), BF16)], axis=1)
                res = []
                for h in range(2):
                    mh = ms_ref[b, h]
                    mh = jnp.concatenate([mh, mh], axis=1) if nk == 2 * blk else mh
                    res.append(_dot(jnp.exp(s_ref[b, h, :, :nk] - mh).astype(BF16), vv))
                u = jnp.where(lane_lo, res[0][:, :blk], res[1][:, :blk])
                l = jnp.where(lane_lo, res[0][:, blk:], res[1][:, blk:])
                m = jnp.where(lane_lo, ms_ref[b, 0], ms_ref[b, 1])
                for a, p in enumerate(pieces(rd, n)):
                    sl = slice(a * plen, (a + 1) * plen)
                    if first_pattern:
                        m_ref[p, :] = m[sl]
                        l_ref[p, :] = l[sl]
                        acc_ref[p, :] = u[sl]
                    else:
                        m_old = m_ref[p, :]
                        m_new = jnp.maximum(m_old, m[sl])
                        a_old = jnp.exp(m_old - m_new)
                        a_blk = jnp.exp(m[sl] - m_new)
                        m_ref[p, :] = m_new
                        l_ref[p, :] = a_old * l_ref[p, :] + a_blk * l[sl]
                        acc_ref[p, :] = a_old * acc_ref[p, :] + a_blk * u[sl]

        for g in range(d * per_stream // ATT_GROUP):
            group(g)

    for idx, (_, d) in enumerate(DIL_PATTERNS):
        run_pattern(d, idx == 0)

    for r4 in range(4):
        for a in range(4):
            rows = pl.ds((4 * a + r4) * blk, blk)
            qp_ref[pl.ds(r4 * (seq // 4) + a, blk, stride=4), :] = acc_ref[rows, :] / l_ref[rows, :]
    for r4 in range(4):
        o_ref[pl.ds(r4, seq // 4, stride=4), :] = qp_ref[pl.ds(r4 * (seq // 4), seq // 4), :]


def _attn_prompt(p, batch, seq):
    ns = DIL_PATTERNS[-1][1]
    assert seq == ns * ATT_BLOCK
    assert all(w // d == ATT_BLOCK and ns % d == 0 and ATT_BLOCK * d // ns >= 8 for w, d in DIL_PATTERNS)
    lanes = 2 * HD_ATT
    n_pairs = H_ATT // 2
    return pl.pallas_call(
        _attn_kernel,
        grid=(batch, n_pairs),
        in_specs=[
            pl.BlockSpec((seq, lanes), lambda b, h: (b, COL_QA // lanes + h)),
            pl.BlockSpec((seq, lanes), lambda b, h: (b, COL_KA // lanes + h)),
            pl.BlockSpec((seq, lanes), lambda b, h: (b, COL_VA // lanes + h)),
        ],
        out_specs=[
            pl.BlockSpec((seq, lanes), lambda b, h: (b, h)),
            pl.BlockSpec((1, 2, HD_ATT, seq), lambda b, h: (b, h, 0, 0)),
            pl.BlockSpec((1, 2, HD_ATT, seq), lambda b, h: (b, h, 0, 0)),
        ],
        out_shape=[
            jax.ShapeDtypeStruct((batch * seq, D_ATT), F32),
            jax.ShapeDtypeStruct((batch, H_ATT, HD_ATT, seq), F32),
            jax.ShapeDtypeStruct((batch, H_ATT, HD_ATT, seq), F32),
        ],
        scratch_shapes=[pltpu.VMEM((seq, lanes), F32)] * 6 + [
            pltpu.VMEM((ATT_GROUP, 2, ATT_BLOCK, 2 * ATT_BLOCK), F32),
            pltpu.VMEM((ATT_GROUP, 2, ATT_BLOCK, lanes), F32),
        ],
        compiler_params=_params("parallel", "parallel"),
        name="attn_prompt",
    )(p, p, p)


_GLA_LEVELS = (32, 16, 8, 4, 2, 1)


def _gla_exponent_matrix():
    c = GLA_CHUNK
    t = np.arange(c)[:, None]
    u = np.arange(c)[None, :]
    mats = [(u <= t), (u > t)]
    for h in _GLA_LEVELS:
        mid = (t // (2 * h)) * (2 * h) + h - 1
        upper = (t % (2 * h)) >= h
        mats.append(np.where(upper, (u > mid) & (u <= t), (u > t) & (u <= mid)))
    tmat = np.concatenate(mats, axis=0).astype(np.float32)
    return np.concatenate([tmat, tmat], axis=1)


def _gla_kernel(q_ref, k_ref, v_ref, g_ref, t_ref, o_ref, s_ref, st_ref, f_ref, a_ref, kv_ref):
    c = GLA_CHUNK
    scale = DK_GLA ** -0.5
    ti = lax.broadcasted_iota(jnp.int32, (c, c), 0)
    si = lax.broadcasted_iota(jnp.int32, (c, c), 1)
    txs = ti ^ si
    below = ti > si
    level_masks = [below & (txs >= h) & (txs < 2 * h) for h in _GLA_LEVELS]
    diag = ti == si

    st_ref[...] = jnp.zeros_like(st_ref)

    def group(gi, carry):
        rows = [pl.ds(pl.multiple_of((gi * GLA_GROUP + i) * c, c), c) for i in range(GLA_GROUP)]
        for i, r in enumerate(rows):
            g = g_ref[r, :]
            g_hi = g.astype(BF16)
            g_lo = (g - g_hi.astype(F32)).astype(BF16)
            f_ref[i] = jnp.exp(_dot(t_ref[...], jnp.concatenate([g_hi, g_lo], axis=0)))
        for i, r in enumerate(rows):
            q = q_ref[r, :] * scale
            k = k_ref[r, :]
            a = jnp.where(diag, _dot_nt(q.astype(BF16), k.astype(BF16)), 0.0)
            for lvl in range(len(_GLA_LEVELS)):
                fl = f_ref[i, 2 * c + lvl * c: 3 * c + lvl * c, :]
                a = a + jnp.where(level_masks[lvl],
                                  _dot_nt((q * fl).astype(BF16), (k * fl).astype(BF16)), 0.0)
            a_ref[i] = a.astype(BF16)
        for i, r in enumerate(rows):
            v = v_ref[r, :].astype(BF16)
            kd = (k_ref[r, :] * f_ref[i, c:2 * c, :]).astype(BF16)
            kv_ref[i] = _dot_tn(v, kd)
            o_ref[r, :] = _dot(a_ref[i], v)
        for i, r in enumerate(rows):
            st = st_ref[...]
            qe = (q_ref[r, :] * scale * f_ref[i, 0:c, :]).astype(BF16)
            o_ref[r, :] += _dot_nt(qe, st.astype(BF16))
            st_ref[...] = st * f_ref[i, c - 1:c, :] + kv_ref[i]
        return carry

    lax.fori_loop(0, q_ref.shape[0] // (c * GLA_GROUP), group, 0)
    s_ref[0, 0] = st_ref[...].T


def _gla_prompt(p, la, tmat, batch, seq):
    return pl.pallas_call(
        _gla_kernel,
        grid=(batch, H_GLA),
        in_specs=[
            pl.BlockSpec((seq, DK_GLA), lambda b, h: (b, COL_QG // DK_GLA + h)),
            pl.BlockSpec((seq, DK_GLA), lambda b, h: (b, COL_KG // DK_GLA + h)),
            pl.BlockSpec((seq, DV_GLA), lambda b, h: (b, COL_VG // DV_GLA + h)),
            pl.BlockSpec((seq, DK_GLA), lambda b, h: (b, h)),
            pl.BlockSpec(tmat.shape, lambda b, h: (0, 0)),
        ],
        out_specs=[
            pl.BlockSpec((seq, DV_GLA), lambda b, h: (b, h)),
            pl.BlockSpec((1, 1, DK_GLA, DV_GLA), lambda b, h: (b, h, 0, 0)),
        ],
        out_shape=[
            jax.ShapeDtypeStruct((batch * seq, D_GLA_V), F32),
            jax.ShapeDtypeStruct((batch, H_GLA, DK_GLA, DV_GLA), F32),
        ],
        scratch_shapes=[pltpu.VMEM((DV_GLA, DK_GLA), F32),
                        pltpu.VMEM((GLA_GROUP,) + (tmat.shape[0], DK_GLA), F32),
                        pltpu.VMEM((GLA_GROUP, GLA_CHUNK, GLA_CHUNK), BF16),
                        pltpu.VMEM((GLA_GROUP, DV_GLA, DK_GLA), F32)],
        compiler_params=_params("parallel", "parallel"),
        name="gla_prompt",
    )(p, p, p, la, tmat)


def _merge_kernel(x_ref, oa_ref, og_ref, rg_ref, ga_ref, gg_ref, w_ref, o_ref):
    a = (_rms(oa_ref[...]) * ga_ref[...]).astype(BF16)
    parts = []
    for h in range(H_GLA):
        cols = slice(h * DV_GLA, (h + 1) * DV_GLA)
        r = rg_ref[:, cols]
        parts.append((_rms(og_ref[:, cols]) * gg_ref[...] * (r * jax.nn.sigmoid(r))).astype(BF16))
    gg = jnp.concatenate(parts, axis=-1)
    o_ref[...] = x_ref[...] + _dot(a, w_ref[0:D_ATT, :]) + _dot(gg, w_ref[D_ATT:, :])


def _merge(x, o_att, o_gla, p, g_att, g_gla, w_out, *, tm):
    m = x.shape[0]
    return pl.pallas_call(
        _merge_kernel,
        grid=(m // tm,),
        in_specs=[
            pl.BlockSpec((tm, D_MODEL), lambda i: (i, 0)),
            pl.BlockSpec((tm, D_ATT), lambda i: (i, 0)),
            pl.BlockSpec((tm, D_GLA_V), lambda i: (i, 0)),
            pl.BlockSpec((tm, D_GLA_V), lambda i: (i, COL_RG // D_GLA_V)),
            pl.BlockSpec((1, D_ATT), lambda i: (0, 0)),
            pl.BlockSpec((1, DV_GLA), lambda i: (0, 0)),
            pl.BlockSpec((D_MODEL, D_MODEL), lambda i: (0, 0)),
        ],
        out_specs=pl.BlockSpec((tm, D_MODEL), lambda i: (i, 0)),
        out_shape=jax.ShapeDtypeStruct((m, D_MODEL), F32),
        compiler_params=_params("parallel"),
        name="merge",
    )(x, o_att, o_gla, p, g_att, g_gla, w_out)


SATT_HEADS_PER_STEP = 4


def _sattn_scores(q, k_ref, s_ref):
    q16 = (q * HD_ATT ** -0.5).astype(BF16)
    for h in range(k_ref.shape[1]):
        s_ref[h:h + 1, :] = _dot(q16, k_ref[0, h].astype(BF16))[h:h + 1, :]


def _sattn_finish(q, kn, vn, v_ref, s_ref):
    scale = HD_ATT ** -0.5
    wb = v_ref.shape[-1]
    t = lax.broadcasted_iota(jnp.int32, (1, wb), 1)
    cnt = jnp.zeros((1, wb), F32)
    for w, d in DIL_PATTERNS:
        cnt = cnt + jnp.where((t >= wb - w) & ((t & (d - 1)) == 0), 1.0, 0.0)
    bias = jnp.where(cnt > 0.0, 0.0, jnp.float32(-jnp.inf))
    n_pat = float(len(DIL_PATTERNS))
    n_heads = v_ref.shape[1]
    s_new = jnp.sum(q * scale * kn, axis=-1, keepdims=True)
    s = s_ref[...] + bias
    m = jnp.maximum(jnp.max(s, axis=-1, keepdims=True), s_new)
    e = cnt * jnp.exp(s - m)
    e_new = n_pat * jnp.exp(s_new - m)
    den = jnp.sum(e, axis=-1, keepdims=True) + e_new
    e16 = e.astype(BF16)
    head = lax.broadcasted_iota(jnp.int32, (n_heads, HD_ATT), 0)
    num = e_new * vn
    for h in range(n_heads):
        num = num + jnp.where(head == h, _dot_nt(e16, v_ref[0, h].astype(BF16)), 0.0)
    return num / den


SGLA_SEQS_PER_STEP = 4
_SPLIT = 3


def _bf16_pieces(x):
    pieces = []
    for _ in range(_SPLIT):
        p = x.astype(BF16)
        pieces.append(p)
        x = x - p.astype(F32)
    return pieces


def _sgla_kernel(q_ref, k_ref, g_ref, v_ref, s_ref, so_ref, o_ref):
    scale = DK_GLA ** -0.5
    n_vec = 3
    sel_rows = 16
    r = lax.broadcasted_iota(jnp.int32, (sel_rows, n_vec * DV_GLA), 0)
    c = lax.broadcasted_iota(jnp.int32, (sel_rows, n_vec * DV_GLA), 1)
    selector = jnp.where((r // _SPLIT == c // DV_GLA) & (r < n_vec * _SPLIT), 1.0, 0.0).astype(BF16)
    pad = jnp.zeros((sel_rows - n_vec * _SPLIT, DK_GLA), BF16)
    for s in range(q_ref.shape[0]):
        for h in range(H_GLA):
            hrow = slice(h, h + 1)
            vecs = (jnp.exp(g_ref[s, hrow, :]), k_ref[s, hrow, :], q_ref[s, hrow, :] * scale)
            lhs = jnp.concatenate([p for x in vecs for p in _bf16_pieces(x)] + [pad], axis=0)
            cols = _dot_tn(lhs, selector)
            decay, kcol, qcol = (cols[:, i * DV_GLA:(i + 1) * DV_GLA] for i in range(n_vec))
            s_new = decay * s_ref[s, h] + kcol * v_ref[s, hrow, :]
            so_ref[s, h] = s_new
            o_ref[s, hrow, :] = jnp.sum(qcol * s_new, axis=0, keepdims=True)


def _gla_sample(q, k, g, v, state):
    nb = state.shape[0]
    bs = SGLA_SEQS_PER_STEP
    krow = pl.BlockSpec((bs, H_GLA, DK_GLA), lambda b: (b, 0, 0))
    vrow = pl.BlockSpec((bs, H_GLA, DV_GLA), lambda b: (b, 0, 0))
    st = pl.BlockSpec((bs, H_GLA, DK_GLA, DV_GLA), lambda b: (b, 0, 0, 0))
    return pl.pallas_call(
        _sgla_kernel,
        grid=(nb // bs,),
        in_specs=[krow, krow, krow, vrow, st],
        out_specs=[st, vrow],
        out_shape=[jax.ShapeDtypeStruct((nb, H_GLA, DK_GLA, DV_GLA), F32),
                   jax.ShapeDtypeStruct((nb, H_GLA, DV_GLA), F32)],
        compiler_params=_params("parallel"),
        name="gla_sample",
    )(q, k, g, v, state)


def kernel(x_prompt, x_sample, cache_att_k, cache_att_v, state_gla, g_ffn1, w_ffn1_in, w_ffn1_out, g_mix, w_in, w_gate2, b_gate2, g_att_out, g_gla_out, w_out, g_ffn2, w_ffn2_in, w_ffn2_out, g_final):
    depth = w_in.shape[0]
    assert depth == 1
    batch, seq, _ = x_prompt.shape
    nb, dec_seq, _ = x_sample.shape
    assert dec_seq == 1
    xp = x_prompt.reshape(batch * seq, D_MODEL)
    xs = x_sample.reshape(nb, D_MODEL)
    row = lambda a: a.reshape(1, -1)
    l = 0

    w_in_t = w_in[l].T
    w_lr = w_in_t[D_PROJ_MAIN:].astype(BF16)
    w_g2 = w_gate2[l].astype(BF16)
    b_g2 = row(b_gate2[l])
    w_o = w_out[l].astype(BF16)
    gf = row(g_final)
    tmat = jnp.asarray(_gla_exponent_matrix(), dtype=BF16)

    xs, w1a, w1b, w1o = _ffn(xs, row(g_ffn1[l]), w_ffn1_in[l], w_ffn1_in[l], w_ffn1_out[l], gf,
                             tm=nb, tf=FFN_TILE, final_norm=False, emit_bf16=True)
    ps, las, w_main = _proj(xs, row(g_mix[l]), w_in_t, w_lr, w_g2, b_g2, tm=nb, tn=512, emit_bf16=True)
    heads = lambda c0: ps[:, c0:c0 + D_ATT].reshape(nb, H_ATT, HD_ATT)
    k_new, v_new = heads(COL_KA), heads(COL_VA)
    xp, o_att_s, w2i, w2o = _ffn(
        xp, row(g_ffn1[l]), w1a, w1b, w1o, gf, tm=FFN_ROWS_SIDE, tf=FFN_TILE_SIDE, final_norm=False,
        side=(heads(COL_QA), k_new, v_new,
              jnp.transpose(cache_att_k[l], (0, 2, 3, 1)), jnp.transpose(cache_att_v[l], (0, 2, 3, 1)),
              w_ffn2_in[l], w_ffn2_out[l]))
    krows = lambda a: a.reshape(nb, H_GLA, DK_GLA)
    s_new, o_gla_s = _gla_sample(
        krows(ps[:, COL_QG:COL_QG + D_GLA_K]), krows(ps[:, COL_KG:COL_KG + D_GLA_K]), krows(las),
        ps[:, COL_VG:COL_VG + D_GLA_V].reshape(nb, H_GLA, DV_GLA), state_gla[l])
    xs = _merge(xs, o_att_s.reshape(nb, D_ATT), o_gla_s.reshape(nb, D_GLA_V), ps,
                row(g_att_out[l]), row(g_gla_out[l]), w_o, tm=nb)
    ys = _ffn(xs, row(g_ffn2[l]), w2i, w2i, w2o, gf, tm=nb, tf=FFN_TILE, final_norm=True)
    nk_s = k_new.reshape(1, nb, 1, H_ATT, HD_ATT)
    nv_s = v_new.reshape(1, nb, 1, H_ATT, HD_ATT)

    pp, lap = _proj(xp, row(g_mix[l]), w_main, w_lr, w_g2, b_g2, tm=1024, tn=1024)
    o_att, kt_p, vt_p = _attn_prompt(pp, batch, seq)
    o_gla, s_fin = _gla_prompt(pp, lap, tmat, batch, seq)
    xp = _merge(xp, o_att, o_gla, pp, row(g_att_out[l]), row(g_gla_out[l]), w_o, tm=512)
    yp = _ffn(xp, row(g_ffn2[l]), w2i, w2i, w2o, gf, tm=FFN_ROWS, tf=FFN_TILE_WIDE_ROWS, final_norm=True)
    nk_p = jnp.transpose(kt_p, (0, 3, 1, 2))[None]
    nv_p = jnp.transpose(vt_p, (0, 3, 1, 2))[None]

    return (yp.reshape(batch, seq, D_MODEL), ys.reshape(nb, 1, D_MODEL), nk_p, nv_p,
            s_fin[None], nk_s, nv_s, s_new[None])
```
